```python
import math
import jax, jax.numpy as jnp
from jax import lax
import numpy as np

D_MODEL = 1024
BATCH = 16
SEQ = 4096
DEPTH = 1

CHUNK = 64
CONV_W = 4
D_RNN = (4 * D_MODEL // 3) // 128 * 128
RNN_BLOCK_W = 128
RNN_BLOCKS = D_RNN // RNN_BLOCK_W
RG_C = 8.0
DN_HEAD_DIM = 128
DN_HEADS = D_MODEL // DN_HEAD_DIM
DN_W = DN_HEADS * DN_HEAD_DIM
N_MEM = 256
CA_HEADS = 4
CA_HEAD_DIM = D_MODEL // CA_HEADS
N_GROUPS = 8
EXPERTS_PER_GROUP = 8
N_EXPERTS = N_GROUPS * EXPERTS_PER_GROUP
TOP_K = 2
D_EXPERT = D_MODEL // 2
MOE_BLOCK = 256
EPS = 1e-6
IN_WIDTHS = (D_RNN, D_RNN, 3 * DN_W, DN_W, DN_HEADS, DN_HEADS, D_MODEL, D_MODEL)
D_IN = 2 * D_RNN + 4 * DN_W + 2 * DN_HEADS + 2 * D_MODEL

kernel_name = "hybrid_rglru_gdn_hmoe_block"


def rmsnorm(x, w):
    xf = x.astype(jnp.float32)
    y = xf * lax.rsqrt(jnp.mean(xf * xf, axis=-1, keepdims=True) + EPS)
    return (y * w.astype(jnp.float32)).astype(x.dtype)


def l2norm(x):
    return x * lax.rsqrt(jnp.sum(x * x, axis=-1, keepdims=True) + EPS)


def split_columns(t, widths):
    offsets = np.cumsum(widths)[:-1].tolist()
    return jnp.split(t, offsets, axis=-1)


def causal_dwconv(x, w):
    width = w.shape[0]
    s = x.shape[1]
    xp = jnp.pad(x, ((0, 0), (width - 1, 0), (0, 0)))
    return sum(xp[:, k:k + s] * w[k] for k in range(width))


def rglru(x, wa, ba, wx, bx, lam):
    bsz, s, c = x.shape
    xf = x.astype(jnp.float32)
    xb = xf.reshape(bsz, s, RNN_BLOCKS, RNN_BLOCK_W)
    r = jax.nn.sigmoid(jnp.einsum('bsnk,nkj->bsnj', xb, wa.astype(jnp.float32)).reshape(bsz, s, c) + ba)
    i = jax.nn.sigmoid(jnp.einsum('bsnk,nkj->bsnj', xb, wx.astype(jnp.float32)).reshape(bsz, s, c) + bx)
    log_a = -RG_C * r * jax.nn.softplus(-lam.astype(jnp.float32))
    a = jnp.exp(log_a)
    b = jnp.sqrt(-jnp.expm1(2.0 * log_a)) * (i * xf)

    def combine(left, right):
        a1, b1 = left
        a2, b2 = right
        return a1 * a2, a2 * b1 + b2

    _, h = lax.associative_scan(combine, (a, b), axis=1)
    return h.astype(x.dtype)


def gated_delta_rule(q, k, v, g, beta):
    bsz, s, h, dk = q.shape
    dv = v.shape[-1]
    nc = s // CHUNK
    out_dtype = v.dtype

    def to_chunks(t):
        t = jnp.moveaxis(t.astype(jnp.float32), 2, 1)
        return t.reshape((bsz, h, nc, CHUNK) + t.shape[3:])

    q = to_chunks(q) * (dk ** -0.5)
    k = to_chunks(k)
    v = to_chunks(v)
    beta = to_chunks(beta)
    g = jnp.cumsum(to_chunks(g), axis=-1)
    idx = jnp.arange(CHUNK)
    causal = idx[:, None] >= idx[None, :]
    strict = idx[:, None] > idx[None, :]
    decay = jnp.exp(jnp.where(causal, g[..., :, None] - g[..., None, :], -jnp.inf))
    k_beta = k * beta[..., None]
    v_beta = v * beta[..., None]
    lower = jnp.where(strict, jnp.einsum('bhncd,bhnmd->bhncm', k_beta, k) * decay, 0.0)
    eye = jnp.eye(CHUNK, dtype=jnp.float32)
    t_mat = lax.linalg.triangular_solve(lower + eye, jnp.broadcast_to(eye, lower.shape),
                                        left_side=True, lower=True)
    u = jnp.einsum('bhncm,bhnmv->bhncv', t_mat, v_beta)
    w = jnp.einsum('bhncm,bhnmk->bhnck', t_mat, k_beta * jnp.exp(g)[..., None])
    intra = jnp.where(causal, jnp.einsum('bhncd,bhnmd->bhncm', q, k) * decay, 0.0)
    q_dec = q * jnp.exp(g)[..., None]
    k_dec = k * jnp.exp(g[..., -1:] - g)[..., None]
    g_last = jnp.exp(g[..., -1])

    def step(state, inp):
        u_c, w_c, intra_c, qd_c, kd_c, gl_c = inp
        v_new = u_c - jnp.einsum('bhck,bhkv->bhcv', w_c, state)
        o_c = jnp.einsum('bhck,bhkv->bhcv', qd_c, state) + jnp.einsum('bhcm,bhmv->bhcv', intra_c, v_new)
        state = state * gl_c[..., None, None] + jnp.einsum('bhck,bhcv->bhkv', kd_c, v_new)
        return state, o_c

    xs = tuple(jnp.moveaxis(t, 2, 0) for t in (u, w, intra, q_dec, k_dec, g_last))
    state0 = jnp.zeros((bsz, h, dk, dv), jnp.float32)
    _, o = lax.scan(step, state0, xs)
    o = jnp.moveaxis(o, 0, 2).reshape(bsz, h, s, dv)
    return jnp.transpose(o, (0, 2, 1, 3)).astype(out_dtype)


def memory_cross_attention(u, mem_n, w_q, w_kv, w_o):
    bsz, s, _ = u.shape
    n_mem = mem_n.shape[1]
    q = (u @ w_q).reshape(bsz, s, CA_HEADS, CA_HEAD_DIM)
    kv = (mem_n @ w_kv).reshape(bsz, n_mem, 2, CA_HEADS, CA_HEAD_DIM)
    k, v = kv[:, :, 0], kv[:, :, 1]
    scores = jnp.einsum('bshd,bmhd->bhsm', q.astype(jnp.float32), k.astype(jnp.float32)) * (CA_HEAD_DIM ** -0.5)
    p = jax.nn.softmax(scores, axis=-1)
    o = jnp.einsum('bhsm,bmhd->bshd', p, v.astype(jnp.float32)).astype(u.dtype)
    return o.reshape(bsz, s, CA_HEADS * CA_HEAD_DIM) @ w_o


def hierarchical_moe(u, wg, bg, we, be, w_gate, w_up, w_down):
    bsz, s, d = u.shape
    n = bsz * s
    t = u.reshape(n, d)
    tf = t.astype(jnp.float32)
    grp_logits = tf @ wg.astype(jnp.float32) + bg
    grp_prob = jax.nn.softmax(grp_logits, axis=-1)
    _, grp = lax.top_k(grp_logits, 1)
    p_grp = jnp.take_along_axis(grp_prob, grp, axis=1)
    exp_logits = (tf @ we.astype(jnp.float32) + be).reshape(n, N_GROUPS, EXPERTS_PER_GROUP)
    in_grp = jnp.take_along_axis(exp_logits, grp[:, :, None], axis=1)[:, 0]
    p_in = jax.nn.softmax(in_grp, axis=-1)
    top_p, top_i = lax.top_k(p_in, TOP_K)
    top_p = top_p / jnp.sum(top_p, axis=-1, keepdims=True)
    expert = grp * EXPERTS_PER_GROUP + top_i
    weight = p_grp * top_p

    nk = n * TOP_K
    flat_e = expert.reshape(nk)
    flat_tok = jnp.repeat(jnp.arange(n), TOP_K)
    flat_w = weight.reshape(nk)
    order = jnp.argsort(flat_e)
    se, stok, sw = flat_e[order], flat_tok[order], flat_w[order]
    counts = jnp.bincount(flat_e, length=N_EXPERTS)
    padded = (counts + MOE_BLOCK - 1) // MOE_BLOCK * MOE_BLOCK
    start = jnp.cumsum(counts) - counts
    pend = jnp.cumsum(padded)
    pstart = pend - padded
    dest = pstart[se] + jnp.arange(nk) - start[se]
    n_blocks = (nk + N_EXPERTS * (MOE_BLOCK - 1)) // MOE_BLOCK
    rows = n_blocks * MOE_BLOCK
    xs = jnp.zeros((rows, d), t.dtype).at[dest].set(t[stok])
    block_e = jnp.clip(jnp.searchsorted(pend, jnp.arange(n_blocks) * MOE_BLOCK, side='right'), 0, N_EXPERTS - 1)

    def expert_block(args):
        xb, e = args
        hid = jax.nn.silu(xb @ w_gate[e]) * (xb @ w_up[e])
        return hid @ w_down[e]

    yb = lax.map(expert_block, (xs.reshape(n_blocks, MOE_BLOCK, d), block_e))
    y = yb.reshape(rows, d)[dest] * sw[:, None].astype(t.dtype)
    out = jax.ops.segment_sum(y, stok, num_segments=n)
    return out.reshape(bsz, s, d)


def setup_inputs(seed: int = 0) -> dict:
    key = jax.random.key(seed)
    ks = jax.random.split(key, 40)
    f32 = jnp.float32
    L = DEPTH

    def nrm(k, shape, scale):
        return jax.random.normal(k, shape, f32) * scale

    def gain(k, shape):
        return 1.0 + 0.05 * jax.random.normal(k, shape, f32)

    lam_u = jax.random.uniform(ks[10], (L, D_RNN), f32, 0.9, 0.999)
    lam_a = lam_u ** (1.0 / RG_C)
    dt = jnp.exp(jax.random.uniform(ks[14], (L, DN_HEADS), f32, math.log(1e-3), math.log(1e-1)))
    return {
        "x": nrm(ks[0], (BATCH, SEQ, D_MODEL), 1.0),
        "mem": nrm(ks[1], (BATCH, N_MEM, D_MODEL), 1.0),
        "norm1_w": gain(ks[2], (L, D_MODEL)),
        "w_in": nrm(ks[3], (L, D_MODEL, D_IN), D_MODEL ** -0.5),
        "rnn_conv_w": nrm(ks[4], (L, CONV_W, D_RNN), CONV_W ** -0.5),
        "rnn_conv_b": nrm(ks[5], (L, D_RNN), 0.02),
        "rglru_wa": nrm(ks[6], (L, RNN_BLOCKS, RNN_BLOCK_W, RNN_BLOCK_W), RNN_BLOCK_W ** -0.5),
        "rglru_ba": nrm(ks[7], (L, D_RNN), 0.02),
        "rglru_wx": nrm(ks[8], (L, RNN_BLOCKS, RNN_BLOCK_W, RNN_BLOCK_W), RNN_BLOCK_W ** -0.5),
        "rglru_bx": nrm(ks[9], (L, D_RNN), 0.02),
        "rglru_lambda": jnp.log(lam_a) - jnp.log1p(-lam_a),
        "w_branch_a": nrm(ks[11], (L, D_RNN, D_MODEL), D_RNN ** -0.5),
        "dn_conv_w": nrm(ks[12], (L, CONV_W, 3 * DN_W), CONV_W ** -0.5),
        "dn_a_log": jnp.log(jax.random.uniform(ks[13], (L, DN_HEADS), f32, 1.0, 16.0)),
        "dn_dt_bias": dt + jnp.log(-jnp.expm1(-dt)),
        "dn_norm_w": gain(ks[15], (L, DN_HEAD_DIM)),
        "w_branch_b": nrm(ks[16], (L, DN_W, D_MODEL), DN_W ** -0.5),
        "w_out": nrm(ks[17], (L, D_MODEL, D_MODEL), D_MODEL ** -0.5),
        "norm2_w": gain(ks[18], (L, D_MODEL)),
        "mem_norm_w": gain(ks[19], (L, D_MODEL)),
        "w_cq": nrm(ks[20], (L, D_MODEL, CA_HEADS * CA_HEAD_DIM), D_MODEL ** -0.5),
        "w_ckv": nrm(ks[21], (L, D_MODEL, 2 * CA_HEADS * CA_HEAD_DIM), D_MODEL ** -0.5),
        "w_co": nrm(ks[22], (L, CA_HEADS * CA_HEAD_DIM, D_MODEL), D_MODEL ** -0.5),
        "norm3_w": gain(ks[23], (L, D_MODEL)),
        "w_router_group": nrm(ks[24], (L, D_MODEL, N_GROUPS), D_MODEL ** -0.5),
        "b_router_group": nrm(ks[25], (L, N_GROUPS), 0.01),
        "w_router_expert": nrm(ks[26], (L, D_MODEL, N_EXPERTS), D_MODEL ** -0.5),
        "b_router_expert": nrm(ks[27], (L, N_EXPERTS), 0.01),
        "w_exp_gate": nrm(ks[28], (L, N_EXPERTS, D_MODEL, D_EXPERT), D_MODEL ** -0.5),
        "w_exp_up": nrm(ks[29], (L, N_EXPERTS, D_MODEL, D_EXPERT), D_MODEL ** -0.5),
        "w_exp_down": nrm(ks[30], (L, N_EXPERTS, D_EXPERT, D_MODEL), D_EXPERT ** -0.5),
        "norm_f_w": gain(ks[31], (D_MODEL,)),
    }


def reference(x, mem, norm1_w, w_in, rnn_conv_w, rnn_conv_b, rglru_wa, rglru_ba, rglru_wx, rglru_bx,
              rglru_lambda, w_branch_a, dn_conv_w, dn_a_log, dn_dt_bias, dn_norm_w, w_branch_b, w_out,
              norm2_w, mem_norm_w, w_cq, w_ckv, w_co, norm3_w, w_router_group, b_router_group,
              w_router_expert, b_router_expert, w_exp_gate, w_exp_up, w_exp_down, norm_f_w):
    bsz, s, _ = x.shape
    h = x
    for l in range(DEPTH):
        u = rmsnorm(h, norm1_w[l])
        rx, rg, qkv, z, a_in, b_in, ga, gb = split_columns(u @ w_in[l], IN_WIDTHS)

        rx = causal_dwconv(rx, rnn_conv_w[l]) + rnn_conv_b[l]
        h_rnn = rglru(rx, rglru_wa[l], rglru_ba[l], rglru_wx[l], rglru_bx[l], rglru_lambda[l])
        y_a = (jax.nn.gelu(rg) * h_rnn) @ w_branch_a[l]

        qkv = jax.nn.silu(causal_dwconv(qkv, dn_conv_w[l]))
        q, k, v = jnp.split(qkv, 3, axis=-1)
        q = l2norm(q.reshape(bsz, s, DN_HEADS, DN_HEAD_DIM).astype(jnp.float32))
        k = l2norm(k.reshape(bsz, s, DN_HEADS, DN_HEAD_DIM).astype(jnp.float32))
        v = v.reshape(bsz, s, DN_HEADS, DN_HEAD_DIM)
        g = -jnp.exp(dn_a_log[l].astype(jnp.float32)) * jax.nn.softplus(a_in.astype(jnp.float32) + dn_dt_bias[l])
        beta = jax.nn.sigmoid(b_in.astype(jnp.float32))
        o = gated_delta_rule(q, k, v, g, beta)
        o = rmsnorm(o, dn_norm_w[l]) * jax.nn.silu(z.reshape(bsz, s, DN_HEADS, DN_HEAD_DIM))
        y_b = o.reshape(bsz, s, DN_W) @ w_branch_b[l]

        h = h + (jax.nn.sigmoid(ga) * y_a + jax.nn.sigmoid(gb) * y_b) @ w_out[l]

        h = h + memory_cross_attention(rmsnorm(h, norm2_w[l]), rmsnorm(mem, mem_norm_w[l]),
                                       w_cq[l], w_ckv[l], w_co[l])

        h = h + hierarchical_moe(rmsnorm(h, norm3_w[l]), w_router_group[l], b_router_group[l],
                                 w_router_expert[l], b_router_expert[l],
                                 w_exp_gate[l], w_exp_up[l], w_exp_down[l])
    return rmsnorm(h, norm_f_w)
```

```python
import functools
import math

import jax
import jax.numpy as jnp
from jax import lax
from jax.experimental import pallas as pl
from jax.experimental.pallas import tpu as pltpu

F32 = jnp.float32
BF16 = jnp.bfloat16
I32 = jnp.int32
HIGHEST = lax.Precision.HIGHEST

NORM_EPS = 1e-6
CONV_TAPS = 4
RNN_BLOCK = 128
RG_POWER = 8.0
HEAD_DIM = 128
GDN_CHUNK = 128
CA_HEADS = 4
N_GROUPS = 8
GROUP_SIZE = 8
N_EXPERTS = N_GROUPS * GROUP_SIZE
MOE_BLOCK = 256
LANES = 128
SUBLANES = 8
VMEM_LIMIT = 48 * 1024 * 1024


def _params(*semantics):
    return pltpu.CompilerParams(dimension_semantics=semantics, vmem_limit_bytes=VMEM_LIMIT)


def _dot(a, b):
    return jnp.dot(a.astype(BF16), b.astype(BF16), preferred_element_type=F32)


def _dot_nt(a, b):
    return lax.dot_general(a.astype(BF16), b.astype(BF16), (((1,), (1,)), ((), ())),
                           preferred_element_type=F32)


def _dot_tn(a, b):
    return lax.dot_general(a.astype(BF16), b.astype(BF16), (((0,), (0,)), ((), ())),
                           preferred_element_type=F32)


def _dot_f32(a, b):
    return jnp.dot(a, b, precision=HIGHEST, preferred_element_type=F32)


def _rmsnorm(x, w):
    return x * lax.rsqrt(jnp.mean(x * x, axis=-1, keepdims=True) + NORM_EPS) * w


def _sigmoid(x):
    return 1.0 / (1.0 + jnp.exp(-x))


def _silu(x):
    return x * _sigmoid(x)


def _softplus(x):
    return jnp.maximum(x, 0.0) + jnp.log(1.0 + jnp.exp(-jnp.abs(x)))


def _neg_expm1(y):
    u = jnp.exp(y)
    d = 1.0 - u
    return jnp.where(y > -0.25, jnp.where(d == 0.0, -y, d * y / jnp.log(u)), d)


def _gelu_tanh(x):
    return 0.5 * x * (1.0 + jnp.tanh(math.sqrt(2.0 / math.pi) * (x + 0.044715 * (x * x * x))))


def _norm_mm_kernel(x_ref, nw_ref, w_ref, o_ref, u_ref):
    @pl.when(pl.program_id(1) == 0)
    def _():
        u_ref[...] = _rmsnorm(x_ref[...], nw_ref[...]).astype(BF16)

    o_ref[...] = jnp.dot(u_ref[...], w_ref[...], preferred_element_type=F32).astype(o_ref.dtype)


def _norm_mm(x, nw, w, out_dtype, tm, tn):
    n, d = x.shape
    c = w.shape[1]
    return pl.pallas_call(
        _norm_mm_kernel,
        grid=(n // tm, c // tn),
        in_specs=[pl.BlockSpec((tm, d), lambda i, j: (i, 0)),
                  pl.BlockSpec((1, d), lambda i, j: (0, 0)),
                  pl.BlockSpec((d, tn), lambda i, j: (0, j))],
        out_specs=pl.BlockSpec((tm, tn), lambda i, j: (i, j)),
        out_shape=jax.ShapeDtypeStruct((n, c), out_dtype),
        scratch_shapes=[pltpu.VMEM((tm, d), BF16)],
        compiler_params=_params("parallel", "arbitrary"),
        name="norm_mm",
    )(x, nw, w)


def _load_conv_window(x_ref, xbuf, ts):
    @pl.when(pl.program_id(1) == 0)
    def _():
        xbuf[0:SUBLANES, :] = jnp.zeros((SUBLANES, xbuf.shape[1]), F32)

    @pl.when(pl.program_id(1) != 0)
    def _():
        xbuf[0:SUBLANES, :] = xbuf[ts:ts + SUBLANES, :]

    xbuf[SUBLANES:SUBLANES + ts, :] = x_ref[...]


def _causal_conv(xbuf, cw_ref, ts):
    base = SUBLANES - (CONV_TAPS - 1)
    acc = cw_ref[0:1, :] * xbuf[base:base + ts, :]
    for k in range(1, CONV_TAPS):
        acc = acc + cw_ref[k:k + 1, :] * xbuf[base + k:base + k + ts, :]
    return acc


def _rglru_kernel(rx_ref, rg_ref, cw_ref, cb_ref, wa_ref, ba_ref, wx_ref, bx_ref, lam_ref, o_ref,
                  xbuf, a_ref, b_ref, carry_ref):
    ts, c = rx_ref.shape
    _load_conv_window(rx_ref, xbuf, ts)

    @pl.when(pl.program_id(1) == 0)
    def _():
        carry_ref[...] = jnp.zeros_like(carry_ref)

    xc = _causal_conv(xbuf, cw_ref, ts) + cb_ref[...]
    neg_sp = -RG_POWER * _softplus(-lam_ref[...])
    for n in range(c // RNN_BLOCK):
        sl = slice(n * RNN_BLOCK, (n + 1) * RNN_BLOCK)
        xb = xc[:, sl]
        r = _sigmoid(_dot(xb, wa_ref[n]) + ba_ref[:, sl])
        i = _sigmoid(_dot(xb, wx_ref[n]) + bx_ref[:, sl])
        log_a = neg_sp[:, sl] * r
        a_ref[:, sl] = jnp.exp(log_a)
        b_ref[:, sl] = jnp.sqrt(_neg_expm1(2.0 * log_a)) * (i * xb)

    row = lax.broadcasted_iota(I32, (SUBLANES, c), 0)

    def slab(t, carry):
        rows = pl.ds(pl.multiple_of(t * SUBLANES, SUBLANES), SUBLANES)
        a = a_ref[rows, :]
        b = b_ref[rows, :]
        for d in (1, 2, 4):
            a_sh = jnp.where(row >= d, pltpu.roll(a, d, 0), 1.0)
            b_sh = jnp.where(row >= d, pltpu.roll(b, d, 0), 0.0)
            b = a * b_sh + b
            a = a * a_sh
        h = a * carry + b
        b_ref[rows, :] = h
        return h[SUBLANES - 1:SUBLANES, :]

    carry_ref[...] = lax.fori_loop(0, ts // SUBLANES, slab, carry_ref[...])
    o_ref[...] = (_gelu_tanh(rg_ref[...]) * b_ref[...]).astype(o_ref.dtype)


def _rglru(proj_a, bsz, seq, cw, cb, wa, ba, wx, bx, lam, ts):
    c = cw.shape[1]
    ns = seq // ts
    full = lambda shape: pl.BlockSpec(shape, lambda b, s: (0,) * len(shape))
    return pl.pallas_call(
        _rglru_kernel,
        grid=(bsz, ns),
        in_specs=[pl.BlockSpec((ts, c), lambda b, s: (b * ns + s, 0)),
                  pl.BlockSpec((ts, c), lambda b, s: (b * ns + s, 1)),
                  full(cw.shape), full(cb.shape), full(wa.shape), full(ba.shape),
                  full(wx.shape), full(bx.shape), full(lam.shape)],
        out_specs=pl.BlockSpec((ts, c), lambda b, s: (b * ns + s, 0)),
        out_shape=jax.ShapeDtypeStruct((bsz * seq, c), BF16),
        scratch_shapes=[pltpu.VMEM((ts + SUBLANES, c), F32), pltpu.VMEM((ts, c), F32),
                        pltpu.VMEM((ts, c), F32), pltpu.VMEM((1, c), F32)],
        compiler_params=_params("parallel", "arbitrary"),
        name="rglru",
    )(proj_a, proj_a, cw, cb, wa, ba, wx, bx, lam)


def _lane_bcast(x, lane):
    return jnp.broadcast_to(x[:, lane:lane + 1], x.shape)


def _gdn_kernel(qkv_ref, z_ref, ab_ref, abt_ref, cw_ref, acol_ref, dcol_ref, arow_ref, drow_ref, nw_ref,
                o_ref, xbuf, s_ref):
    ts = qkv_ref.shape[0]
    n_heads = s_ref.shape[0]
    dn_w = n_heads * HEAD_DIM
    _load_conv_window(qkv_ref, xbuf, ts)

    @pl.when(pl.program_id(1) == 0)
    def _():
        s_ref[...] = jnp.zeros_like(s_ref)

    row = lax.broadcasted_iota(I32, (ts, ts), 0)
    col = lax.broadcasted_iota(I32, (ts, ts), 1)
    lower_incl = (row >= col).astype(F32)
    upper_incl = (row <= col).astype(F32)
    eye = (row == col).astype(F32)

    ab = ab_ref[...]
    g_col = -acol_ref[...] * _softplus(ab + dcol_ref[...])
    beta_col = _sigmoid(ab)
    g_row = -arow_ref[...] * _softplus(abt_ref[...] + drow_ref[...])
    cum_col = _dot_f32(lower_incl, g_col)
    cum_row = _dot_f32(g_row, upper_incl)
    last_col = jnp.broadcast_to(cum_col[ts - 1:ts, :], cum_col.shape)
    exp_cum = jnp.exp(cum_col)
    exp_rem = jnp.exp(last_col - cum_col)
    exp_last = jnp.exp(cum_col[ts - 1:ts, :])

    qkv = _silu(_causal_conv(xbuf, cw_ref, ts))

    for h in range(n_heads):
        sl = slice(h * HEAD_DIM, (h + 1) * HEAD_DIM)
        q = qkv[:, sl]
        k = qkv[:, dn_w + h * HEAD_DIM:dn_w + (h + 1) * HEAD_DIM]
        v = qkv[:, 2 * dn_w + h * HEAD_DIM:2 * dn_w + (h + 1) * HEAD_DIM]
        q = q * lax.rsqrt(jnp.sum(q * q, axis=-1, keepdims=True) + NORM_EPS) * (HEAD_DIM ** -0.5)
        k = k * lax.rsqrt(jnp.sum(k * k, axis=-1, keepdims=True) + NORM_EPS)
        beta = _lane_bcast(beta_col, n_heads + h)
        e_cum = _lane_bcast(exp_cum, h)
        e_rem = _lane_bcast(exp_rem, h)
        decay = jnp.exp(jnp.minimum(_lane_bcast(cum_col, h) - cum_row[h:h + 1, :], 0.0))
        k_beta = k * beta
        neg_l = jnp.where(row > col, -(_dot_nt(k_beta, k) * decay), 0.0)
        t_mat = eye + neg_l
        p = neg_l
        for _ in range(int(math.log2(ts)) - 1):
            p = _dot_f32(p, p)
            t_mat = t_mat + _dot_f32(t_mat, p)
        uw = _dot(t_mat, jnp.concatenate([v * beta, k_beta * e_cum], axis=1))
        u, w = uw[:, :HEAD_DIM], uw[:, HEAD_DIM:]
        intra = jnp.where(row >= col, _dot_nt(q, k) * decay, 0.0)
        state = s_ref[h]
        ws_qs = _dot(jnp.concatenate([w, q * e_cum], axis=0), state)
        v_new = u - ws_qs[:ts]
        o = ws_qs[ts:] + _dot(intra, v_new)
        s_ref[h] = state * _lane_bcast(jnp.broadcast_to(exp_last, (HEAD_DIM, LANES)), h) + _dot_tn(k * e_rem, v_new)
        o = _rmsnorm(o, nw_ref[...]) * _silu(z_ref[:, sl])
        o_ref[:, sl] = o.astype(o_ref.dtype)


def _gdn(proj_b, ab, abt, bsz, seq, cw, acol, dcol, arow, drow, nw, n_heads):
    ts = GDN_CHUNK
    ns = seq // ts
    dn_w = n_heads * HEAD_DIM
    full = lambda shape: pl.BlockSpec(shape, lambda b, s: (0,) * len(shape))
    return pl.pallas_call(
        _gdn_kernel,
        grid=(bsz, ns),
        in_specs=[pl.BlockSpec((ts, 3 * dn_w), lambda b, s: (b * ns + s, 0)),
                  pl.BlockSpec((ts, dn_w), lambda b, s: (b * ns + s, 3)),
                  pl.BlockSpec((ts, LANES), lambda b, s: (b * ns + s, 0)),
                  pl.BlockSpec((2 * n_heads, ts), lambda b, s: (0, b * ns + s)),
                  full(cw.shape), full(acol.shape), full(dcol.shape), full(arow.shape), full(drow.shape),
                  full(nw.shape)],
        out_specs=pl.BlockSpec((ts, dn_w), lambda b, s: (b * ns + s, 0)),
        out_shape=jax.ShapeDtypeStruct((bsz * seq, dn_w), BF16),
        scratch_shapes=[pltpu.VMEM((ts + SUBLANES, 3 * dn_w), F32),
                        pltpu.VMEM((n_heads, HEAD_DIM, HEAD_DIM), F32)],
        compiler_params=_params("parallel", "arbitrary"),
        name="gdn",
    )(proj_b, proj_b, ab, abt, cw, acol, dcol, arow, drow, nw)


def _merge_kernel(x_ref, ya_ref, yb_ref, ga_ref, gb_ref, wa_ref, wb_ref, wo_ref, o_ref):
    y_a = jnp.dot(ya_ref[...], wa_ref[...], preferred_element_type=F32)
    y_b = jnp.dot(yb_ref[...], wb_ref[...], preferred_element_type=F32)
    m = _sigmoid(ga_ref[...]) * y_a + _sigmoid(gb_ref[...]) * y_b
    o_ref[...] = x_ref[...] + _dot(m, wo_ref[...])


def _merge(x, gated_a, gated_b, proj_b, w_a, w_b, w_o, tm):
    n, d = x.shape
    full = lambda shape: pl.BlockSpec(shape, lambda i: (0,) * len(shape))
    return pl.pallas_call(
        _merge_kernel,
        grid=(n // tm,),
        in_specs=[pl.BlockSpec((tm, d), lambda i: (i, 0)),
                  pl.BlockSpec((tm, gated_a.shape[1]), lambda i: (i, 0)),
                  pl.BlockSpec((tm, gated_b.shape[1]), lambda i: (i, 0)),
                  pl.BlockSpec((tm, d), lambda i: (i, 4)),
                  pl.BlockSpec((tm, d), lambda i: (i, 5)),
                  full(w_a.shape), full(w_b.shape), full(w_o.shape)],
        out_specs=pl.BlockSpec((tm, d), lambda i: (i, 0)),
        out_shape=jax.ShapeDtypeStruct((n, d), F32),
        compiler_params=_params("parallel"),
        name="merge",
    )(x, gated_a, gated_b, proj_b, proj_b, w_a, w_b, w_o)


def _cross_kernel(h_ref, nw_ref, wq_ref, kv_ref, wo_ref, o_ref):
    x = h_ref[...]
    d = x.shape[1]
    hd = d // CA_HEADS
    u = _rmsnorm(x, nw_ref[...])
    q = _dot(u, wq_ref[...])
    outs = []
    for h in range(CA_HEADS):
        k_h = kv_ref[:, h * hd:(h + 1) * hd]
        v_h = kv_ref[:, d + h * hd:d + (h + 1) * hd]
        s = _dot_nt(q[:, h * hd:(h + 1) * hd], k_h) * (hd ** -0.5)
        s = s - jnp.max(s, axis=-1, keepdims=True)
        e = jnp.exp(s)
        p = e / jnp.sum(e, axis=-1, keepdims=True)
        outs.append(_dot(p, v_h))
    o = jnp.concatenate(outs, axis=1)
    o_ref[...] = x + _dot(o, wo_ref[...])


def _cross(h, nw, w_q, kv, w_o, bsz, seq, n_mem, ts):
    n, d = h.shape
    ns = seq // ts
    full = lambda shape: pl.BlockSpec(shape, lambda b, s: (0,) * len(shape))
    return pl.pallas_call(
        _cross_kernel,
        grid=(bsz, ns),
        in_specs=[pl.BlockSpec((ts, d), lambda b, s: (b * ns + s, 0)),
                  full(nw.shape), full(w_q.shape),
                  pl.BlockSpec((n_mem, 2 * d), lambda b, s: (b, 0)),
                  full(w_o.shape)],
        out_specs=pl.BlockSpec((ts, d), lambda b, s: (b * ns + s, 0)),
        out_shape=jax.ShapeDtypeStruct((n, d), F32),
        compiler_params=_params("parallel", "parallel"),
        name="cross_attn",
    )(h, nw, w_q, kv, w_o)


def _router_kernel(h_ref, nw_ref, wr_ref, br_ref, u_ref, idx_ref, wt_ref, cnt_ref, base_ref):
    t = h_ref.shape[0]

    @pl.when(pl.program_id(0) == 0)
    def _():
        base_ref[...] = jnp.zeros_like(base_ref)

    u = _rmsnorm(h_ref[...], nw_ref[...])
    u_ref[...] = u
    logits = _dot_f32(u, wr_ref[...]) + br_ref[...]
    lane = lax.broadcasted_iota(I32, (t, LANES), 1)
    lanef = lane.astype(F32)
    big = float(LANES)
    neg = -jnp.inf

    lg = jnp.where((lane >= N_EXPERTS) & (lane < N_EXPERTS + N_GROUPS), logits, neg)
    gmax = jnp.max(lg, axis=-1, keepdims=True)
    grp = jnp.min(jnp.where(lg == gmax, lanef - float(N_EXPERTS), big), axis=-1, keepdims=True)
    p_grp = 1.0 / jnp.sum(jnp.exp(lg - gmax), axis=-1, keepdims=True)

    in_grp = (lane < N_EXPERTS) & ((lane // GROUP_SIZE).astype(F32) == grp)
    le = jnp.where(in_grp, logits, neg)
    m1 = jnp.max(le, axis=-1, keepdims=True)
    i1 = jnp.min(jnp.where(le == m1, lanef, big), axis=-1, keepdims=True)
    le2 = jnp.where(lanef == i1, neg, le)
    m2 = jnp.max(le2, axis=-1, keepdims=True)
    i2 = jnp.min(jnp.where(le2 == m2, lanef, big), axis=-1, keepdims=True)
    ratio = jnp.exp(m2 - m1)
    p1 = 1.0 / (1.0 + ratio)
    p2 = ratio * p1

    oh1 = lanef == i1
    oh2 = lanef == i2
    onehot = jnp.where(oh1 | oh2, 1.0, 0.0)
    r_i = lax.broadcasted_iota(I32, (t, t), 0)
    c_i = lax.broadcasted_iota(I32, (t, t), 1)
    before = jnp.where(r_i > c_i, 1.0, 0.0)
    rank = _dot(before, onehot) + base_ref[...]
    r1 = jnp.sum(jnp.where(oh1, rank, 0.0), axis=-1, keepdims=True)
    r2 = jnp.sum(jnp.where(oh2, rank, 0.0), axis=-1, keepdims=True)
    base_ref[...] = base_ref[...] + jnp.sum(onehot, axis=0, keepdims=True)

    idx = jnp.where(lane == 0, i1, jnp.where(lane == 1, i2, jnp.where(lane == 2, r1, jnp.where(lane == 3, r2, 0.0))))
    idx_ref[...] = idx.astype(I32)
    wt_ref[...] = jnp.where(lane == 0, p_grp * p1, jnp.where(lane == 1, p_grp * p2, 0.0))
    cnt_ref[...] = jnp.broadcast_to(base_ref[...], cnt_ref.shape)


def _router(h, nw, wr, br, t):
    n, d = h.shape
    full = lambda shape: pl.BlockSpec(shape, lambda i: (0,) * len(shape))
    return pl.pallas_call(
        _router_kernel,
        grid=(n // t,),
        in_specs=[pl.BlockSpec((t, d), lambda i: (i, 0)), full(nw.shape), full(wr.shape), full(br.shape)],
        out_specs=[pl.BlockSpec((t, d), lambda i: (i, 0)),
                   pl.BlockSpec((t, LANES), lambda i: (i, 0)),
                   pl.BlockSpec((t, LANES), lambda i: (i, 0)),
                   pl.BlockSpec((SUBLANES, LANES), lambda i: (0, 0))],
        out_shape=[jax.ShapeDtypeStruct((n, d), F32),
                   jax.ShapeDtypeStruct((n, LANES), I32),
                   jax.ShapeDtypeStruct((n, LANES), F32),
                   jax.ShapeDtypeStruct((SUBLANES, LANES), F32)],
        scratch_shapes=[pltpu.VMEM((1, LANES), F32)],
        compiler_params=_params("arbitrary"),
        name="router",
    )(h, nw, wr, br)


def _row_copy(src, src_row, dst, dst_row, sem):
    return pltpu.make_async_copy(src.at[pl.ds(src_row, 1)], dst.at[pl.ds(dst_row, 1)], sem)


def _dispatch_kernel(dest_ref, u_ref, xs_in_ref, xs_ref, sem):
    del xs_in_ref
    t = u_ref.shape[0]

    def start(r, carry):
        _row_copy(u_ref, r, xs_ref, dest_ref[0, 0, 2 * r], sem).start()
        _row_copy(u_ref, r, xs_ref, dest_ref[0, 0, 2 * r + 1], sem).start()
        return carry

    def wait(r, carry):
        _row_copy(u_ref, r, xs_ref, dest_ref[0, 0, 2 * r], sem).wait()
        _row_copy(u_ref, r, xs_ref, dest_ref[0, 0, 2 * r + 1], sem).wait()
        return carry

    lax.fori_loop(0, t, start, 0)
    lax.fori_loop(0, t, wait, 0)


def _dispatch(dest3, u, xs_zero, t):
    n, d = u.shape
    return pl.pallas_call(
        _dispatch_kernel,
        grid=(n // t,),
        in_specs=[pl.BlockSpec((1, 1, 2 * t), lambda i: (i, 0, 0), memory_space=pltpu.SMEM),
                  pl.BlockSpec((t, d), lambda i: (i, 0)),
                  pl.BlockSpec(memory_space=pl.ANY)],
        out_specs=pl.BlockSpec(memory_space=pl.ANY),
        out_shape=jax.ShapeDtypeStruct(xs_zero.shape, xs_zero.dtype),
        scratch_shapes=[pltpu.SemaphoreType.DMA(())],
        input_output_aliases={2: 0},
        compiler_params=_params("arbitrary"),
        name="moe_dispatch",
    )(dest3, u, xs_zero)


def _expert_kernel(be_ref, nu_ref, xs_ref, wg_ref, wu_ref, wd_ref, y_ref, wg_bf, wu_bf, wd_bf):
    j = pl.program_id(0)
    used = j < nu_ref[0]
    new_expert = (j == 0) | (be_ref[j] != be_ref[jnp.maximum(j - 1, 0)])

    @pl.when(used & new_expert)
    def _():
        wg_bf[...] = wg_ref[0].astype(BF16)
        wu_bf[...] = wu_ref[0].astype(BF16)
        wd_bf[...] = wd_ref[0].astype(BF16)

    @pl.when(used)
    def _():
        x = xs_ref[...].astype(BF16)
        hid = _silu(jnp.dot(x, wg_bf[...], preferred_element_type=F32)) * jnp.dot(
            x, wu_bf[...], preferred_element_type=F32)
        y_ref[...] = _dot(hid, wd_bf[...])

    @pl.when(jnp.logical_not(used))
    def _():
        y_ref[...] = jnp.zeros_like(y_ref)


def _experts(block_e, n_used, xs, w_gate, w_up, w_down):
    rows, d = xs.shape
    de = w_gate.shape[2]
    grid_spec = pltpu.PrefetchScalarGridSpec(
        num_scalar_prefetch=2,
        grid=(rows // MOE_BLOCK,),
        in_specs=[pl.BlockSpec((MOE_BLOCK, d), lambda j, be, nu: (j, 0)),
                  pl.BlockSpec((1, d, de), lambda j, be, nu: (be[j], 0, 0)),
                  pl.BlockSpec((1, d, de), lambda j, be, nu: (be[j], 0, 0)),
                  pl.BlockSpec((1, de, d), lambda j, be, nu: (be[j], 0, 0))],
        out_specs=pl.BlockSpec((MOE_BLOCK, d), lambda j, be, nu: (j, 0)),
        scratch_shapes=[pltpu.VMEM((d, de), BF16), pltpu.VMEM((d, de), BF16), pltpu.VMEM((de, d), BF16)],
    )
    return pl.pallas_call(
        _expert_kernel,
        grid_spec=grid_spec,
        out_shape=jax.ShapeDtypeStruct((rows, d), F32),
        compiler_params=_params("arbitrary"),
        name="moe_experts",
    )(block_e, n_used, xs, w_gate, w_up, w_down)


def _combine_kernel(dest_ref, h_ref, wt_ref, nw_ref, y_ref, o_ref, buf, sem, *, final_norm):
    t = h_ref.shape[0]

    def start(r, carry):
        _row_copy(y_ref, dest_ref[0, 0, 2 * r], buf.at[0], r, sem).start()
        _row_copy(y_ref, dest_ref[0, 0, 2 * r + 1], buf.at[1], r, sem).start()
        return carry

    def wait(r, carry):
        _row_copy(y_ref, dest_ref[0, 0, 2 * r], buf.at[0], r, sem).wait()
        _row_copy(y_ref, dest_ref[0, 0, 2 * r + 1], buf.at[1], r, sem).wait()
        return carry

    lax.fori_loop(0, t, start, 0)
    lax.fori_loop(0, t, wait, 0)
    wt = wt_ref[...]
    moe = buf[0] * wt[:, 0:1] + buf[1] * wt[:, 1:2]
    out = h_ref[...] + moe
    o_ref[...] = _rmsnorm(out, nw_ref[...]) if final_norm else out


def _combine(dest3, h, wt, nw, y, t, final_norm):
    n, d = h.shape
    return pl.pallas_call(
        functools.partial(_combine_kernel, final_norm=final_norm),
        grid=(n // t,),
        in_specs=[pl.BlockSpec((1, 1, 2 * t), lambda i: (i, 0, 0), memory_space=pltpu.SMEM),
                  pl.BlockSpec((t, d), lambda i: (i, 0)),
                  pl.BlockSpec((t, LANES), lambda i: (i, 0)),
                  pl.BlockSpec((1, d), lambda i: (0, 0)),
                  pl.BlockSpec(memory_space=pl.ANY)],
        out_specs=pl.BlockSpec((t, d), lambda i: (i, 0)),
        out_shape=jax.ShapeDtypeStruct((n, d), F32),
        scratch_shapes=[pltpu.VMEM((2, t, d), F32), pltpu.SemaphoreType.DMA(())],
        compiler_params=_params("arbitrary"),
        name="moe_combine",
    )(dest3, h, wt, nw, y)


def _tile(n, pref):
    return pref if n % pref == 0 else n


def kernel(x, mem, norm1_w, w_in, rnn_conv_w, rnn_conv_b, rglru_wa, rglru_ba, rglru_wx, rglru_bx, rglru_lambda, w_branch_a, dn_conv_w, dn_a_log, dn_dt_bias, dn_norm_w, w_branch_b, w_out, norm2_w, mem_norm_w, w_cq, w_ckv, w_co, norm3_w, w_router_group, b_router_group, w_router_expert, b_router_expert, w_exp_gate, w_exp_up, w_exp_down, norm_f_w):
    bsz, seq, d = x.shape
    n = bsz * seq
    n_mem = mem.shape[1]
    depth = w_in.shape[0]
    d_rnn = rnn_conv_w.shape[2]
    n_heads = dn_a_log.shape[1]
    dn_w = n_heads * HEAD_DIM
    row = lambda v: v.reshape(1, -1).astype(F32)

    h = x.reshape(n, d)
    mem2 = mem.reshape(bsz * n_mem, d)
    tm = _tile(n, 1024)
    for l in range(depth):
        o_rg, o_qkv, o_z = d_rnn, 2 * d_rnn, 2 * d_rnn + 3 * dn_w
        o_a = o_z + dn_w
        o_ga = o_a + 2 * n_heads
        wi = w_in[l]
        w_pa = wi[:, :o_qkv].astype(BF16)
        w_pb = jnp.concatenate([wi[:, o_qkv:o_a], wi[:, o_ga:]], axis=1).astype(BF16)
        w_ab = jnp.pad(wi[:, o_a:o_ga], ((0, 0), (0, LANES - 2 * n_heads))).astype(BF16)
        n1 = row(norm1_w[l])
        proj_a = _norm_mm(h, n1, w_pa, F32, tm, _tile(w_pa.shape[1], 512))
        proj_b = _norm_mm(h, n1, w_pb, F32, tm, _tile(w_pb.shape[1], 512))
        ab = _norm_mm(h, n1, w_ab, F32, tm, LANES)

        gated_a = _rglru(proj_a, bsz, seq, rnn_conv_w[l], row(rnn_conv_b[l]), rglru_wa[l].astype(BF16),
                         row(rglru_ba[l]), rglru_wx[l].astype(BF16), row(rglru_bx[l]), row(rglru_lambda[l]),
                         _tile(seq, 512))

        a_dec = jnp.exp(dn_a_log[l].astype(F32))
        pad_h = lambda v: jnp.pad(v, (0, LANES - n_heads))
        acol, dcol = row(pad_h(a_dec)), row(pad_h(dn_dt_bias[l]))
        arow = jnp.broadcast_to(jnp.pad(a_dec, (0, n_heads))[:, None], (2 * n_heads, GDN_CHUNK))
        drow = jnp.broadcast_to(jnp.pad(dn_dt_bias[l], (0, n_heads))[:, None], (2 * n_heads, GDN_CHUNK))
        abt = ab[:, :2 * n_heads].T
        gated_b = _gdn(proj_b, ab, abt, bsz, seq, dn_conv_w[l], acol, dcol, arow, drow, row(dn_norm_w[l]), n_heads)

        h = _merge(h, gated_a, gated_b, proj_b, w_branch_a[l].astype(BF16), w_branch_b[l].astype(BF16),
                   w_out[l].astype(BF16), _tile(n, 512))

        kv = _norm_mm(mem2, row(mem_norm_w[l]), w_ckv[l].astype(BF16), BF16, _tile(bsz * n_mem, 1024), 512)
        h = _cross(h, row(norm2_w[l]), w_cq[l].astype(BF16), kv, w_co[l].astype(BF16), bsz, seq, n_mem,
                   _tile(seq, 512))

        w_r = jnp.pad(jnp.concatenate([w_router_expert[l], w_router_group[l]], axis=1),
                      ((0, 0), (0, LANES - N_EXPERTS - N_GROUPS)))
        b_r = row(jnp.pad(jnp.concatenate([b_router_expert[l], b_router_group[l]]), (0, LANES - N_EXPERTS - N_GROUPS)))
        u3, idx, wt, cnt = _router(h, row(norm3_w[l]), w_r, b_r, _tile(n, 512))

        counts = cnt[0, :N_EXPERTS].astype(I32)
        padded = (counts + MOE_BLOCK - 1) // MOE_BLOCK * MOE_BLOCK
        pend = jnp.cumsum(padded)
        pstart = pend - padded
        dest = pstart[idx[:, 0:2]] + idx[:, 2:4]
        n_blocks = (2 * n + N_EXPERTS * (MOE_BLOCK - 1)) // MOE_BLOCK
        block_e = jnp.clip(jnp.searchsorted(pend, jnp.arange(n_blocks, dtype=I32) * MOE_BLOCK, side='right'),
                           0, N_EXPERTS - 1).astype(I32)
        n_used = (pend[-1:] // MOE_BLOCK).astype(I32)
        t_moe = _tile(n, 256)
        dest3 = dest.reshape(n // t_moe, 1, 2 * t_moe)
        xs = _dispatch(dest3, u3, jnp.zeros((n_blocks * MOE_BLOCK, d), F32), t_moe)
        yb = _experts(block_e, n_used, xs, w_exp_gate[l], w_exp_up[l], w_exp_down[l])
        h = _combine(dest3, h, wt, row(norm_f_w), yb, t_moe, final_norm=(l == depth - 1))
    return h.reshape(bsz, seq, d)
```

```python
import functools
import math

import jax
import jax.numpy as jnp
from jax import lax
from jax.experimental import pallas as pl
from jax.experimental.pallas import tpu as pltpu

F32 = jnp.float32
BF16 = jnp.bfloat16
I32 = jnp.int32
HIGHEST = lax.Precision.HIGHEST

NORM_EPS = 1e-6
CONV_TAPS = 4
RNN_BLOCK = 128
RG_POWER = 8.0
HEAD_DIM = 128
GDN_CHUNK = 128
CA_HEADS = 4
N_GROUPS = 8
GROUP_SIZE = 8
N_EXPERTS = N_GROUPS * GROUP_SIZE
MOE_BLOCK = 256
LANES = 128
SUBLANES = 8
DMA_UNROLL = 8
VMEM_LIMIT = 48 * 1024 * 1024


def _params(*semantics):
    return pltpu.CompilerParams(dimension_semantics=semantics, vmem_limit_bytes=VMEM_LIMIT)


def _dot(a, b):
    return jnp.dot(a.astype(BF16), b.astype(BF16), preferred_element_type=F32)


def _dot_nt(a, b):
    return lax.dot_general(a.astype(BF16), b.astype(BF16), (((1,), (1,)), ((), ())),
                           preferred_element_type=F32)


def _dot_tn(a, b):
    return lax.dot_general(a.astype(BF16), b.astype(BF16), (((0,), (0,)), ((), ())),
                           preferred_element_type=F32)


def _dot_f32(a, b):
    return jnp.dot(a, b, precision=HIGHEST, preferred_element_type=F32)


def _rmsnorm(x, w):
    return x * lax.rsqrt(jnp.mean(x * x, axis=-1, keepdims=True) + NORM_EPS) * w


def _sigmoid(x):
    return 0.5 * jnp.tanh(0.5 * x) + 0.5


def _silu(x):
    return x * _sigmoid(x)


def _softplus(x):
    return jnp.maximum(x, 0.0) + jnp.log(1.0 + jnp.exp(-jnp.abs(x)))


def _one_minus_exp2(y, exp_y):
    return jnp.tanh(-y) * (1.0 + exp_y * exp_y)


def _gelu_tanh(x):
    return 0.5 * x * (1.0 + jnp.tanh(math.sqrt(2.0 / math.pi) * (x + 0.044715 * (x * x * x))))


def _norm_mm_kernel(x_ref, nw_ref, w_ref, o_ref, u_ref):
    @pl.when(pl.program_id(1) == 0)
    def _():
        u_ref[...] = _rmsnorm(x_ref[...], nw_ref[...]).astype(BF16)

    o_ref[...] = jnp.dot(u_ref[...], w_ref[...], preferred_element_type=F32).astype(o_ref.dtype)


def _norm_mm(x, nw, w, out_dtype, tm, tn):
    n, d = x.shape
    c = w.shape[1]
    return pl.pallas_call(
        _norm_mm_kernel,
        grid=(n // tm, c // tn),
        in_specs=[pl.BlockSpec((tm, d), lambda i, j: (i, 0)),
                  pl.BlockSpec((1, d), lambda i, j: (0, 0)),
                  pl.BlockSpec((d, tn), lambda i, j: (0, j))],
        out_specs=pl.BlockSpec((tm, tn), lambda i, j: (i, j)),
        out_shape=jax.ShapeDtypeStruct((n, c), out_dtype),
        scratch_shapes=[pltpu.VMEM((tm, d), BF16)],
        compiler_params=_params("parallel", "arbitrary"),
        name="norm_mm",
    )(x, nw, w)


def _load_conv_window(x_ref, xbuf, ts):
    @pl.when(pl.program_id(1) == 0)
    def _():
        xbuf[0:SUBLANES, :] = jnp.zeros((SUBLANES, xbuf.shape[1]), F32)

    @pl.when(pl.program_id(1) != 0)
    def _():
        xbuf[0:SUBLANES, :] = xbuf[ts:ts + SUBLANES, :]

    xbuf[SUBLANES:SUBLANES + ts, :] = x_ref[...].astype(F32)


def _causal_conv(xbuf, cw_ref, ts):
    base = SUBLANES - (CONV_TAPS - 1)
    acc = cw_ref[0:1, :] * xbuf[base:base + ts, :]
    for k in range(1, CONV_TAPS):
        acc = acc + cw_ref[k:k + 1, :] * xbuf[base + k:base + k + ts, :]
    return acc


def _rglru_kernel(rx_ref, rg_ref, cw_ref, cb_ref, wa_ref, ba_ref, wx_ref, bx_ref, lam_ref, o_ref,
                  xbuf, a_ref, b_ref, carry_ref):
    ts, c = rx_ref.shape
    _load_conv_window(rx_ref, xbuf, ts)

    @pl.when(pl.program_id(1) == 0)
    def _():
        carry_ref[...] = jnp.zeros_like(carry_ref)

    xc = _causal_conv(xbuf, cw_ref, ts) + cb_ref[...]
    neg_sp = -RG_POWER * _softplus(-lam_ref[...])
    for n in range(c // RNN_BLOCK):
        sl = slice(n * RNN_BLOCK, (n + 1) * RNN_BLOCK)
        xb = xc[:, sl]
        r = _sigmoid(_dot(xb, wa_ref[n]) + ba_ref[:, sl])
        i = _sigmoid(_dot(xb, wx_ref[n]) + bx_ref[:, sl])
        log_a = neg_sp[:, sl] * r
        a = jnp.exp(log_a)
        a_ref[:, sl] = a
        b_ref[:, sl] = jnp.sqrt(_one_minus_exp2(log_a, a)) * (i * xb)

    row = lax.broadcasted_iota(I32, (SUBLANES, c), 0)

    def slab(t, carry):
        rows = pl.ds(pl.multiple_of(t * SUBLANES, SUBLANES), SUBLANES)
        a = a_ref[rows, :]
        b = b_ref[rows, :]
        for d in (1, 2, 4):
            a_sh = jnp.where(row >= d, pltpu.roll(a, d, 0), 1.0)
            b_sh = jnp.where(row >= d, pltpu.roll(b, d, 0), 0.0)
            b = a * b_sh + b
            a = a * a_sh
        h = a * carry + b
        b_ref[rows, :] = h
        return h[SUBLANES - 1:SUBLANES, :]

    carry_ref[...] = lax.fori_loop(0, ts // SUBLANES, slab, carry_ref[...])
    o_ref[...] = (_gelu_tanh(rg_ref[...].astype(F32)) * b_ref[...]).astype(o_ref.dtype)


def _rglru(proj_a, bsz, seq, cw, cb, wa, ba, wx, bx, lam, ts):
    c = cw.shape[1]
    ns = seq // ts
    full = lambda shape: pl.BlockSpec(shape, lambda b, s: (0,) * len(shape))
    return pl.pallas_call(
        _rglru_kernel,
        grid=(bsz, ns),
        in_specs=[pl.BlockSpec((ts, c), lambda b, s: (b * ns + s, 0)),
                  pl.BlockSpec((ts, c), lambda b, s: (b * ns + s, 1)),
                  full(cw.shape), full(cb.shape), full(wa.shape), full(ba.shape),
                  full(wx.shape), full(bx.shape), full(lam.shape)],
        out_specs=pl.BlockSpec((ts, c), lambda b, s: (b * ns + s, 0)),
        out_shape=jax.ShapeDtypeStruct((bsz * seq, c), BF16),
        scratch_shapes=[pltpu.VMEM((ts + SUBLANES, c), F32), pltpu.VMEM((ts, c), F32),
                        pltpu.VMEM((ts, c), F32), pltpu.VMEM((1, c), F32)],
        compiler_params=_params("parallel", "arbitrary"),
        name="rglru",
    )(proj_a, proj_a, cw, cb, wa, ba, wx, bx, lam)


def _lane_bcast(x, lane):
    return jnp.broadcast_to(x[:, lane:lane + 1], x.shape)


def _unit_lower_inverses(neg_ls, eye):
    ts = eye.shape[0]
    ts_mats = [eye + n for n in neg_ls]
    ps = [_dot(n, n) for n in neg_ls]
    levels = int(math.log2(ts)) - 1
    for lvl in range(levels - 1):
        both = [_dot(jnp.concatenate([t, p], axis=0), p) for t, p in zip(ts_mats, ps)]
        ts_mats = [t + b[:ts] for t, b in zip(ts_mats, both)]
        ps = [b[ts:] for b in both]
    return [t + _dot(t, p) for t, p in zip(ts_mats, ps)]


def _gdn_kernel(qkv_ref, z_ref, ab_ref, abt_ref, cw_ref, acol_ref, dcol_ref, arow_ref, drow_ref, nw_ref,
                o_ref, xbuf, s_ref):
    ts = qkv_ref.shape[0]
    n_heads = s_ref.shape[0]
    dn_w = n_heads * HEAD_DIM
    _load_conv_window(qkv_ref, xbuf, ts)

    @pl.when(pl.program_id(1) == 0)
    def _():
        s_ref[...] = jnp.zeros_like(s_ref)

    row = lax.broadcasted_iota(I32, (ts, ts), 0)
    col = lax.broadcasted_iota(I32, (ts, ts), 1)
    lower_incl = (row >= col).astype(F32)
    upper_incl = (row <= col).astype(F32)
    eye = (row == col).astype(F32)

    ab = ab_ref[...]
    g_col = -acol_ref[...] * _softplus(ab + dcol_ref[...])
    beta_col = _sigmoid(ab)
    g_row = -arow_ref[...] * _softplus(abt_ref[...] + drow_ref[...])
    cum_col = _dot_f32(lower_incl, g_col)
    cum_row = _dot_f32(g_row, upper_incl)
    last_col = jnp.broadcast_to(cum_col[ts - 1:ts, :], cum_col.shape)
    exp_cum = jnp.exp(cum_col)
    exp_rem = jnp.exp(last_col - cum_col)
    exp_last = jnp.exp(cum_col[ts - 1:ts, :])

    qkv = _silu(_causal_conv(xbuf, cw_ref, ts))

    heads = range(n_heads)
    head_cols = lambda h, part: slice(part * dn_w + h * HEAD_DIM, part * dn_w + (h + 1) * HEAD_DIM)
    qs, ks, k_betas, decays, rhs = [], [], [], [], []
    for h in heads:
        q = qkv[:, head_cols(h, 0)]
        k = qkv[:, head_cols(h, 1)]
        v = qkv[:, head_cols(h, 2)]
        q = q * (lax.rsqrt(jnp.sum(q * q, axis=-1, keepdims=True) + NORM_EPS) * (HEAD_DIM ** -0.5))
        k = k * lax.rsqrt(jnp.sum(k * k, axis=-1, keepdims=True) + NORM_EPS)
        beta = _lane_bcast(beta_col, n_heads + h)
        k_beta = k * beta
        qs.append(q)
        ks.append(k)
        k_betas.append(k_beta)
        decays.append(jnp.exp(jnp.minimum(_lane_bcast(cum_col, h) - cum_row[h:h + 1, :], 0.0)))
        rhs.append(jnp.concatenate([v * beta, k_beta * _lane_bcast(exp_cum, h)], axis=1))
    kks = [_dot_nt(k_betas[h], ks[h]) for h in heads]
    qks = [_dot_nt(qs[h], ks[h]) for h in heads]
    neg_ls = [jnp.where(row > col, -(kks[h] * decays[h]), 0.0) for h in heads]
    t_mats = _unit_lower_inverses(neg_ls, eye)
    uws = [_dot(t_mats[h], rhs[h]) for h in heads]
    ws_qs = [_dot(jnp.concatenate([uws[h][:, HEAD_DIM:], qs[h] * _lane_bcast(exp_cum, h)], axis=0), s_ref[h])
             for h in heads]
    v_news = [uws[h][:, :HEAD_DIM] - ws_qs[h][:ts] for h in heads]
    intras = [jnp.where(row >= col, qks[h] * decays[h], 0.0) for h in heads]
    outs = [ws_qs[h][ts:] + _dot(intras[h], v_news[h]) for h in heads]
    kvs = [_dot((ks[h] * _lane_bcast(exp_rem, h)).T, v_news[h]) for h in heads]
    for h in heads:
        s_ref[h] = s_ref[h] * _lane_bcast(jnp.broadcast_to(exp_last, (HEAD_DIM, LANES)), h) + kvs[h]
        o = _rmsnorm(outs[h], nw_ref[...]) * _silu(z_ref[:, head_cols(h, 0)].astype(F32))
        o_ref[:, head_cols(h, 0)] = o.astype(o_ref.dtype)


def _gdn(proj_b, ab, abt, bsz, seq, cw, acol, dcol, arow, drow, nw, n_heads):
    ts = GDN_CHUNK
    ns = seq // ts
    dn_w = n_heads * HEAD_DIM
    full = lambda shape: pl.BlockSpec(shape, lambda b, s: (0,) * len(shape))
    return pl.pallas_call(
        _gdn_kernel,
        grid=(bsz, ns),
        in_specs=[pl.BlockSpec((ts, 3 * dn_w), lambda b, s: (b * ns + s, 0)),
                  pl.BlockSpec((ts, dn_w), lambda b, s: (b * ns + s, 3)),
                  pl.BlockSpec((ts, LANES), lambda b, s: (b * ns + s, 0)),
                  pl.BlockSpec((2 * n_heads, ts), lambda b, s: (0, b * ns + s)),
                  full(cw.shape), full(acol.shape), full(dcol.shape), full(arow.shape), full(drow.shape),
                  full(nw.shape)],
        out_specs=pl.BlockSpec((ts, dn_w), lambda b, s: (b * ns + s, 0)),
        out_shape=jax.ShapeDtypeStruct((bsz * seq, dn_w), BF16),
        scratch_shapes=[pltpu.VMEM((ts + SUBLANES, 3 * dn_w), F32),
                        pltpu.VMEM((n_heads, HEAD_DIM, HEAD_DIM), F32)],
        compiler_params=_params("parallel", "arbitrary"),
        name="gdn",
    )(proj_b, proj_b, ab, abt, cw, acol, dcol, arow, drow, nw)


def _merge_kernel(x_ref, ya_ref, yb_ref, ga_ref, gb_ref, wa_ref, wb_ref, wo_ref, o_ref):
    y_a = jnp.dot(ya_ref[...], wa_ref[...], preferred_element_type=F32)
    y_b = jnp.dot(yb_ref[...], wb_ref[...], preferred_element_type=F32)
    m = _sigmoid(ga_ref[...].astype(F32)) * y_a + _sigmoid(gb_ref[...].astype(F32)) * y_b
    o_ref[...] = x_ref[...] + _dot(m, wo_ref[...])


def _merge(x, gated_a, gated_b, proj_b, w_a, w_b, w_o, tm):
    n, d = x.shape
    full = lambda shape: pl.BlockSpec(shape, lambda i: (0,) * len(shape))
    return pl.pallas_call(
        _merge_kernel,
        grid=(n // tm,),
        in_specs=[pl.BlockSpec((tm, d), lambda i: (i, 0)),
                  pl.BlockSpec((tm, gated_a.shape[1]), lambda i: (i, 0)),
                  pl.BlockSpec((tm, gated_b.shape[1]), lambda i: (i, 0)),
                  pl.BlockSpec((tm, d), lambda i: (i, 4)),
                  pl.BlockSpec((tm, d), lambda i: (i, 5)),
                  full(w_a.shape), full(w_b.shape), full(w_o.shape)],
        out_specs=pl.BlockSpec((tm, d), lambda i: (i, 0)),
        out_shape=jax.ShapeDtypeStruct((n, d), F32),
        compiler_params=_params("parallel"),
        name="merge",
    )(x, gated_a, gated_b, proj_b, proj_b, w_a, w_b, w_o)


def _cross_kernel(h_ref, nw_ref, wq_ref, kv_ref, wo_ref, o_ref):
    x = h_ref[...]
    d = x.shape[1]
    hd = d // CA_HEADS
    u = _rmsnorm(x, nw_ref[...])
    q = _dot(u, wq_ref[...])
    outs = []
    for h in range(CA_HEADS):
        k_h = kv_ref[:, h * hd:(h + 1) * hd]
        v_h = kv_ref[:, d + h * hd:d + (h + 1) * hd]
        s = _dot_nt(q[:, h * hd:(h + 1) * hd], k_h) * (hd ** -0.5)
        s = s - jnp.max(s, axis=-1, keepdims=True)
        e = jnp.exp(s)
        p = e / jnp.sum(e, axis=-1, keepdims=True)
        outs.append(_dot(p, v_h))
    o = jnp.concatenate(outs, axis=1)
    o_ref[...] = x + _dot(o, wo_ref[...])


def _cross(h, nw, w_q, kv, w_o, bsz, seq, n_mem, ts):
    n, d = h.shape
    ns = seq // ts
    full = lambda shape: pl.BlockSpec(shape, lambda b, s: (0,) * len(shape))
    return pl.pallas_call(
        _cross_kernel,
        grid=(bsz, ns),
        in_specs=[pl.BlockSpec((ts, d), lambda b, s: (b * ns + s, 0)),
                  full(nw.shape), full(w_q.shape),
                  pl.BlockSpec((n_mem, 2 * d), lambda b, s: (b, 0)),
                  full(w_o.shape)],
        out_specs=pl.BlockSpec((ts, d), lambda b, s: (b * ns + s, 0)),
        out_shape=jax.ShapeDtypeStruct((n, d), F32),
        compiler_params=_params("parallel", "parallel"),
        name="cross_attn",
    )(h, nw, w_q, kv, w_o)


def _router_kernel(h_ref, nw_ref, wr_ref, br_ref, u_ref, idx_ref, wt_ref, cnt_ref, base_ref):
    t = h_ref.shape[0]

    @pl.when(pl.program_id(0) == 0)
    def _():
        base_ref[...] = jnp.zeros_like(base_ref)

    u = _rmsnorm(h_ref[...], nw_ref[...])
    u_ref[...] = u
    logits = _dot_f32(u, wr_ref[...]) + br_ref[...]
    lane = lax.broadcasted_iota(I32, (t, LANES), 1)
    lanef = lane.astype(F32)
    big = float(LANES)
    neg = -jnp.inf

    lg = jnp.where((lane >= N_EXPERTS) & (lane < N_EXPERTS + N_GROUPS), logits, neg)
    gmax = jnp.max(lg, axis=-1, keepdims=True)
    grp = jnp.min(jnp.where(lg == gmax, lanef - float(N_EXPERTS), big), axis=-1, keepdims=True)
    p_grp = 1.0 / jnp.sum(jnp.exp(lg - gmax), axis=-1, keepdims=True)

    in_grp = (lane < N_EXPERTS) & ((lane // GROUP_SIZE).astype(F32) == grp)
    le = jnp.where(in_grp, logits, neg)
    m1 = jnp.max(le, axis=-1, keepdims=True)
    i1 = jnp.min(jnp.where(le == m1, lanef, big), axis=-1, keepdims=True)
    le2 = jnp.where(lanef == i1, neg, le)
    m2 = jnp.max(le2, axis=-1, keepdims=True)
    i2 = jnp.min(jnp.where(le2 == m2, lanef, big), axis=-1, keepdims=True)
    ratio = jnp.exp(m2 - m1)
    p1 = 1.0 / (1.0 + ratio)
    p2 = ratio * p1

    oh1 = lanef == i1
    oh2 = lanef == i2
    onehot = jnp.where(oh1 | oh2, 1.0, 0.0)
    r_i = lax.broadcasted_iota(I32, (t, t), 0)
    c_i = lax.broadcasted_iota(I32, (t, t), 1)
    before = jnp.where(r_i > c_i, 1.0, 0.0)
    rank = _dot(before, onehot) + base_ref[...]
    r1 = jnp.sum(jnp.where(oh1, rank, 0.0), axis=-1, keepdims=True)
    r2 = jnp.sum(jnp.where(oh2, rank, 0.0), axis=-1, keepdims=True)
    base_ref[...] = base_ref[...] + jnp.sum(onehot, axis=0, keepdims=True)

    idx = jnp.where(lane == 0, i1, jnp.where(lane == 1, i2, jnp.where(lane == 2, r1, jnp.where(lane == 3, r2, 0.0))))
    idx_ref[...] = idx.astype(I32)
    wt_ref[...] = jnp.where(lane == 0, p_grp * p1, jnp.where(lane == 1, p_grp * p2, 0.0))
    cnt_ref[...] = jnp.broadcast_to(base_ref[...], cnt_ref.shape)


def _router(h, nw, wr, br, t):
    n, d = h.shape
    full = lambda shape: pl.BlockSpec(shape, lambda i: (0,) * len(shape))
    return pl.pallas_call(
        _router_kernel,
        grid=(n // t,),
        in_specs=[pl.BlockSpec((t, d), lambda i: (i, 0)), full(nw.shape), full(wr.shape), full(br.shape)],
        out_specs=[pl.BlockSpec((t, d), lambda i: (i, 0)),
                   pl.BlockSpec((t, LANES), lambda i: (i, 0)),
                   pl.BlockSpec((t, LANES), lambda i: (i, 0)),
                   pl.BlockSpec((SUBLANES, LANES), lambda i: (0, 0))],
        out_shape=[jax.ShapeDtypeStruct((n, d), F32),
                   jax.ShapeDtypeStruct((n, LANES), I32),
                   jax.ShapeDtypeStruct((n, LANES), F32),
                   jax.ShapeDtypeStruct((SUBLANES, LANES), F32)],
        scratch_shapes=[pltpu.VMEM((1, LANES), F32)],
        compiler_params=_params("arbitrary"),
        name="router",
    )(h, nw, wr, br)


def _row_copy(src, src_row, dst, dst_row, sem):
    return pltpu.make_async_copy(src.at[pl.ds(src_row, 1)], dst.at[pl.ds(dst_row, 1)], sem)


def _dispatch_kernel(dest_ref, u_ref, xs_in_ref, xs_ref, sem):
    del xs_in_ref
    t = u_ref.shape[0]

    def start(r, carry):
        _row_copy(u_ref, r, xs_ref, dest_ref[0, 0, 2 * r], sem).start()
        _row_copy(u_ref, r, xs_ref, dest_ref[0, 0, 2 * r + 1], sem).start()
        return carry

    def wait(r, carry):
        _row_copy(u_ref, r, xs_ref, dest_ref[0, 0, 2 * r], sem).wait()
        _row_copy(u_ref, r, xs_ref, dest_ref[0, 0, 2 * r + 1], sem).wait()
        return carry

    lax.fori_loop(0, t, start, 0, unroll=DMA_UNROLL)
    lax.fori_loop(0, t, wait, 0, unroll=DMA_UNROLL)


def _dispatch(dest3, u, xs_zero, t):
    n, d = u.shape
    return pl.pallas_call(
        _dispatch_kernel,
        grid=(n // t,),
        in_specs=[pl.BlockSpec((1, 1, 2 * t), lambda i: (i, 0, 0), memory_space=pltpu.SMEM),
                  pl.BlockSpec((t, d), lambda i: (i, 0)),
                  pl.BlockSpec(memory_space=pl.ANY)],
        out_specs=pl.BlockSpec(memory_space=pl.ANY),
        out_shape=jax.ShapeDtypeStruct(xs_zero.shape, xs_zero.dtype),
        scratch_shapes=[pltpu.SemaphoreType.DMA(())],
        input_output_aliases={2: 0},
        compiler_params=_params("arbitrary"),
        name="moe_dispatch",
    )(dest3, u, xs_zero)


def _expert_kernel(be_ref, nu_ref, xs_ref, wg_ref, wu_ref, wd_ref, y_ref, wg_bf, wu_bf, wd_bf):
    j = pl.program_id(0)
    used = j < nu_ref[0]
    new_expert = (j == 0) | (be_ref[j] != be_ref[jnp.maximum(j - 1, 0)])

    @pl.when(used & new_expert)
    def _():
        wg_bf[...] = wg_ref[0].astype(BF16)
        wu_bf[...] = wu_ref[0].astype(BF16)
        wd_bf[...] = wd_ref[0].astype(BF16)

    @pl.when(used)
    def _():
        x = xs_ref[...].astype(BF16)
        hid = _silu(jnp.dot(x, wg_bf[...], preferred_element_type=F32)) * jnp.dot(
            x, wu_bf[...], preferred_element_type=F32)
        y_ref[...] = _dot(hid, wd_bf[...])

    @pl.when(jnp.logical_not(used))
    def _():
        y_ref[...] = jnp.zeros_like(y_ref)


def _experts(block_e, n_used, xs, w_gate, w_up, w_down):
    rows, d = xs.shape
    de = w_gate.shape[2]
    grid_spec = pltpu.PrefetchScalarGridSpec(
        num_scalar_prefetch=2,
        grid=(rows // MOE_BLOCK,),
        in_specs=[pl.BlockSpec((MOE_BLOCK, d), lambda j, be, nu: (j, 0)),
                  pl.BlockSpec((1, d, de), lambda j, be, nu: (be[j], 0, 0)),
                  pl.BlockSpec((1, d, de), lambda j, be, nu: (be[j], 0, 0)),
                  pl.BlockSpec((1, de, d), lambda j, be, nu: (be[j], 0, 0))],
        out_specs=pl.BlockSpec((MOE_BLOCK, d), lambda j, be, nu: (j, 0)),
        scratch_shapes=[pltpu.VMEM((d, de), BF16), pltpu.VMEM((d, de), BF16), pltpu.VMEM((de, d), BF16)],
    )
    return pl.pallas_call(
        _expert_kernel,
        grid_spec=grid_spec,
        out_shape=jax.ShapeDtypeStruct((rows, d), F32),
        compiler_params=_params("arbitrary"),
        name="moe_experts",
    )(block_e, n_used, xs, w_gate, w_up, w_down)


def _combine_kernel(dest_ref, h_ref, wt_ref, nw_ref, y_ref, o_ref, buf, sem, *, final_norm):
    t = h_ref.shape[0]

    def start(r, carry):
        _row_copy(y_ref, dest_ref[0, 0, 2 * r], buf.at[0], r, sem).start()
        _row_copy(y_ref, dest_ref[0, 0, 2 * r + 1], buf.at[1], r, sem).start()
        return carry

    def wait(r, carry):
        _row_copy(y_ref, dest_ref[0, 0, 2 * r], buf.at[0], r, sem).wait()
        _row_copy(y_ref, dest_ref[0, 0, 2 * r + 1], buf.at[1], r, sem).wait()
        return carry

    lax.fori_loop(0, t, start, 0, unroll=DMA_UNROLL)
    lax.fori_loop(0, t, wait, 0, unroll=DMA_UNROLL)
    wt = wt_ref[...]
    moe = buf[0] * wt[:, 0:1] + buf[1] * wt[:, 1:2]
    out = h_ref[...] + moe
    o_ref[...] = _rmsnorm(out, nw_ref[...]) if final_norm else out


def _combine(dest3, h, wt, nw, y, t, final_norm):
    n, d = h.shape
    return pl.pallas_call(
        functools.partial(_combine_kernel, final_norm=final_norm),
        grid=(n // t,),
        in_specs=[pl.BlockSpec((1, 1, 2 * t), lambda i: (i, 0, 0), memory_space=pltpu.SMEM),
                  pl.BlockSpec((t, d), lambda i: (i, 0)),
                  pl.BlockSpec((t, LANES), lambda i: (i, 0)),
                  pl.BlockSpec((1, d), lambda i: (0, 0)),
                  pl.BlockSpec(memory_space=pl.ANY)],
        out_specs=pl.BlockSpec((t, d), lambda i: (i, 0)),
        out_shape=jax.ShapeDtypeStruct((n, d), F32),
        scratch_shapes=[pltpu.VMEM((2, t, d), F32), pltpu.SemaphoreType.DMA(())],
        compiler_params=_params("arbitrary"),
        name="moe_combine",
    )(dest3, h, wt, nw, y)


def _tile(n, pref):
    return pref if n % pref == 0 else n


def kernel(x, mem, norm1_w, w_in, rnn_conv_w, rnn_conv_b, rglru_wa, rglru_ba, rglru_wx, rglru_bx, rglru_lambda, w_branch_a, dn_conv_w, dn_a_log, dn_dt_bias, dn_norm_w, w_branch_b, w_out, norm2_w, mem_norm_w, w_cq, w_ckv, w_co, norm3_w, w_router_group, b_router_group, w_router_expert, b_router_expert, w_exp_gate, w_exp_up, w_exp_down, norm_f_w):
    bsz, seq, d = x.shape
    n = bsz * seq
    n_mem = mem.shape[1]
    depth = w_in.shape[0]
    d_rnn = rnn_conv_w.shape[2]
    n_heads = dn_a_log.shape[1]
    dn_w = n_heads * HEAD_DIM
    row = lambda v: v.reshape(1, -1).astype(F32)

    h = x.reshape(n, d)
    mem2 = mem.reshape(bsz * n_mem, d)
    tm = _tile(n, 1024)
    for l in range(depth):
        o_rg, o_qkv, o_z = d_rnn, 2 * d_rnn, 2 * d_rnn + 3 * dn_w
        o_a = o_z + dn_w
        o_ga = o_a + 2 * n_heads
        wi = w_in[l]
        w_pa = wi[:, :o_qkv].astype(BF16)
        w_pb = jnp.concatenate([wi[:, o_qkv:o_a], wi[:, o_ga:]], axis=1).astype(BF16)
        w_ab = jnp.pad(wi[:, o_a:o_ga], ((0, 0), (0, LANES - 2 * n_heads))).astype(BF16)
        n1 = row(norm1_w[l])
        proj_a = _norm_mm(h, n1, w_pa, BF16, tm, _tile(w_pa.shape[1], 512))
        proj_b = _norm_mm(h, n1, w_pb, BF16, tm, _tile(w_pb.shape[1], 512))
        ab = _norm_mm(h, n1, w_ab, F32, tm, LANES)

        gated_a = _rglru(proj_a, bsz, seq, rnn_conv_w[l], row(rnn_conv_b[l]), rglru_wa[l].astype(BF16),
                         row(rglru_ba[l]), rglru_wx[l].astype(BF16), row(rglru_bx[l]), row(rglru_lambda[l]),
                         _tile(seq, 512))

        a_dec = jnp.exp(dn_a_log[l].astype(F32))
        pad_h = lambda v: jnp.pad(v, (0, LANES - n_heads))
        acol, dcol = row(pad_h(a_dec)), row(pad_h(dn_dt_bias[l]))
        arow = jnp.broadcast_to(jnp.pad(a_dec, (0, n_heads))[:, None], (2 * n_heads, GDN_CHUNK))
        drow = jnp.broadcast_to(jnp.pad(dn_dt_bias[l], (0, n_heads))[:, None], (2 * n_heads, GDN_CHUNK))
        abt = ab[:, :2 * n_heads].T
        gated_b = _gdn(proj_b, ab, abt, bsz, seq, dn_conv_w[l], acol, dcol, arow, drow, row(dn_norm_w[l]), n_heads)

        h = _merge(h, gated_a, gated_b, proj_b, w_branch_a[l].astype(BF16), w_branch_b[l].astype(BF16),
                   w_out[l].astype(BF16), _tile(n, 512))

        kv = _norm_mm(mem2, row(mem_norm_w[l]), w_ckv[l].astype(BF16), BF16, _tile(bsz * n_mem, 1024), 512)
        h = _cross(h, row(norm2_w[l]), w_cq[l].astype(BF16), kv, w_co[l].astype(BF16), bsz, seq, n_mem,
                   _tile(seq, 512))

        w_r = jnp.pad(jnp.concatenate([w_router_expert[l], w_router_group[l]], axis=1),
                      ((0, 0), (0, LANES - N_EXPERTS - N_GROUPS)))
        b_r = row(jnp.pad(jnp.concatenate([b_router_expert[l], b_router_group[l]]), (0, LANES - N_EXPERTS - N_GROUPS)))
        u3, idx, wt, cnt = _router(h, row(norm3_w[l]), w_r, b_r, _tile(n, 512))

        counts = cnt[0, :N_EXPERTS].astype(I32)
        padded = (counts + MOE_BLOCK - 1) // MOE_BLOCK * MOE_BLOCK
        pend = jnp.cumsum(padded)
        pstart = pend - padded
        dest = pstart[idx[:, 0:2]] + idx[:, 2:4]
        n_blocks = (2 * n + N_EXPERTS * (MOE_BLOCK - 1)) // MOE_BLOCK
        block_row = jnp.arange(n_blocks, dtype=I32) * MOE_BLOCK
        block_e = jnp.minimum(jnp.sum((pend[None, :] <= block_row[:, None]).astype(I32), axis=1), N_EXPERTS - 1)
        n_used = (pend[-1:] // MOE_BLOCK).astype(I32)
        t_moe = _tile(n, 256)
        dest3 = dest.reshape(n // t_moe, 1, 2 * t_moe)
        xs = _dispatch(dest3, u3, jnp.zeros((n_blocks * MOE_BLOCK, d), F32), t_moe)
        yb = _experts(block_e, n_used, xs, w_exp_gate[l], w_exp_up[l], w_exp_down[l])
        h = _combine(dest3, h, wt, row(norm_f_w), yb, t_moe, final_norm=(l == depth - 1))
    return h.reshape(bsz, seq, d)
```

```python
import functools
import math

import jax
import jax.numpy as jnp
from jax import lax
from jax.experimental import pallas as pl
from jax.experimental.pallas import tpu as pltpu

F32 = jnp.float32
BF16 = jnp.bfloat16
I32 = jnp.int32
HIGHEST = lax.Precision.HIGHEST

NORM_EPS = 1e-6
CONV_TAPS = 4
RNN_BLOCK = 128
RG_POWER = 8.0
HEAD_DIM = 128
GDN_CHUNK = 128
CA_HEADS = 4
N_GROUPS = 8
GROUP_SIZE = 8
N_EXPERTS = N_GROUPS * GROUP_SIZE
MOE_BLOCK = 256
LANES = 128
SUBLANES = 8
DMA_UNROLL = 8
VMEM_LIMIT = 48 * 1024 * 1024


def _params(*semantics):
    return pltpu.CompilerParams(dimension_semantics=semantics, vmem_limit_bytes=VMEM_LIMIT)


def _dot(a, b):
    return jnp.dot(a.astype(BF16), b.astype(BF16), preferred_element_type=F32)


def _dot_nt(a, b):
    return lax.dot_general(a.astype(BF16), b.astype(BF16), (((1,), (1,)), ((), ())),
                           preferred_element_type=F32)


def _dot_tn(a, b):
    return lax.dot_general(a.astype(BF16), b.astype(BF16), (((0,), (0,)), ((), ())),
                           preferred_element_type=F32)


def _dot_f32(a, b):
    return jnp.dot(a, b, precision=HIGHEST, preferred_element_type=F32)


def _rmsnorm(x, w):
    return x * lax.rsqrt(jnp.mean(x * x, axis=-1, keepdims=True) + NORM_EPS) * w


def _sigmoid(x):
    return 0.5 * jnp.tanh(0.5 * x) + 0.5


def _silu(x):
    return x * _sigmoid(x)


def _softplus(x):
    return jnp.maximum(x, 0.0) + jnp.log(1.0 + jnp.exp(-jnp.abs(x)))


def _one_minus_exp2(y, exp_y):
    u = exp_y * exp_y
    d = 1.0 - u
    return jnp.where(y > -0.125, jnp.where(d == 0.0, -2.0 * y, d * (2.0 * y) / jnp.log(u)), d)


def _gelu_tanh(x):
    return 0.5 * x * (1.0 + jnp.tanh(math.sqrt(2.0 / math.pi) * (x + 0.044715 * (x * x * x))))


def _norm_mm_kernel(x_ref, nw_ref, w_ref, o_ref, u_ref):
    @pl.when(pl.program_id(1) == 0)
    def _():
        u_ref[...] = _rmsnorm(x_ref[...], nw_ref[...]).astype(BF16)

    o_ref[...] = jnp.dot(u_ref[...], w_ref[...], preferred_element_type=F32).astype(o_ref.dtype)


def _norm_mm(x, nw, w, out_dtype, tm, tn):
    n, d = x.shape
    c = w.shape[1]
    return pl.pallas_call(
        _norm_mm_kernel,
        grid=(n // tm, c // tn),
        in_specs=[pl.BlockSpec((tm, d), lambda i, j: (i, 0)),
                  pl.BlockSpec((1, d), lambda i, j: (0, 0)),
                  pl.BlockSpec((d, tn), lambda i, j: (0, j))],
        out_specs=pl.BlockSpec((tm, tn), lambda i, j: (i, j)),
        out_shape=jax.ShapeDtypeStruct((n, c), out_dtype),
        scratch_shapes=[pltpu.VMEM((tm, d), BF16)],
        compiler_params=_params("parallel", "arbitrary"),
        name="norm_mm",
    )(x, nw, w)


def _load_conv_window(x_ref, xbuf, ts):
    @pl.when(pl.program_id(1) == 0)
    def _():
        xbuf[0:SUBLANES, :] = jnp.zeros((SUBLANES, xbuf.shape[1]), F32)

    @pl.when(pl.program_id(1) != 0)
    def _():
        xbuf[0:SUBLANES, :] = xbuf[ts:ts + SUBLANES, :]

    xbuf[SUBLANES:SUBLANES + ts, :] = x_ref[...].astype(F32)


def _causal_conv(xbuf, cw_ref, ts):
    base = SUBLANES - (CONV_TAPS - 1)
    acc = cw_ref[0:1, :] * xbuf[base:base + ts, :]
    for k in range(1, CONV_TAPS):
        acc = acc + cw_ref[k:k + 1, :] * xbuf[base + k:base + k + ts, :]
    return acc


def _rglru_kernel(rx_ref, rg_ref, cw_ref, cb_ref, wa_ref, ba_ref, wx_ref, bx_ref, lam_ref, o_ref,
                  xbuf, a_ref, b_ref, carry_ref):
    ts, c = rx_ref.shape
    _load_conv_window(rx_ref, xbuf, ts)

    @pl.when(pl.program_id(1) == 0)
    def _():
        carry_ref[...] = jnp.zeros_like(carry_ref)

    xc = _causal_conv(xbuf, cw_ref, ts) + cb_ref[...]
    neg_sp = -RG_POWER * _softplus(-lam_ref[...])
    for n in range(c // RNN_BLOCK):
        sl = slice(n * RNN_BLOCK, (n + 1) * RNN_BLOCK)
        xb = xc[:, sl]
        r = _sigmoid(_dot(xb, wa_ref[n]) + ba_ref[:, sl])
        i = _sigmoid(_dot(xb, wx_ref[n]) + bx_ref[:, sl])
        log_a = neg_sp[:, sl] * r
        a = jnp.exp(log_a)
        a_ref[:, sl] = a
        b_ref[:, sl] = jnp.sqrt(_one_minus_exp2(log_a, a)) * (i * xb)

    row = lax.broadcasted_iota(I32, (SUBLANES, c), 0)

    def slab(t, carry):
        rows = pl.ds(pl.multiple_of(t * SUBLANES, SUBLANES), SUBLANES)
        a = a_ref[rows, :]
        b = b_ref[rows, :]
        for d in (1, 2, 4):
            a_sh = jnp.where(row >= d, pltpu.roll(a, d, 0), 1.0)
            b_sh = jnp.where(row >= d, pltpu.roll(b, d, 0), 0.0)
            b = a * b_sh + b
            a = a * a_sh
        h = a * carry + b
        b_ref[rows, :] = h
        return h[SUBLANES - 1:SUBLANES, :]

    carry_ref[...] = lax.fori_loop(0, ts // SUBLANES, slab, carry_ref[...])
    o_ref[...] = (_gelu_tanh(rg_ref[...].astype(F32)) * b_ref[...]).astype(o_ref.dtype)


def _rglru(proj_a, bsz, seq, cw, cb, wa, ba, wx, bx, lam, ts):
    c = cw.shape[1]
    ns = seq // ts
    full = lambda shape: pl.BlockSpec(shape, lambda b, s: (0,) * len(shape))
    return pl.pallas_call(
        _rglru_kernel,
        grid=(bsz, ns),
        in_specs=[pl.BlockSpec((ts, c), lambda b, s: (b * ns + s, 0)),
                  pl.BlockSpec((ts, c), lambda b, s: (b * ns + s, 1)),
                  full(cw.shape), full(cb.shape), full(wa.shape), full(ba.shape),
                  full(wx.shape), full(bx.shape), full(lam.shape)],
        out_specs=pl.BlockSpec((ts, c), lambda b, s: (b * ns + s, 0)),
        out_shape=jax.ShapeDtypeStruct((bsz * seq, c), BF16),
        scratch_shapes=[pltpu.VMEM((ts + SUBLANES, c), F32), pltpu.VMEM((ts, c), F32),
                        pltpu.VMEM((ts, c), F32), pltpu.VMEM((1, c), F32)],
        compiler_params=_params("parallel", "arbitrary"),
        name="rglru",
    )(proj_a, proj_a, cw, cb, wa, ba, wx, bx, lam)


def _lane_bcast(x, lane):
    return jnp.broadcast_to(x[:, lane:lane + 1], x.shape)


def _unit_lower_inverses(neg_ls, eye):
    ts = eye.shape[0]
    ts_mats = [eye + n for n in neg_ls]
    ps = [_dot(n, n) for n in neg_ls]
    levels = int(math.log2(ts)) - 1
    for lvl in range(levels - 1):
        both = [_dot(jnp.concatenate([t, p], axis=0), p) for t, p in zip(ts_mats, ps)]
        ts_mats = [t + b[:ts] for t, b in zip(ts_mats, both)]
        ps = [b[ts:] for b in both]
    return [t + _dot(t, p) for t, p in zip(ts_mats, ps)]


def _gdn_kernel(qkv_ref, z_ref, ab_ref, abt_ref, cw_ref, acol_ref, dcol_ref, arow_ref, drow_ref, nw_ref,
                o_ref, xbuf, s_ref):
    ts = qkv_ref.shape[0]
    n_heads = s_ref.shape[0]
    dn_w = n_heads * HEAD_DIM
    _load_conv_window(qkv_ref, xbuf, ts)

    @pl.when(pl.program_id(1) == 0)
    def _():
        s_ref[...] = jnp.zeros_like(s_ref)

    row = lax.broadcasted_iota(I32, (ts, ts), 0)
    col = lax.broadcasted_iota(I32, (ts, ts), 1)
    lower_incl = (row >= col).astype(F32)
    upper_incl = (row <= col).astype(F32)
    eye = (row == col).astype(F32)

    ab = ab_ref[...]
    g_col = -acol_ref[...] * _softplus(ab + dcol_ref[...])
    beta_col = _sigmoid(ab)
    g_row = -arow_ref[...] * _softplus(abt_ref[...] + drow_ref[...])
    cum_col = _dot_f32(lower_incl, g_col)
    cum_row = _dot_f32(g_row, upper_incl)
    last_col = jnp.broadcast_to(cum_col[ts - 1:ts, :], cum_col.shape)
    exp_cum = jnp.exp(cum_col)
    exp_rem = jnp.exp(last_col - cum_col)
    exp_last = jnp.exp(cum_col[ts - 1:ts, :])

    qkv = _silu(_causal_conv(xbuf, cw_ref, ts))

    heads = range(n_heads)
    head_cols = lambda h, part: slice(part * dn_w + h * HEAD_DIM, part * dn_w + (h + 1) * HEAD_DIM)
    qs, ks, k_betas, decays, rhs = [], [], [], [], []
    for h in heads:
        q = qkv[:, head_cols(h, 0)]
        k = qkv[:, head_cols(h, 1)]
        v = qkv[:, head_cols(h, 2)]
        q = q * (lax.rsqrt(jnp.sum(q * q, axis=-1, keepdims=True) + NORM_EPS) * (HEAD_DIM ** -0.5))
        k = k * lax.rsqrt(jnp.sum(k * k, axis=-1, keepdims=True) + NORM_EPS)
        beta = _lane_bcast(beta_col, n_heads + h)
        k_beta = k * beta
        qs.append(q)
        ks.append(k)
        k_betas.append(k_beta)
        decays.append(jnp.exp(jnp.minimum(_lane_bcast(cum_col, h) - cum_row[h:h + 1, :], 0.0)))
        rhs.append(jnp.concatenate([v * beta, k_beta * _lane_bcast(exp_cum, h)], axis=1))
    kks = [_dot_nt(k_betas[h], ks[h]) for h in heads]
    qks = [_dot_nt(qs[h], ks[h]) for h in heads]
    neg_ls = [jnp.where(row > col, -(kks[h] * decays[h]), 0.0) for h in heads]
    t_mats = _unit_lower_inverses(neg_ls, eye)
    uws = [_dot(t_mats[h], rhs[h]) for h in heads]
    ws_qs = [_dot(jnp.concatenate([uws[h][:, HEAD_DIM:], qs[h] * _lane_bcast(exp_cum, h)], axis=0), s_ref[h])
             for h in heads]
    v_news = [uws[h][:, :HEAD_DIM] - ws_qs[h][:ts] for h in heads]
    intras = [jnp.where(row >= col, qks[h] * decays[h], 0.0) for h in heads]
    outs = [ws_qs[h][ts:] + _dot(intras[h], v_news[h]) for h in heads]
    kvs = [_dot((ks[h] * _lane_bcast(exp_rem, h)).T, v_news[h]) for h in heads]
    for h in heads:
        s_ref[h] = s_ref[h] * _lane_bcast(jnp.broadcast_to(exp_last, (HEAD_DIM, LANES)), h) + kvs[h]
        o = _rmsnorm(outs[h], nw_ref[...]) * _silu(z_ref[:, head_cols(h, 0)].astype(F32))
        o_ref[:, head_cols(h, 0)] = o.astype(o_ref.dtype)


def _gdn(proj_b, ab, abt, bsz, seq, cw, acol, dcol, arow, drow, nw, n_heads):
    ts = GDN_CHUNK
    ns = seq // ts
    dn_w = n_heads * HEAD_DIM
    full = lambda shape: pl.BlockSpec(shape, lambda b, s: (0,) * len(shape))
    return pl.pallas_call(
        _gdn_kernel,
        grid=(bsz, ns),
        in_specs=[pl.BlockSpec((ts, 3 * dn_w), lambda b, s: (b * ns + s, 0)),
                  pl.BlockSpec((ts, dn_w), lambda b, s: (b * ns + s, 3)),
                  pl.BlockSpec((ts, LANES), lambda b, s: (b * ns + s, 0)),
                  pl.BlockSpec((2 * n_heads, ts), lambda b, s: (0, b * ns + s)),
                  full(cw.shape), full(acol.shape), full(dcol.shape), full(arow.shape), full(drow.shape),
                  full(nw.shape)],
        out_specs=pl.BlockSpec((ts, dn_w), lambda b, s: (b * ns + s, 0)),
        out_shape=jax.ShapeDtypeStruct((bsz * seq, dn_w), BF16),
        scratch_shapes=[pltpu.VMEM((ts + SUBLANES, 3 * dn_w), F32),
                        pltpu.VMEM((n_heads, HEAD_DIM, HEAD_DIM), F32)],
        compiler_params=_params("parallel", "arbitrary"),
        name="gdn",
    )(proj_b, proj_b, ab, abt, cw, acol, dcol, arow, drow, nw)


def _merge_kernel(x_ref, ya_ref, yb_ref, ga_ref, gb_ref, wa_ref, wb_ref, wo_ref, o_ref):
    y_a = jnp.dot(ya_ref[...], wa_ref[...], preferred_element_type=F32)
    y_b = jnp.dot(yb_ref[...], wb_ref[...], preferred_element_type=F32)
    m = _sigmoid(ga_ref[...].astype(F32)) * y_a + _sigmoid(gb_ref[...].astype(F32)) * y_b
    o_ref[...] = x_ref[...] + _dot(m, wo_ref[...])


def _merge(x, gated_a, gated_b, proj_b, w_a, w_b, w_o, tm):
    n, d = x.shape
    full = lambda shape: pl.BlockSpec(shape, lambda i: (0,) * len(shape))
    return pl.pallas_call(
        _merge_kernel,
        grid=(n // tm,),
        in_specs=[pl.BlockSpec((tm, d), lambda i: (i, 0)),
                  pl.BlockSpec((tm, gated_a.shape[1]), lambda i: (i, 0)),
                  pl.BlockSpec((tm, gated_b.shape[1]), lambda i: (i, 0)),
                  pl.BlockSpec((tm, d), lambda i: (i, 4)),
                  pl.BlockSpec((tm, d), lambda i: (i, 5)),
                  full(w_a.shape), full(w_b.shape), full(w_o.shape)],
        out_specs=pl.BlockSpec((tm, d), lambda i: (i, 0)),
        out_shape=jax.ShapeDtypeStruct((n, d), F32),
        compiler_params=_params("parallel"),
        name="merge",
    )(x, gated_a, gated_b, proj_b, proj_b, w_a, w_b, w_o)


def _cross_kernel(h_ref, nw_ref, wq_ref, kv_ref, wo_ref, o_ref):
    x = h_ref[...]
    d = x.shape[1]
    hd = d // CA_HEADS
    u = _rmsnorm(x, nw_ref[...])
    q = _dot(u, wq_ref[...])
    outs = []
    for h in range(CA_HEADS):
        k_h = kv_ref[:, h * hd:(h + 1) * hd]
        v_h = kv_ref[:, d + h * hd:d + (h + 1) * hd]
        s = _dot_nt(q[:, h * hd:(h + 1) * hd], k_h) * (hd ** -0.5)
        s = s - jnp.max(s, axis=-1, keepdims=True)
        e = jnp.exp(s)
        p = e / jnp.sum(e, axis=-1, keepdims=True)
        outs.append(_dot(p, v_h))
    o = jnp.concatenate(outs, axis=1)
    o_ref[...] = x + _dot(o, wo_ref[...])


def _cross(h, nw, w_q, kv, w_o, bsz, seq, n_mem, ts):
    n, d = h.shape
    ns = seq // ts
    full = lambda shape: pl.BlockSpec(shape, lambda b, s: (0,) * len(shape))
    return pl.pallas_call(
        _cross_kernel,
        grid=(bsz, ns),
        in_specs=[pl.BlockSpec((ts, d), lambda b, s: (b * ns + s, 0)),
                  full(nw.shape), full(w_q.shape),
                  pl.BlockSpec((n_mem, 2 * d), lambda b, s: (b, 0)),
                  full(w_o.shape)],
        out_specs=pl.BlockSpec((ts, d), lambda b, s: (b * ns + s, 0)),
        out_shape=jax.ShapeDtypeStruct((n, d), F32),
        compiler_params=_params("parallel", "parallel"),
        name="cross_attn",
    )(h, nw, w_q, kv, w_o)


def _router_kernel(h_ref, nw_ref, wr_ref, br_ref, u_ref, idx_ref, wt_ref, cnt_ref, base_ref):
    t = h_ref.shape[0]

    @pl.when(pl.program_id(0) == 0)
    def _():
        base_ref[...] = jnp.zeros_like(base_ref)

    u = _rmsnorm(h_ref[...], nw_ref[...])
    u_ref[...] = u
    logits = _dot_f32(u, wr_ref[...]) + br_ref[...]
    lane = lax.broadcasted_iota(I32, (t, LANES), 1)
    lanef = lane.astype(F32)
    big = float(LANES)
    neg = -jnp.inf

    lg = jnp.where((lane >= N_EXPERTS) & (lane < N_EXPERTS + N_GROUPS), logits, neg)
    gmax = jnp.max(lg, axis=-1, keepdims=True)
    grp = jnp.min(jnp.where(lg == gmax, lanef - float(N_EXPERTS), big), axis=-1, keepdims=True)
    p_grp = 1.0 / jnp.sum(jnp.exp(lg - gmax), axis=-1, keepdims=True)

    in_grp = (lane < N_EXPERTS) & ((lane // GROUP_SIZE).astype(F32) == grp)
    le = jnp.where(in_grp, logits, neg)
    m1 = jnp.max(le, axis=-1, keepdims=True)
    i1 = jnp.min(jnp.where(le == m1, lanef, big), axis=-1, keepdims=True)
    le2 = jnp.where(lanef == i1, neg, le)
    m2 = jnp.max(le2, axis=-1, keepdims=True)
    i2 = jnp.min(jnp.where(le2 == m2, lanef, big), axis=-1, keepdims=True)
    ratio = jnp.exp(m2 - m1)
    p1 = 1.0 / (1.0 + ratio)
    p2 = ratio * p1

    oh1 = lanef == i1
    oh2 = lanef == i2
    onehot = jnp.where(oh1 | oh2, 1.0, 0.0)
    r_i = lax.broadcasted_iota(I32, (t, t), 0)
    c_i = lax.broadcasted_iota(I32, (t, t), 1)
    before = jnp.where(r_i > c_i, 1.0, 0.0)
    rank = _dot(before, onehot) + base_ref[...]
    r1 = jnp.sum(jnp.where(oh1, rank, 0.0), axis=-1, keepdims=True)
    r2 = jnp.sum(jnp.where(oh2, rank, 0.0), axis=-1, keepdims=True)
    base_ref[...] = base_ref[...] + jnp.sum(onehot, axis=0, keepdims=True)

    idx = jnp.where(lane == 0, i1, jnp.where(lane == 1, i2, jnp.where(lane == 2, r1, jnp.where(lane == 3, r2, 0.0))))
    idx_ref[...] = idx.astype(I32)
    wt_ref[...] = jnp.where(lane == 0, p_grp * p1, jnp.where(lane == 1, p_grp * p2, 0.0))
    cnt_ref[...] = jnp.broadcast_to(base_ref[...], cnt_ref.shape)


def _router(h, nw, wr, br, t):
    n, d = h.shape
    full = lambda shape: pl.BlockSpec(shape, lambda i: (0,) * len(shape))
    return pl.pallas_call(
        _router_kernel,
        grid=(n // t,),
        in_specs=[pl.BlockSpec((t, d), lambda i: (i, 0)), full(nw.shape), full(wr.shape), full(br.shape)],
        out_specs=[pl.BlockSpec((t, d), lambda i: (i, 0)),
                   pl.BlockSpec((t, LANES), lambda i: (i, 0)),
                   pl.BlockSpec((t, LANES), lambda i: (i, 0)),
                   pl.BlockSpec((SUBLANES, LANES), lambda i: (0, 0))],
        out_shape=[jax.ShapeDtypeStruct((n, d), F32),
                   jax.ShapeDtypeStruct((n, LANES), I32),
                   jax.ShapeDtypeStruct((n, LANES), F32),
                   jax.ShapeDtypeStruct((SUBLANES, LANES), F32)],
        scratch_shapes=[pltpu.VMEM((1, LANES), F32)],
        compiler_params=_params("arbitrary"),
        name="router",
    )(h, nw, wr, br)


def _row_copy(src, src_row, dst, dst_row, sem):
    return pltpu.make_async_copy(src.at[pl.ds(src_row, 1)], dst.at[pl.ds(dst_row, 1)], sem)


def _wait_rows(src, dst, sem):
    def body(r, carry):
        _row_copy(src, 0, dst, 0, sem).wait()
        return carry

    lax.fori_loop(0, MOE_BLOCK, body, 0, unroll=DMA_UNROLL)


def _expert_kernel(be_ref, nu_ref, prev_ref, next_ref, w_ref, wg_ref, wu_ref, wd_ref, u_hbm, y_hbm,
                   xbuf0, xbuf1, ybuf0, ybuf1, wg_bf, wu_bf, wd_bf, gsems, ssems, *, n_tokens):
    j = pl.program_id(0)
    n_used = nu_ref[0]
    xbufs, ybufs = (xbuf0, xbuf1), (ybuf0, ybuf1)
    trash = 2 * n_tokens

    def start_gather(code_ref, parity):
        for r in range(MOE_BLOCK):
            _row_copy(u_hbm, jnp.maximum(code_ref[0, 0, r] >> 1, 0), xbufs[parity], r, gsems.at[parity]).start()

    def start_scatter(code_ref, parity, to_trash):
        for r in range(MOE_BLOCK):
            code = code_ref[0, 0, r]
            row = (code & 1) * n_tokens + (code >> 1)
            row = jnp.where(to_trash | (code < 0), trash + parity * MOE_BLOCK + r, row)
            _row_copy(ybufs[parity], r, y_hbm, row, ssems.at[parity]).start()

    @pl.when(j == 0)
    def _():
        for q in range(2):
            ybufs[q][...] = jnp.zeros_like(ybufs[q])
        zero_fill = pltpu.make_async_copy(ybuf0, y_hbm.at[pl.ds(trash, MOE_BLOCK)], ssems.at[0])
        zero_fill.start()
        zero_fill.wait()
        start_gather(prev_ref, 0)

    jc = jnp.minimum(j, be_ref.shape[0] - 1)
    @pl.when((j < n_used) & ((j == 0) | (be_ref[jc] != be_ref[jnp.maximum(jc - 1, 0)])))
    def _():
        wg_bf[...] = wg_ref[0].astype(BF16)
        wu_bf[...] = wu_ref[0].astype(BF16)
        wd_bf[...] = wd_ref[0].astype(BF16)

    for p in range(2):
        @pl.when((j <= n_used) & (j % 2 == p))
        def _(p=p):
            _wait_rows(u_hbm, xbufs[p], gsems.at[p])

            @pl.when(j >= 1)
            def _():
                _wait_rows(ybufs[p], y_hbm, ssems.at[p])

        @pl.when((j < n_used) & (j % 2 == p))
        def _(p=p):
            start_gather(next_ref, 1 - p)
            start_scatter(prev_ref, 1 - p, j == 0)
            x = xbufs[p][...].astype(BF16)
            hid = _silu(jnp.dot(x, wg_bf[...], preferred_element_type=F32)) * jnp.dot(
                x, wu_bf[...], preferred_element_type=F32)
            ybufs[p][...] = _dot(hid, wd_bf[...]) * w_ref[...]

        @pl.when((j == n_used) & (j % 2 == p))
        def _(p=p):
            start_scatter(prev_ref, 1 - p, j == 0)
            _wait_rows(ybufs[1 - p], y_hbm, ssems.at[1 - p])


def _experts(block_e, n_used, code3, row_w, u, w_gate, w_up, w_down):
    n, d = u.shape
    nb = code3.shape[0]
    de = w_gate.shape[2]
    last = nb - 1
    grid_spec = pltpu.PrefetchScalarGridSpec(
        num_scalar_prefetch=2,
        grid=(nb + 1,),
        in_specs=[pl.BlockSpec((1, 1, MOE_BLOCK), lambda j, be, nu: (jnp.maximum(j - 1, 0), 0, 0),
                               memory_space=pltpu.SMEM),
                  pl.BlockSpec((1, 1, MOE_BLOCK), lambda j, be, nu: (jnp.minimum(j + 1, last), 0, 0),
                               memory_space=pltpu.SMEM),
                  pl.BlockSpec((MOE_BLOCK, 1), lambda j, be, nu: (jnp.minimum(j, last), 0)),
                  pl.BlockSpec((1, d, de), lambda j, be, nu: (be[jnp.minimum(j, last)], 0, 0)),
                  pl.BlockSpec((1, d, de), lambda j, be, nu: (be[jnp.minimum(j, last)], 0, 0)),
                  pl.BlockSpec((1, de, d), lambda j, be, nu: (be[jnp.minimum(j, last)], 0, 0)),
                  pl.BlockSpec(memory_space=pl.ANY)],
        out_specs=pl.BlockSpec(memory_space=pl.ANY),
        scratch_shapes=[pltpu.VMEM((MOE_BLOCK, d), F32), pltpu.VMEM((MOE_BLOCK, d), F32),
                        pltpu.VMEM((MOE_BLOCK, d), F32), pltpu.VMEM((MOE_BLOCK, d), F32),
                        pltpu.VMEM((d, de), BF16), pltpu.VMEM((d, de), BF16), pltpu.VMEM((de, d), BF16),
                        pltpu.SemaphoreType.DMA((2,)), pltpu.SemaphoreType.DMA((2,))],
    )
    return pl.pallas_call(
        functools.partial(_expert_kernel, n_tokens=n),
        grid_spec=grid_spec,
        out_shape=jax.ShapeDtypeStruct((2 * n + 2 * MOE_BLOCK, d), F32),
        compiler_params=_params("arbitrary"),
        name="moe_experts",
    )(block_e, n_used, code3, code3, row_w, w_gate, w_up, w_down, u)


def _combine_kernel(h_ref, y0_ref, y1_ref, nw_ref, o_ref, *, final_norm):
    out = h_ref[...] + (y0_ref[...] + y1_ref[...])
    o_ref[...] = _rmsnorm(out, nw_ref[...]) if final_norm else out


def _combine(h, y_slots, nw, t, final_norm):
    n, d = h.shape
    nt = n // t
    return pl.pallas_call(
        functools.partial(_combine_kernel, final_norm=final_norm),
        grid=(nt,),
        in_specs=[pl.BlockSpec((t, d), lambda i: (i, 0)),
                  pl.BlockSpec((t, d), lambda i: (i, 0)),
                  pl.BlockSpec((t, d), lambda i: (nt + i, 0)),
                  pl.BlockSpec((1, d), lambda i: (0, 0))],
        out_specs=pl.BlockSpec((t, d), lambda i: (i, 0)),
        out_shape=jax.ShapeDtypeStruct((n, d), F32),
        compiler_params=_params("parallel"),
        name="moe_combine",
    )(h, y_slots, y_slots, nw)


def _tile(n, pref):
    return pref if n % pref == 0 else n


def kernel(x, mem, norm1_w, w_in, rnn_conv_w, rnn_conv_b, rglru_wa, rglru_ba, rglru_wx, rglru_bx, rglru_lambda, w_branch_a, dn_conv_w, dn_a_log, dn_dt_bias, dn_norm_w, w_branch_b, w_out, norm2_w, mem_norm_w, w_cq, w_ckv, w_co, norm3_w, w_router_group, b_router_group, w_router_expert, b_router_expert, w_exp_gate, w_exp_up, w_exp_down, norm_f_w):
    bsz, seq, d = x.shape
    n = bsz * seq
    n_mem = mem.shape[1]
    depth = w_in.shape[0]
    d_rnn = rnn_conv_w.shape[2]
    n_heads = dn_a_log.shape[1]
    dn_w = n_heads * HEAD_DIM
    row = lambda v: v.reshape(1, -1).astype(F32)

    h = x.reshape(n, d)
    mem2 = mem.reshape(bsz * n_mem, d)
    tm = _tile(n, 1024)
    for l in range(depth):
        o_rg, o_qkv, o_z = d_rnn, 2 * d_rnn, 2 * d_rnn + 3 * dn_w
        o_a = o_z + dn_w
        o_ga = o_a + 2 * n_heads
        wi = w_in[l]
        w_pa = wi[:, :o_qkv].astype(BF16)
        w_pb = jnp.concatenate([wi[:, o_qkv:o_a], wi[:, o_ga:]], axis=1).astype(BF16)
        w_ab = jnp.pad(wi[:, o_a:o_ga], ((0, 0), (0, LANES - 2 * n_heads))).astype(BF16)
        n1 = row(norm1_w[l])
        proj_a = _norm_mm(h, n1, w_pa, BF16, tm, _tile(w_pa.shape[1], 512))
        proj_b = _norm_mm(h, n1, w_pb, BF16, tm, _tile(w_pb.shape[1], 512))
        ab = _norm_mm(h, n1, w_ab, F32, tm, LANES)

        gated_a = _rglru(proj_a, bsz, seq, rnn_conv_w[l], row(rnn_conv_b[l]), rglru_wa[l].astype(BF16),
                         row(rglru_ba[l]), rglru_wx[l].astype(BF16), row(rglru_bx[l]), row(rglru_lambda[l]),
                         _tile(seq, 512))

        a_dec = jnp.exp(dn_a_log[l].astype(F32))
        pad_h = lambda v: jnp.pad(v, (0, LANES - n_heads))
        acol, dcol = row(pad_h(a_dec)), row(pad_h(dn_dt_bias[l]))
        arow = jnp.broadcast_to(jnp.pad(a_dec, (0, n_heads))[:, None], (2 * n_heads, GDN_CHUNK))
        drow = jnp.broadcast_to(jnp.pad(dn_dt_bias[l], (0, n_heads))[:, None], (2 * n_heads, GDN_CHUNK))
        abt = ab[:, :2 * n_heads].T
        gated_b = _gdn(proj_b, ab, abt, bsz, seq, dn_conv_w[l], acol, dcol, arow, drow, row(dn_norm_w[l]), n_heads)

        h = _merge(h, gated_a, gated_b, proj_b, w_branch_a[l].astype(BF16), w_branch_b[l].astype(BF16),
                   w_out[l].astype(BF16), _tile(n, 512))

        kv = _norm_mm(mem2, row(mem_norm_w[l]), w_ckv[l].astype(BF16), BF16, _tile(bsz * n_mem, 1024), 512)
        h = _cross(h, row(norm2_w[l]), w_cq[l].astype(BF16), kv, w_co[l].astype(BF16), bsz, seq, n_mem,
                   _tile(seq, 512))

        w_r = jnp.pad(jnp.concatenate([w_router_expert[l], w_router_group[l]], axis=1),
                      ((0, 0), (0, LANES - N_EXPERTS - N_GROUPS)))
        b_r = row(jnp.pad(jnp.concatenate([b_router_expert[l], b_router_group[l]]), (0, LANES - N_EXPERTS - N_GROUPS)))
        u3, idx, wt, cnt = _router(h, row(norm3_w[l]), w_r, b_r, _tile(n, 512))

        counts = cnt[0, :N_EXPERTS].astype(I32)
        padded = (counts + MOE_BLOCK - 1) // MOE_BLOCK * MOE_BLOCK
        pend = jnp.cumsum(padded)
        pstart = pend - padded
        dest = pstart[idx[:, 0:2]] + idx[:, 2:4]
        n_blocks = (2 * n + N_EXPERTS * (MOE_BLOCK - 1)) // MOE_BLOCK
        block_row = jnp.arange(n_blocks, dtype=I32) * MOE_BLOCK
        block_e = jnp.minimum(jnp.sum((pend[None, :] <= block_row[:, None]).astype(I32), axis=1), N_EXPERTS - 1)
        n_used = (pend[-1:] // MOE_BLOCK).astype(I32)
        flat_dest = dest.reshape(2 * n)
        rows = n_blocks * MOE_BLOCK
        code = jnp.full((rows,), -1, I32).at[flat_dest].set(jnp.arange(2 * n, dtype=I32), unique_indices=True)
        row_w = jnp.zeros((rows,), F32).at[flat_dest].set(wt[:, 0:2].reshape(2 * n), unique_indices=True)
        y_slots = _experts(block_e, n_used, code.reshape(n_blocks, 1, MOE_BLOCK), row_w.reshape(rows, 1), u3,
                           w_exp_gate[l], w_exp_up[l], w_exp_down[l])
        h = _combine(h, y_slots, row(norm_f_w), _tile(n, 512), final_norm=(l == depth - 1))
    return h.reshape(bsz, seq, d)
```

```python
import functools
import math

import jax
import jax.numpy as jnp
from jax import lax
from jax.experimental import pallas as pl
from jax.experimental.pallas import tpu as pltpu

F32 = jnp.float32
BF16 = jnp.bfloat16
I32 = jnp.int32
HIGHEST = lax.Precision.HIGHEST

NORM_EPS = 1e-6
CONV_TAPS = 4
RNN_BLOCK = 128
RG_POWER = 8.0
HEAD_DIM = 128
GDN_CHUNK = 128
CA_HEADS = 4
N_GROUPS = 8
GROUP_SIZE = 8
N_EXPERTS = N_GROUPS * GROUP_SIZE
MOE_BLOCK = 256
LANES = 128
SUBLANES = 8
DMA_UNROLL = 8
VMEM_LIMIT = 48 * 1024 * 1024


def _params(*semantics):
    return pltpu.CompilerParams(dimension_semantics=semantics, vmem_limit_bytes=VMEM_LIMIT)


def _dot(a, b):
    return jnp.dot(a.astype(BF16), b.astype(BF16), preferred_element_type=F32)


def _dot_nt(a, b):
    return lax.dot_general(a.astype(BF16), b.astype(BF16), (((1,), (1,)), ((), ())),
                           preferred_element_type=F32)


def _dot_tn(a, b):
    return lax.dot_general(a.astype(BF16), b.astype(BF16), (((0,), (0,)), ((), ())),
                           preferred_element_type=F32)


def _dot_f32(a, b):
    return jnp.dot(a, b, precision=HIGHEST, preferred_element_type=F32)


def _split(x):
    hi = x.astype(BF16)
    return hi, (x - hi.astype(F32)).astype(BF16)


def _dot_split(a_parts, b_parts):
    (a_hi, a_lo), (b_hi, b_lo) = a_parts, b_parts
    dot = functools.partial(jnp.dot, preferred_element_type=F32)
    return dot(a_hi, b_hi) + (dot(a_hi, b_lo) + dot(a_lo, b_hi))


def _rmsnorm(x, w):
    return x * lax.rsqrt(jnp.mean(x * x, axis=-1, keepdims=True) + NORM_EPS) * w


def _sigmoid(x):
    return 0.5 * jnp.tanh(0.5 * x) + 0.5


def _silu(x):
    return x * _sigmoid(x)


def _softplus(x):
    return jnp.maximum(x, 0.0) + jnp.log(1.0 + jnp.exp(-jnp.abs(x)))


def _one_minus_exp2(y, exp_y):
    u = exp_y * exp_y
    d = 1.0 - u
    return jnp.where(y > -0.125, jnp.where(d == 0.0, -2.0 * y, d * (2.0 * y) / jnp.log(u)), d)


def _gelu_tanh(x):
    return 0.5 * x * (1.0 + jnp.tanh(math.sqrt(2.0 / math.pi) * (x + 0.044715 * (x * x * x))))


def _norm_mm_kernel(x_ref, nw_ref, w_ref, o_ref, u_ref):
    @pl.when(pl.program_id(1) == 0)
    def _():
        u_ref[...] = _rmsnorm(x_ref[...], nw_ref[...]).astype(BF16)

    o_ref[...] = jnp.dot(u_ref[...], w_ref[...], preferred_element_type=F32).astype(o_ref.dtype)


def _norm_mm(x, nw, w, out_dtype, tm, tn):
    n, d = x.shape
    c = w.shape[1]
    return pl.pallas_call(
        _norm_mm_kernel,
        grid=(n // tm, c // tn),
        in_specs=[pl.BlockSpec((tm, d), lambda i, j: (i, 0)),
                  pl.BlockSpec((1, d), lambda i, j: (0, 0)),
                  pl.BlockSpec((d, tn), lambda i, j: (0, j))],
        out_specs=pl.BlockSpec((tm, tn), lambda i, j: (i, j)),
        out_shape=jax.ShapeDtypeStruct((n, c), out_dtype),
        scratch_shapes=[pltpu.VMEM((tm, d), BF16)],
        compiler_params=_params("parallel", "arbitrary"),
        name="norm_mm",
    )(x, nw, w)


def _load_conv_window(x_ref, xbuf, ts):
    @pl.when(pl.program_id(1) == 0)
    def _():
        xbuf[0:SUBLANES, :] = jnp.zeros((SUBLANES, xbuf.shape[1]), F32)

    @pl.when(pl.program_id(1) != 0)
    def _():
        xbuf[0:SUBLANES, :] = xbuf[ts:ts + SUBLANES, :]

    xbuf[SUBLANES:SUBLANES + ts, :] = x_ref[...].astype(F32)


def _causal_conv(xbuf, cw_ref, ts):
    base = SUBLANES - (CONV_TAPS - 1)
    acc = cw_ref[0:1, :] * xbuf[base:base + ts, :]
    for k in range(1, CONV_TAPS):
        acc = acc + cw_ref[k:k + 1, :] * xbuf[base + k:base + k + ts, :]
    return acc


def _rglru_kernel(rx_ref, rg_ref, cw_ref, cb_ref, wa_ref, ba_ref, wx_ref, bx_ref, lam_ref, o_ref,
                  xbuf, a_ref, b_ref, carry_ref):
    ts, c = rx_ref.shape
    _load_conv_window(rx_ref, xbuf, ts)

    @pl.when(pl.program_id(1) == 0)
    def _():
        carry_ref[...] = jnp.zeros_like(carry_ref)

    xc = _causal_conv(xbuf, cw_ref, ts) + cb_ref[...]
    neg_sp = -RG_POWER * _softplus(-lam_ref[...])
    for n in range(c // RNN_BLOCK):
        sl = slice(n * RNN_BLOCK, (n + 1) * RNN_BLOCK)
        xb = xc[:, sl]
        r = _sigmoid(_dot(xb, wa_ref[n]) + ba_ref[:, sl])
        i = _sigmoid(_dot(xb, wx_ref[n]) + bx_ref[:, sl])
        log_a = neg_sp[:, sl] * r
        a = jnp.exp(log_a)
        a_ref[:, sl] = a
        b_ref[:, sl] = jnp.sqrt(_one_minus_exp2(log_a, a)) * (i * xb)

    row = lax.broadcasted_iota(I32, (SUBLANES, c), 0)

    def slab(t, carry):
        rows = pl.ds(pl.multiple_of(t * SUBLANES, SUBLANES), SUBLANES)
        a = a_ref[rows, :]
        b = b_ref[rows, :]
        for d in (1, 2, 4):
            a_sh = jnp.where(row >= d, pltpu.roll(a, d, 0), 1.0)
            b_sh = jnp.where(row >= d, pltpu.roll(b, d, 0), 0.0)
            b = a * b_sh + b
            a = a * a_sh
        h = a * carry + b
        b_ref[rows, :] = h
        return h[SUBLANES - 1:SUBLANES, :]

    carry_ref[...] = lax.fori_loop(0, ts // SUBLANES, slab, carry_ref[...])
    o_ref[...] = (_gelu_tanh(rg_ref[...].astype(F32)) * b_ref[...]).astype(o_ref.dtype)


def _rglru(proj_a, bsz, seq, cw, cb, wa, ba, wx, bx, lam, ts):
    c = cw.shape[1]
    ns = seq // ts
    full = lambda shape: pl.BlockSpec(shape, lambda b, s: (0,) * len(shape))
    return pl.pallas_call(
        _rglru_kernel,
        grid=(bsz, ns),
        in_specs=[pl.BlockSpec((ts, c), lambda b, s: (b * ns + s, 0)),
                  pl.BlockSpec((ts, c), lambda b, s: (b * ns + s, 1)),
                  full(cw.shape), full(cb.shape), full(wa.shape), full(ba.shape),
                  full(wx.shape), full(bx.shape), full(lam.shape)],
        out_specs=pl.BlockSpec((ts, c), lambda b, s: (b * ns + s, 0)),
        out_shape=jax.ShapeDtypeStruct((bsz * seq, c), BF16),
        scratch_shapes=[pltpu.VMEM((ts + SUBLANES, c), F32), pltpu.VMEM((ts, c), F32),
                        pltpu.VMEM((ts, c), F32), pltpu.VMEM((1, c), F32)],
        compiler_params=_params("parallel", "arbitrary"),
        name="rglru",
    )(proj_a, proj_a, cw, cb, wa, ba, wx, bx, lam)


def _lane_bcast(x, lane):
    return jnp.broadcast_to(x[:, lane:lane + 1], x.shape)


def _unit_lower_inverses(neg_ls, eye):
    ts = eye.shape[0]
    ts_mats = [eye + n for n in neg_ls]
    ps = [_dot_split(_split(n), _split(n)) for n in neg_ls]
    levels = int(math.log2(ts)) - 1
    for lvl in range(levels - 1):
        p_parts = [_split(p) for p in ps]
        lhs_parts = [tuple(jnp.concatenate([a, b], axis=0) for a, b in zip(_split(t), pp))
                     for t, pp in zip(ts_mats, p_parts)]
        both = [_dot_split(lp, pp) for lp, pp in zip(lhs_parts, p_parts)]
        ts_mats = [t + b[:ts] for t, b in zip(ts_mats, both)]
        ps = [b[ts:] for b in both]
    return [t + _dot_split(_split(t), _split(p)) for t, p in zip(ts_mats, ps)]


def _gdn_kernel(qkv_ref, z_ref, ab_ref, abt_ref, cw_ref, acol_ref, dcol_ref, arow_ref, drow_ref, nw_ref,
                o_ref, xbuf, s_ref):
    ts = qkv_ref.shape[0]
    n_heads = s_ref.shape[0]
    dn_w = n_heads * HEAD_DIM
    _load_conv_window(qkv_ref, xbuf, ts)

    @pl.when(pl.program_id(1) == 0)
    def _():
        s_ref[...] = jnp.zeros_like(s_ref)

    row = lax.broadcasted_iota(I32, (ts, ts), 0)
    col = lax.broadcasted_iota(I32, (ts, ts), 1)
    lower_incl = (row >= col).astype(F32)
    upper_incl = (row <= col).astype(F32)
    eye = (row == col).astype(F32)

    ab = ab_ref[...]
    g_col = -acol_ref[...] * _softplus(ab + dcol_ref[...])
    beta_col = _sigmoid(ab)
    g_row = -arow_ref[...] * _softplus(abt_ref[...] + drow_ref[...])
    cum_col = _dot_f32(lower_incl, g_col)
    cum_row = _dot_f32(g_row, upper_incl)
    last_col = jnp.broadcast_to(cum_col[ts - 1:ts, :], cum_col.shape)
    exp_cum = jnp.exp(cum_col)
    exp_rem = jnp.exp(last_col - cum_col)
    exp_last = jnp.exp(cum_col[ts - 1:ts, :])

    qkv = _silu(_causal_conv(xbuf, cw_ref, ts))

    heads = range(n_heads)
    head_cols = lambda h, part: slice(part * dn_w + h * HEAD_DIM, part * dn_w + (h + 1) * HEAD_DIM)
    qs, ks, k_betas, decays, rhs = [], [], [], [], []
    for h in heads:
        q = qkv[:, head_cols(h, 0)]
        k = qkv[:, head_cols(h, 1)]
        v = qkv[:, head_cols(h, 2)]
        q = q * (lax.rsqrt(jnp.sum(q * q, axis=-1, keepdims=True) + NORM_EPS) * (HEAD_DIM ** -0.5))
        k = k * lax.rsqrt(jnp.sum(k * k, axis=-1, keepdims=True) + NORM_EPS)
        beta = _lane_bcast(beta_col, n_heads + h)
        k_beta = k * beta
        qs.append(q)
        ks.append(k)
        k_betas.append(k_beta)
        decays.append(jnp.exp(jnp.minimum(_lane_bcast(cum_col, h) - cum_row[h:h + 1, :], 0.0)))
        rhs.append(jnp.concatenate([v * beta, k_beta * _lane_bcast(exp_cum, h)], axis=1))
    kks = [_dot_nt(k_betas[h], ks[h]) for h in heads]
    qks = [_dot_nt(qs[h], ks[h]) for h in heads]
    neg_ls = [jnp.where(row > col, -(kks[h] * decays[h]), 0.0) for h in heads]
    t_mats = _unit_lower_inverses(neg_ls, eye)
    uws = [_dot(t_mats[h], rhs[h]) for h in heads]
    ws_qs = [_dot(jnp.concatenate([uws[h][:, HEAD_DIM:], qs[h] * _lane_bcast(exp_cum, h)], axis=0), s_ref[h])
             for h in heads]
    v_news = [uws[h][:, :HEAD_DIM] - ws_qs[h][:ts] for h in heads]
    intras = [jnp.where(row >= col, qks[h] * decays[h], 0.0) for h in heads]
    outs = [ws_qs[h][ts:] + _dot(intras[h], v_news[h]) for h in heads]
    kvs = [_dot((ks[h] * _lane_bcast(exp_rem, h)).T, v_news[h]) for h in heads]
    for h in heads:
        s_ref[h] = s_ref[h] * _lane_bcast(jnp.broadcast_to(exp_last, (HEAD_DIM, LANES)), h) + kvs[h]
        o = _rmsnorm(outs[h], nw_ref[...]) * _silu(z_ref[:, head_cols(h, 0)].astype(F32))
        o_ref[:, head_cols(h, 0)] = o.astype(o_ref.dtype)


def _gdn(proj_b, ab, abt, bsz, seq, cw, acol, dcol, arow, drow, nw, n_heads):
    ts = GDN_CHUNK
    ns = seq // ts
    dn_w = n_heads * HEAD_DIM
    full = lambda shape: pl.BlockSpec(shape, lambda b, s: (0,) * len(shape))
    return pl.pallas_call(
        _gdn_kernel,
        grid=(bsz, ns),
        in_specs=[pl.BlockSpec((ts, 3 * dn_w), lambda b, s: (b * ns + s, 0)),
                  pl.BlockSpec((ts, dn_w), lambda b, s: (b * ns + s, 3)),
                  pl.BlockSpec((ts, LANES), lambda b, s: (b * ns + s, 0)),
                  pl.BlockSpec((2 * n_heads, ts), lambda b, s: (0, b * ns + s)),
                  full(cw.shape), full(acol.shape), full(dcol.shape), full(arow.shape), full(drow.shape),
                  full(nw.shape)],
        out_specs=pl.BlockSpec((ts, dn_w), lambda b, s: (b * ns + s, 0)),
        out_shape=jax.ShapeDtypeStruct((bsz * seq, dn_w), BF16),
        scratch_shapes=[pltpu.VMEM((ts + SUBLANES, 3 * dn_w), F32),
                        pltpu.VMEM((n_heads, HEAD_DIM, HEAD_DIM), F32)],
        compiler_params=_params("parallel", "arbitrary"),
        name="gdn",
    )(proj_b, proj_b, ab, abt, cw, acol, dcol, arow, drow, nw)


def _merge_kernel(x_ref, ya_ref, yb_ref, ga_ref, gb_ref, wa_ref, wb_ref, wo_ref, o_ref):
    y_a = jnp.dot(ya_ref[...], wa_ref[...], preferred_element_type=F32)
    y_b = jnp.dot(yb_ref[...], wb_ref[...], preferred_element_type=F32)
    m = _sigmoid(ga_ref[...].astype(F32)) * y_a + _sigmoid(gb_ref[...].astype(F32)) * y_b
    o_ref[...] = x_ref[...] + _dot(m, wo_ref[...])


def _merge(x, gated_a, gated_b, proj_b, w_a, w_b, w_o, tm):
    n, d = x.shape
    full = lambda shape: pl.BlockSpec(shape, lambda i: (0,) * len(shape))
    return pl.pallas_call(
        _merge_kernel,
        grid=(n // tm,),
        in_specs=[pl.BlockSpec((tm, d), lambda i: (i, 0)),
                  pl.BlockSpec((tm, gated_a.shape[1]), lambda i: (i, 0)),
                  pl.BlockSpec((tm, gated_b.shape[1]), lambda i: (i, 0)),
                  pl.BlockSpec((tm, d), lambda i: (i, 4)),
                  pl.BlockSpec((tm, d), lambda i: (i, 5)),
                  full(w_a.shape), full(w_b.shape), full(w_o.shape)],
        out_specs=pl.BlockSpec((tm, d), lambda i: (i, 0)),
        out_shape=jax.ShapeDtypeStruct((n, d), F32),
        compiler_params=_params("parallel"),
        name="merge",
    )(x, gated_a, gated_b, proj_b, proj_b, w_a, w_b, w_o)


def _cross_kernel(h_ref, nw_ref, wq_ref, kv_ref, wo_ref, o_ref):
    x = h_ref[...]
    d = x.shape[1]
    hd = d // CA_HEADS
    u = _rmsnorm(x, nw_ref[...])
    q = _dot(u, wq_ref[...])
    outs = []
    for h in range(CA_HEADS):
        k_h = kv_ref[:, h * hd:(h + 1) * hd]
        v_h = kv_ref[:, d + h * hd:d + (h + 1) * hd]
        s = _dot_nt(q[:, h * hd:(h + 1) * hd], k_h) * (hd ** -0.5)
        s = s - jnp.max(s, axis=-1, keepdims=True)
        e = jnp.exp(s)
        p = e / jnp.sum(e, axis=-1, keepdims=True)
        outs.append(_dot(p, v_h))
    o = jnp.concatenate(outs, axis=1)
    o_ref[...] = x + _dot(o, wo_ref[...])


def _cross(h, nw, w_q, kv, w_o, bsz, seq, n_mem, ts):
    n, d = h.shape
    ns = seq // ts
    full = lambda shape: pl.BlockSpec(shape, lambda b, s: (0,) * len(shape))
    return pl.pallas_call(
        _cross_kernel,
        grid=(bsz, ns),
        in_specs=[pl.BlockSpec((ts, d), lambda b, s: (b * ns + s, 0)),
                  full(nw.shape), full(w_q.shape),
                  pl.BlockSpec((n_mem, 2 * d), lambda b, s: (b, 0)),
                  full(w_o.shape)],
        out_specs=pl.BlockSpec((ts, d), lambda b, s: (b * ns + s, 0)),
        out_shape=jax.ShapeDtypeStruct((n, d), F32),
        compiler_params=_params("parallel", "parallel"),
        name="cross_attn",
    )(h, nw, w_q, kv, w_o)


def _router_kernel(h_ref, nw_ref, wr_ref, br_ref, u_ref, idx_ref, wt_ref, cnt_ref, base_ref):
    t = h_ref.shape[0]

    @pl.when(pl.program_id(0) == 0)
    def _():
        base_ref[...] = jnp.zeros_like(base_ref)

    u = _rmsnorm(h_ref[...], nw_ref[...])
    u_ref[...] = u
    logits = _dot_f32(u, wr_ref[...]) + br_ref[...]
    lane = lax.broadcasted_iota(I32, (t, LANES), 1)
    lanef = lane.astype(F32)
    big = float(LANES)
    neg = -jnp.inf

    lg = jnp.where((lane >= N_EXPERTS) & (lane < N_EXPERTS + N_GROUPS), logits, neg)
    gmax = jnp.max(lg, axis=-1, keepdims=True)
    grp = jnp.min(jnp.where(lg == gmax, lanef - float(N_EXPERTS), big), axis=-1, keepdims=True)
    p_grp = 1.0 / jnp.sum(jnp.exp(lg - gmax), axis=-1, keepdims=True)

    in_grp = (lane < N_EXPERTS) & ((lane // GROUP_SIZE).astype(F32) == grp)
    le = jnp.where(in_grp, logits, neg)
    m1 = jnp.max(le, axis=-1, keepdims=True)
    i1 = jnp.min(jnp.where(le == m1, lanef, big), axis=-1, keepdims=True)
    le2 = jnp.where(lanef == i1, neg, le)
    m2 = jnp.max(le2, axis=-1, keepdims=True)
    i2 = jnp.min(jnp.where(le2 == m2, lanef, big), axis=-1, keepdims=True)
    ratio = jnp.exp(m2 - m1)
    p1 = 1.0 / (1.0 + ratio)
    p2 = ratio * p1

    oh1 = lanef == i1
    oh2 = lanef == i2
    onehot = jnp.where(oh1 | oh2, 1.0, 0.0)
    r_i = lax.broadcasted_iota(I32, (t, t), 0)
    c_i = lax.broadcasted_iota(I32, (t, t), 1)
    before = jnp.where(r_i > c_i, 1.0, 0.0)
    rank = _dot(before, onehot) + base_ref[...]
    r1 = jnp.sum(jnp.where(oh1, rank, 0.0), axis=-1, keepdims=True)
    r2 = jnp.sum(jnp.where(oh2, rank, 0.0), axis=-1, keepdims=True)
    base_ref[...] = base_ref[...] + jnp.sum(onehot, axis=0, keepdims=True)

    idx = jnp.where(lane == 0, i1, jnp.where(lane == 1, i2, jnp.where(lane == 2, r1, jnp.where(lane == 3, r2, 0.0))))
    idx_ref[...] = idx.astype(I32)
    wt_ref[...] = jnp.where(lane == 0, p_grp * p1, jnp.where(lane == 1, p_grp * p2, 0.0))
    cnt_ref[...] = jnp.broadcast_to(base_ref[...], cnt_ref.shape)


def _router(h, nw, wr, br, t):
    n, d = h.shape
    full = lambda shape: pl.BlockSpec(shape, lambda i: (0,) * len(shape))
    return pl.pallas_call(
        _router_kernel,
        grid=(n // t,),
        in_specs=[pl.BlockSpec((t, d), lambda i: (i, 0)), full(nw.shape), full(wr.shape), full(br.shape)],
        out_specs=[pl.BlockSpec((t, d), lambda i: (i, 0)),
                   pl.BlockSpec((t, LANES), lambda i: (i, 0)),
                   pl.BlockSpec((t, LANES), lambda i: (i, 0)),
                   pl.BlockSpec((SUBLANES, LANES), lambda i: (0, 0))],
        out_shape=[jax.ShapeDtypeStruct((n, d), F32),
                   jax.ShapeDtypeStruct((n, LANES), I32),
                   jax.ShapeDtypeStruct((n, LANES), F32),
                   jax.ShapeDtypeStruct((SUBLANES, LANES), F32)],
        scratch_shapes=[pltpu.VMEM((1, LANES), F32)],
        compiler_params=_params("arbitrary"),
        name="router",
    )(h, nw, wr, br)


def _row_copy(src, src_row, dst, dst_row, sem):
    return pltpu.make_async_copy(src.at[pl.ds(src_row, 1)], dst.at[pl.ds(dst_row, 1)], sem)


def _dispatch_kernel(dest_ref, u_ref, xs_in_ref, xs_ref, sem):
    del xs_in_ref
    t = u_ref.shape[0]

    def start(r, carry):
        _row_copy(u_ref, r, xs_ref, dest_ref[0, 0, 2 * r], sem).start()
        _row_copy(u_ref, r, xs_ref, dest_ref[0, 0, 2 * r + 1], sem).start()
        return carry

    def wait(r, carry):
        _row_copy(u_ref, r, xs_ref, dest_ref[0, 0, 2 * r], sem).wait()
        _row_copy(u_ref, r, xs_ref, dest_ref[0, 0, 2 * r + 1], sem).wait()
        return carry

    lax.fori_loop(0, t, start, 0, unroll=DMA_UNROLL)
    lax.fori_loop(0, t, wait, 0, unroll=DMA_UNROLL)


def _dispatch(dest3, u, xs_zero, t):
    n, d = u.shape
    return pl.pallas_call(
        _dispatch_kernel,
        grid=(n // t,),
        in_specs=[pl.BlockSpec((1, 1, 2 * t), lambda i: (i, 0, 0), memory_space=pltpu.SMEM),
                  pl.BlockSpec((t, d), lambda i: (i, 0)),
                  pl.BlockSpec(memory_space=pl.ANY)],
        out_specs=pl.BlockSpec(memory_space=pl.ANY),
        out_shape=jax.ShapeDtypeStruct(xs_zero.shape, xs_zero.dtype),
        scratch_shapes=[pltpu.SemaphoreType.DMA(())],
        input_output_aliases={2: 0},
        compiler_params=_params("arbitrary"),
        name="moe_dispatch",
    )(dest3, u, xs_zero)


def _expert_kernel(be_ref, nu_ref, xs_ref, wg_ref, wu_ref, wd_ref, y_ref, wg_bf, wu_bf, wd_bf):
    j = pl.program_id(0)
    used = j < nu_ref[0]
    new_expert = (j == 0) | (be_ref[j] != be_ref[jnp.maximum(j - 1, 0)])

    @pl.when(used & new_expert)
    def _():
        wg_bf[...] = wg_ref[0].astype(BF16)
        wu_bf[...] = wu_ref[0].astype(BF16)
        wd_bf[...] = wd_ref[0].astype(BF16)

    @pl.when(used)
    def _():
        x = xs_ref[...].astype(BF16)
        hid = _silu(jnp.dot(x, wg_bf[...], preferred_element_type=F32)) * jnp.dot(
            x, wu_bf[...], preferred_element_type=F32)
        y_ref[...] = _dot(hid, wd_bf[...])

    @pl.when(jnp.logical_not(used))
    def _():
        y_ref[...] = jnp.zeros_like(y_ref)


def _experts(block_e, n_used, xs, w_gate, w_up, w_down):
    rows, d = xs.shape
    de = w_gate.shape[2]
    grid_spec = pltpu.PrefetchScalarGridSpec(
        num_scalar_prefetch=2,
        grid=(rows // MOE_BLOCK,),
        in_specs=[pl.BlockSpec((MOE_BLOCK, d), lambda j, be, nu: (j, 0)),
                  pl.BlockSpec((1, d, de), lambda j, be, nu: (be[j], 0, 0)),
                  pl.BlockSpec((1, d, de), lambda j, be, nu: (be[j], 0, 0)),
                  pl.BlockSpec((1, de, d), lambda j, be, nu: (be[j], 0, 0))],
        out_specs=pl.BlockSpec((MOE_BLOCK, d), lambda j, be, nu: (j, 0)),
        scratch_shapes=[pltpu.VMEM((d, de), BF16), pltpu.VMEM((d, de), BF16), pltpu.VMEM((de, d), BF16)],
    )
    return pl.pallas_call(
        _expert_kernel,
        grid_spec=grid_spec,
        out_shape=jax.ShapeDtypeStruct((rows, d), F32),
        compiler_params=_params("arbitrary"),
        name="moe_experts",
    )(block_e, n_used, xs, w_gate, w_up, w_down)


def _combine_kernel(dest_ref, h_ref, wt_ref, nw_ref, y_ref, o_ref, buf, sem, *, final_norm):
    t = h_ref.shape[0]

    def start(r, carry):
        _row_copy(y_ref, dest_ref[0, 0, 2 * r], buf.at[0], r, sem).start()
        _row_copy(y_ref, dest_ref[0, 0, 2 * r + 1], buf.at[1], r, sem).start()
        return carry

    def wait(r, carry):
        _row_copy(y_ref, dest_ref[0, 0, 2 * r], buf.at[0], r, sem).wait()
        _row_copy(y_ref, dest_ref[0, 0, 2 * r + 1], buf.at[1], r, sem).wait()
        return carry

    lax.fori_loop(0, t, start, 0, unroll=DMA_UNROLL)
    lax.fori_loop(0, t, wait, 0, unroll=DMA_UNROLL)
    wt = wt_ref[...]
    moe = buf[0] * wt[:, 0:1] + buf[1] * wt[:, 1:2]
    out = h_ref[...] + moe
    o_ref[...] = _rmsnorm(out, nw_ref[...]) if final_norm else out


def _combine(dest3, h, wt, nw, y, t, final_norm):
    n, d = h.shape
    return pl.pallas_call(
        functools.partial(_combine_kernel, final_norm=final_norm),
        grid=(n // t,),
        in_specs=[pl.BlockSpec((1, 1, 2 * t), lambda i: (i, 0, 0), memory_space=pltpu.SMEM),
                  pl.BlockSpec((t, d), lambda i: (i, 0)),
                  pl.BlockSpec((t, LANES), lambda i: (i, 0)),
                  pl.BlockSpec((1, d), lambda i: (0, 0)),
                  pl.BlockSpec(memory_space=pl.ANY)],
        out_specs=pl.BlockSpec((t, d), lambda i: (i, 0)),
        out_shape=jax.ShapeDtypeStruct((n, d), F32),
        scratch_shapes=[pltpu.VMEM((2, t, d), F32), pltpu.SemaphoreType.DMA(())],
        compiler_params=_params("arbitrary"),
        name="moe_combine",
    )(dest3, h, wt, nw, y)


def _tile(n, pref):
    return pref if n % pref == 0 else n


def kernel(x, mem, norm1_w, w_in, rnn_conv_w, rnn_conv_b, rglru_wa, rglru_ba, rglru_wx, rglru_bx, rglru_lambda, w_branch_a, dn_conv_w, dn_a_log, dn_dt_bias, dn_norm_w, w_branch_b, w_out, norm2_w, mem_norm_w, w_cq, w_ckv, w_co, norm3_w, w_router_group, b_router_group, w_router_expert, b_router_expert, w_exp_gate, w_exp_up, w_exp_down, norm_f_w):
    bsz, seq, d = x.shape
    n = bsz * seq
    n_mem = mem.shape[1]
    depth = w_in.shape[0]
    d_rnn = rnn_conv_w.shape[2]
    n_heads = dn_a_log.shape[1]
    dn_w = n_heads * HEAD_DIM
    row = lambda v: v.reshape(1, -1).astype(F32)

    h = x.reshape(n, d)
    mem2 = mem.reshape(bsz * n_mem, d)
    tm = _tile(n, 1024)
    for l in range(depth):
        o_rg, o_qkv, o_z = d_rnn, 2 * d_rnn, 2 * d_rnn + 3 * dn_w
        o_a = o_z + dn_w
        o_ga = o_a + 2 * n_heads
        wi = w_in[l]
        w_pa = wi[:, :o_qkv].astype(BF16)
        w_pb = jnp.concatenate([wi[:, o_qkv:o_a], wi[:, o_ga:]], axis=1).astype(BF16)
        w_ab = jnp.pad(wi[:, o_a:o_ga], ((0, 0), (0, LANES - 2 * n_heads))).astype(BF16)
        n1 = row(norm1_w[l])
        proj_a = _norm_mm(h, n1, w_pa, BF16, tm, _tile(w_pa.shape[1], 512))
        proj_b = _norm_mm(h, n1, w_pb, BF16, tm, _tile(w_pb.shape[1], 512))
        ab = _norm_mm(h, n1, w_ab, F32, tm, LANES)

        gated_a = _rglru(proj_a, bsz, seq, rnn_conv_w[l], row(rnn_conv_b[l]), rglru_wa[l].astype(BF16),
                         row(rglru_ba[l]), rglru_wx[l].astype(BF16), row(rglru_bx[l]), row(rglru_lambda[l]),
                         _tile(seq, 512))

        a_dec = jnp.exp(dn_a_log[l].astype(F32))
        pad_h = lambda v: jnp.pad(v, (0, LANES - n_heads))
        acol, dcol = row(pad_h(a_dec)), row(pad_h(dn_dt_bias[l]))
        arow = jnp.broadcast_to(jnp.pad(a_dec, (0, n_heads))[:, None], (2 * n_heads, GDN_CHUNK))
        drow = jnp.broadcast_to(jnp.pad(dn_dt_bias[l], (0, n_heads))[:, None], (2 * n_heads, GDN_CHUNK))
        abt = ab[:, :2 * n_heads].T
        gated_b = _gdn(proj_b, ab, abt, bsz, seq, dn_conv_w[l], acol, dcol, arow, drow, row(dn_norm_w[l]), n_heads)

        h = _merge(h, gated_a, gated_b, proj_b, w_branch_a[l].astype(BF16), w_branch_b[l].astype(BF16),
                   w_out[l].astype(BF16), _tile(n, 512))

        kv = _norm_mm(mem2, row(mem_norm_w[l]), w_ckv[l].astype(BF16), BF16, _tile(bsz * n_mem, 1024), 512)
        h = _cross(h, row(norm2_w[l]), w_cq[l].astype(BF16), kv, w_co[l].astype(BF16), bsz, seq, n_mem,
                   _tile(seq, 512))

        w_r = jnp.pad(jnp.concatenate([w_router_expert[l], w_router_group[l]], axis=1),
                      ((0, 0), (0, LANES - N_EXPERTS - N_GROUPS)))
        b_r = row(jnp.pad(jnp.concatenate([b_router_expert[l], b_router_group[l]]), (0, LANES - N_EXPERTS - N_GROUPS)))
        u3, idx, wt, cnt = _router(h, row(norm3_w[l]), w_r, b_r, _tile(n, 512))

        counts = cnt[0, :N_EXPERTS].astype(I32)
        padded = (counts + MOE_BLOCK - 1) // MOE_BLOCK * MOE_BLOCK
        pend = jnp.cumsum(padded)
        pstart = pend - padded
        dest = pstart[idx[:, 0:2]] + idx[:, 2:4]
        n_blocks = (2 * n + N_EXPERTS * (MOE_BLOCK - 1)) // MOE_BLOCK
        block_row = jnp.arange(n_blocks, dtype=I32) * MOE_BLOCK
        block_e = jnp.minimum(jnp.sum((pend[None, :] <= block_row[:, None]).astype(I32), axis=1), N_EXPERTS - 1)
        n_used = (pend[-1:] // MOE_BLOCK).astype(I32)
        t_moe = _tile(n, 256)
        dest3 = dest.reshape(n // t_moe, 1, 2 * t_moe)
        xs = _dispatch(dest3, u3, jnp.zeros((n_blocks * MOE_BLOCK, d), F32), t_moe)
        yb = _experts(block_e, n_used, xs, w_exp_gate[l], w_exp_up[l], w_exp_down[l])
        h = _combine(dest3, h, wt, row(norm_f_w), yb, t_moe, final_norm=(l == depth - 1))
    return h.reshape(bsz, seq, d)
```

```python
import functools
import math

import jax
import jax.numpy as jnp
from jax import lax
from jax.experimental import pallas as pl
from jax.experimental.pallas import tpu as pltpu

F32 = jnp.float32
BF16 = jnp.bfloat16
I32 = jnp.int32
HIGHEST = lax.Precision.HIGHEST

NORM_EPS = 1e-6
CONV_TAPS = 4
RNN_BLOCK = 128
RG_POWER = 8.0
HEAD_DIM = 128
GDN_CHUNK = 128
GDN_CHUNKS_PER_STEP = 2
CA_HEADS = 4
N_GROUPS = 8
GROUP_SIZE = 8
N_EXPERTS = N_GROUPS * GROUP_SIZE
MOE_BLOCK = 256
LANES = 128
SUBLANES = 8
DMA_UNROLL = 8
VMEM_LIMIT = 48 * 1024 * 1024


def _params(*semantics):
    return pltpu.CompilerParams(dimension_semantics=semantics, vmem_limit_bytes=VMEM_LIMIT)


def _dot(a, b):
    return jnp.dot(a.astype(BF16), b.astype(BF16), preferred_element_type=F32)


def _dot_nt(a, b):
    return lax.dot_general(a.astype(BF16), b.astype(BF16), (((1,), (1,)), ((), ())),
                           preferred_element_type=F32)


def _dot_tn(a, b):
    return lax.dot_general(a.astype(BF16), b.astype(BF16), (((0,), (0,)), ((), ())),
                           preferred_element_type=F32)


def _dot_f32(a, b):
    return jnp.dot(a, b, precision=HIGHEST, preferred_element_type=F32)


def _split(x):
    hi = x.astype(BF16)
    return hi, (x - hi.astype(F32)).astype(BF16)


def _dot_split(a_parts, b_parts):
    (a_hi, a_lo), (b_hi, b_lo) = a_parts, b_parts
    dot = functools.partial(jnp.dot, preferred_element_type=F32)
    return dot(a_hi, b_hi) + (dot(a_hi, b_lo) + dot(a_lo, b_hi))


def _rmsnorm(x, w):
    return x * lax.rsqrt(jnp.mean(x * x, axis=-1, keepdims=True) + NORM_EPS) * w


def _sigmoid(x):
    return 0.5 * jnp.tanh(0.5 * x) + 0.5


def _silu(x):
    return x * _sigmoid(x)


def _softplus(x):
    return jnp.maximum(x, 0.0) + jnp.log(1.0 + jnp.exp(-jnp.abs(x)))


def _one_minus_exp2(y, exp_y):
    return jnp.tanh(-y) * (1.0 + exp_y * exp_y)


def _gelu_tanh(x):
    return 0.5 * x * (1.0 + jnp.tanh(math.sqrt(2.0 / math.pi) * (x + 0.044715 * (x * x * x))))


def _norm_mm_kernel(x_ref, nw_ref, w_ref, o_ref, u_ref):
    @pl.when(pl.program_id(1) == 0)
    def _():
        u_ref[...] = _rmsnorm(x_ref[...], nw_ref[...]).astype(BF16)

    o_ref[...] = jnp.dot(u_ref[...], w_ref[...], preferred_element_type=F32).astype(o_ref.dtype)


def _norm_mm(x, nw, w, out_dtype, tm, tn):
    n, d = x.shape
    c = w.shape[1]
    return pl.pallas_call(
        _norm_mm_kernel,
        grid=(n // tm, c // tn),
        in_specs=[pl.BlockSpec((tm, d), lambda i, j: (i, 0)),
                  pl.BlockSpec((1, d), lambda i, j: (0, 0)),
                  pl.BlockSpec((d, tn), lambda i, j: (0, j))],
        out_specs=pl.BlockSpec((tm, tn), lambda i, j: (i, j)),
        out_shape=jax.ShapeDtypeStruct((n, c), out_dtype),
        scratch_shapes=[pltpu.VMEM((tm, d), BF16)],
        compiler_params=_params("parallel", "arbitrary"),
        name="norm_mm",
    )(x, nw, w)


def _in_proj_kernel(x_ref, nw_ref, w_ref, wab_ref, oa_ref, ob_ref, ab_ref, u_ref, *, n_a):
    j = pl.program_id(1)

    @pl.when(j == 0)
    def _():
        u_ref[...] = _rmsnorm(x_ref[...], nw_ref[...]).astype(BF16)
        ab_ref[...] = jnp.dot(u_ref[...], wab_ref[...], preferred_element_type=F32)

    @pl.when(j < n_a)
    def _():
        oa_ref[...] = jnp.dot(u_ref[...], w_ref[...], preferred_element_type=F32).astype(oa_ref.dtype)

    @pl.when(j >= n_a)
    def _():
        ob_ref[...] = jnp.dot(u_ref[...], w_ref[...], preferred_element_type=F32).astype(ob_ref.dtype)


def _in_proj(x, nw, w_cat, w_ab, c_a, tm, tn):
    n, d = x.shape
    c = w_cat.shape[1]
    n_a, n_b = c_a // tn, (c - c_a) // tn
    return pl.pallas_call(
        functools.partial(_in_proj_kernel, n_a=n_a),
        grid=(n // tm, n_a + n_b),
        in_specs=[pl.BlockSpec((tm, d), lambda i, j: (i, 0)),
                  pl.BlockSpec((1, d), lambda i, j: (0, 0)),
                  pl.BlockSpec((d, tn), lambda i, j: (0, j)),
                  pl.BlockSpec(w_ab.shape, lambda i, j: (0, 0))],
        out_specs=[pl.BlockSpec((tm, tn), lambda i, j: (i, jnp.minimum(j, n_a - 1))),
                   pl.BlockSpec((tm, tn), lambda i, j: (i, jnp.maximum(j - n_a, 0))),
                   pl.BlockSpec((tm, w_ab.shape[1]), lambda i, j: (i, 0))],
        out_shape=[jax.ShapeDtypeStruct((n, c_a), BF16), jax.ShapeDtypeStruct((n, c - c_a), BF16),
                   jax.ShapeDtypeStruct((n, w_ab.shape[1]), F32)],
        scratch_shapes=[pltpu.VMEM((tm, d), BF16)],
        compiler_params=_params("arbitrary", "arbitrary"),
        name="in_proj",
    )(x, nw, w_cat, w_ab)


def _load_conv_window(x_ref, xbuf, ts):
    @pl.when(pl.program_id(1) == 0)
    def _():
        xbuf[0:SUBLANES, :] = jnp.zeros((SUBLANES, xbuf.shape[1]), F32)

    @pl.when(pl.program_id(1) != 0)
    def _():
        xbuf[0:SUBLANES, :] = xbuf[ts:ts + SUBLANES, :]

    xbuf[SUBLANES:SUBLANES + ts, :] = x_ref[...].astype(F32)


def _causal_conv(xbuf, cw_ref, ts, first_row=0, cols=slice(None)):
    base = first_row + SUBLANES - (CONV_TAPS - 1)
    acc = cw_ref[0:1, cols] * xbuf[base:base + ts, cols]
    for k in range(1, CONV_TAPS):
        acc = acc + cw_ref[k:k + 1, cols] * xbuf[base + k:base + k + ts, cols]
    return acc


def _rglru_kernel(rx_ref, rg_ref, cw_ref, cb_ref, wa_ref, ba_ref, wx_ref, bx_ref, lam_ref, o_ref,
                  xbuf, a_ref, b_ref, carry_ref):
    ts, c = rx_ref.shape
    _load_conv_window(rx_ref, xbuf, ts)

    @pl.when(pl.program_id(1) == 0)
    def _():
        carry_ref[...] = jnp.zeros_like(carry_ref)

    xc = _causal_conv(xbuf, cw_ref, ts) + cb_ref[...]
    neg_sp = -RG_POWER * _softplus(-lam_ref[...])
    for n in range(c // RNN_BLOCK):
        sl = slice(n * RNN_BLOCK, (n + 1) * RNN_BLOCK)
        xb = xc[:, sl]
        r = _sigmoid(_dot(xb, wa_ref[n]) + ba_ref[:, sl])
        i = _sigmoid(_dot(xb, wx_ref[n]) + bx_ref[:, sl])
        log_a = neg_sp[:, sl] * r
        a = jnp.exp(log_a)
        a_ref[:, sl] = a
        b_ref[:, sl] = jnp.sqrt(_one_minus_exp2(log_a, a)) * (i * xb)

    row = lax.broadcasted_iota(I32, (SUBLANES, c), 0)

    def slab(t, carry):
        rows = pl.ds(pl.multiple_of(t * SUBLANES, SUBLANES), SUBLANES)
        a = a_ref[rows, :]
        b = b_ref[rows, :]
        for d in (1, 2, 4):
            a_sh = jnp.where(row >= d, pltpu.roll(a, d, 0), 1.0)
            b_sh = jnp.where(row >= d, pltpu.roll(b, d, 0), 0.0)
            b = a * b_sh + b
            a = a * a_sh
        h = a * carry + b
        b_ref[rows, :] = h
        return h[SUBLANES - 1:SUBLANES, :]

    carry_ref[...] = lax.fori_loop(0, ts // SUBLANES, slab, carry_ref[...])
    o_ref[...] = (_gelu_tanh(rg_ref[...].astype(F32)) * b_ref[...]).astype(o_ref.dtype)


def _rglru(proj_a, bsz, seq, cw, cb, wa, ba, wx, bx, lam, ts):
    c = cw.shape[1]
    ns = seq // ts
    full = lambda shape: pl.BlockSpec(shape, lambda b, s: (0,) * len(shape))
    return pl.pallas_call(
        _rglru_kernel,
        grid=(bsz, ns),
        in_specs=[pl.BlockSpec((ts, c), lambda b, s: (b * ns + s, 0)),
                  pl.BlockSpec((ts, c), lambda b, s: (b * ns + s, 1)),
                  full(cw.shape), full(cb.shape), full(wa.shape), full(ba.shape),
                  full(wx.shape), full(bx.shape), full(lam.shape)],
        out_specs=pl.BlockSpec((ts, c), lambda b, s: (b * ns + s, 0)),
        out_shape=jax.ShapeDtypeStruct((bsz * seq, c), BF16),
        scratch_shapes=[pltpu.VMEM((ts + SUBLANES, c), F32), pltpu.VMEM((ts, c), F32),
                        pltpu.VMEM((ts, c), F32), pltpu.VMEM((1, c), F32)],
        compiler_params=_params("parallel", "arbitrary"),
        name="rglru",
    )(proj_a, proj_a, cw, cb, wa, ba, wx, bx, lam)


def _lane_bcast(x, lane):
    return jnp.broadcast_to(x[:, lane:lane + 1], x.shape)


def _unit_lower_inverses(neg_ls, eye, fillers):
    ts = eye.shape[0]
    run_filler = lambda: next(fillers, lambda: None)()
    ts_mats = [eye + n for n in neg_ls]
    ps = [_dot_split(_split(n), _split(n)) for n in neg_ls]
    run_filler()
    levels = int(math.log2(ts)) - 1
    for lvl in range(levels - 1):
        p_parts = [_split(p) for p in ps]
        lhs_parts = [tuple(jnp.concatenate([a, b], axis=0) for a, b in zip(_split(t), pp))
                     for t, pp in zip(ts_mats, p_parts)]
        both = [_dot_split(lp, pp) for lp, pp in zip(lhs_parts, p_parts)]
        ts_mats = [t + b[:ts] for t, b in zip(ts_mats, both)]
        ps = [b[ts:] for b in both]
        run_filler()
    return [t + _dot_split(_split(t), _split(p)) for t, p in zip(ts_mats, ps)]


def _gdn_kernel(qkv_ref, z_ref, ab_ref, abt_ref, cw_ref, acol_ref, dcol_ref, arow_ref, drow_ref, nw_ref,
                o_ref, xbuf, s_ref):
    ts = GDN_CHUNK
    n_heads = s_ref.shape[0]
    dn_w = n_heads * HEAD_DIM
    _load_conv_window(qkv_ref, xbuf, qkv_ref.shape[0])

    @pl.when(pl.program_id(1) == 0)
    def _():
        s_ref[...] = jnp.zeros_like(s_ref)

    row = lax.broadcasted_iota(I32, (ts, ts), 0)
    col = lax.broadcasted_iota(I32, (ts, ts), 1)
    lower_incl = (row >= col).astype(F32)
    upper_incl = (row <= col).astype(F32)
    eye = (row == col).astype(F32)
    head_cols = lambda h, part: slice(part * dn_w + h * HEAD_DIM, part * dn_w + (h + 1) * HEAD_DIM)
    heads = range(n_heads)

    n_chunks = qkv_ref.shape[0] // ts
    state = [dict(q=[None] * n_heads, k=[None] * n_heads, k_beta=[None] * n_heads, decay=[None] * n_heads,
                  rhs=[None] * n_heads, qd=[None] * n_heads, kd=[None] * n_heads) for _ in range(n_chunks)]

    def prepare_gates(c):
        st = state[c]
        rows = slice(c * ts, (c + 1) * ts)
        ab = ab_ref[rows, :]
        g_col = -acol_ref[...] * _softplus(ab + dcol_ref[...])
        st["beta"] = _sigmoid(ab)
        g_row = -arow_ref[...] * _softplus(abt_ref[:, rows] + drow_ref[...])
        st["cum_col"] = _dot_f32(lower_incl, g_col)
        st["cum_row"] = _dot_f32(g_row, upper_incl)
        last = st["cum_col"][ts - 1:ts, :]
        st["exp_cum"] = jnp.exp(st["cum_col"])
        st["exp_rem"] = jnp.exp(jnp.broadcast_to(last, (ts, LANES)) - st["cum_col"])
        st["exp_last"] = jnp.exp(last)

    def conv_silu(c, h, part):
        return _silu(_causal_conv(xbuf, cw_ref, ts, c * ts, head_cols(h, part)))

    def prepare_key(c, h):
        st = state[c]
        k = conv_silu(c, h, 1)
        k = k * lax.rsqrt(jnp.sum(k * k, axis=-1, keepdims=True) + NORM_EPS)
        st["k"][h], st["k_beta"][h] = k, k * _lane_bcast(st["beta"], n_heads + h)
        st["decay"][h] = jnp.exp(jnp.minimum(_lane_bcast(st["cum_col"], h) - st["cum_row"][h:h + 1, :], 0.0))

    def prepare_query_value(c, h):
        st = state[c]
        q, v = conv_silu(c, h, 0), conv_silu(c, h, 2)
        q = q * (lax.rsqrt(jnp.sum(q * q, axis=-1, keepdims=True) + NORM_EPS) * (HEAD_DIM ** -0.5))
        e_cum = _lane_bcast(st["exp_cum"], h)
        st["q"][h] = q
        st["rhs"][h] = jnp.concatenate([v * _lane_bcast(st["beta"], n_heads + h), st["k_beta"][h] * e_cum], axis=1)
        st["qd"][h] = q * e_cum
        st["kd"][h] = st["k"][h] * _lane_bcast(st["exp_rem"], h)

    def key_thunks(c):
        if c >= n_chunks:
            return []
        return [functools.partial(prepare_gates, c)] + [functools.partial(prepare_key, c, h) for h in heads]

    def solve(c):
        st = state[c]
        kks = [_dot_nt(st["k_beta"][h], st["k"][h]) for h in heads]
        neg_ls = [jnp.where(row > col, -(kks[h] * st["decay"][h]), 0.0) for h in heads]
        thunks = [functools.partial(prepare_query_value, c, h) for h in heads] + key_thunks(c + 1)
        slots = int(math.log2(ts))
        per_slot = -(-len(thunks) // slots)
        groups = [thunks[i:i + per_slot] for i in range(0, len(thunks), per_slot)]
        fillers = iter([functools.partial(lambda g: [t() for t in g], g) for g in groups])
        t_mats = _unit_lower_inverses(neg_ls, eye, fillers)
        for group in fillers:
            group()
        qks = [_dot_nt(st["q"][h], st["k"][h]) for h in heads]
        st["intra"] = [jnp.where(row >= col, qks[h] * st["decay"][h], 0.0) for h in heads]
        st["uw"] = [_dot(t_mats[h], st["rhs"][h]) for h in heads]

    def advance(c):
        st = state[c]
        rows = slice(c * ts, (c + 1) * ts)
        ws_qs = [_dot(jnp.concatenate([st["uw"][h][:, HEAD_DIM:], st["qd"][h]], axis=0), s_ref[h]) for h in heads]
        v_news = [st["uw"][h][:, :HEAD_DIM] - ws_qs[h][:ts] for h in heads]
        outs = [ws_qs[h][ts:] + _dot(st["intra"][h], v_news[h]) for h in heads]
        kvs = [_dot(st["kd"][h].T, v_news[h]) for h in heads]
        for h in heads:
            s_ref[h] = s_ref[h] * _lane_bcast(jnp.broadcast_to(st["exp_last"], (HEAD_DIM, LANES)), h) + kvs[h]
            o = _rmsnorm(outs[h], nw_ref[...]) * _silu(z_ref[rows, head_cols(h, 0)].astype(F32))
            o_ref[rows, head_cols(h, 0)] = o.astype(o_ref.dtype)

    for thunk in key_thunks(0):
        thunk()
    for c in range(n_chunks):
        solve(c)
        advance(c)


def _gdn(proj_b, ab, abt, bsz, seq, cw, acol, dcol, arow, drow, nw, n_heads):
    ts = GDN_CHUNK * GDN_CHUNKS_PER_STEP
    assert seq % ts == 0
    ns = seq // ts
    dn_w = n_heads * HEAD_DIM
    full = lambda shape: pl.BlockSpec(shape, lambda b, s: (0,) * len(shape))
    return pl.pallas_call(
        _gdn_kernel,
        grid=(bsz, ns),
        in_specs=[pl.BlockSpec((ts, 3 * dn_w), lambda b, s: (b * ns + s, 0)),
                  pl.BlockSpec((ts, dn_w), lambda b, s: (b * ns + s, 3)),
                  pl.BlockSpec((ts, LANES), lambda b, s: (b * ns + s, 0)),
                  pl.BlockSpec((2 * n_heads, ts), lambda b, s: (0, b * ns + s)),
                  full(cw.shape), full(acol.shape), full(dcol.shape), full(arow.shape), full(drow.shape),
                  full(nw.shape)],
        out_specs=pl.BlockSpec((ts, dn_w), lambda b, s: (b * ns + s, 0)),
        out_shape=jax.ShapeDtypeStruct((bsz * seq, dn_w), BF16),
        scratch_shapes=[pltpu.VMEM((ts + SUBLANES, 3 * dn_w), F32),
                        pltpu.VMEM((n_heads, HEAD_DIM, HEAD_DIM), F32)],
        compiler_params=_params("parallel", "arbitrary"),
        name="gdn",
    )(proj_b, proj_b, ab, abt, cw, acol, dcol, arow, drow, nw)


def _merge_kernel(x_ref, ya_ref, yb_ref, ga_ref, gb_ref, wa_ref, wb_ref, wo_ref, o_ref):
    y_a = jnp.dot(ya_ref[...], wa_ref[...], preferred_element_type=F32)
    y_b = jnp.dot(yb_ref[...], wb_ref[...], preferred_element_type=F32)
    m = _sigmoid(ga_ref[...].astype(F32)) * y_a + _sigmoid(gb_ref[...].astype(F32)) * y_b
    o_ref[...] = x_ref[...] + _dot(m, wo_ref[...])


def _merge(x, gated_a, gated_b, proj_b, w_a, w_b, w_o, tm):
    n, d = x.shape
    full = lambda shape: pl.BlockSpec(shape, lambda i: (0,) * len(shape))
    return pl.pallas_call(
        _merge_kernel,
        grid=(n // tm,),
        in_specs=[pl.BlockSpec((tm, d), lambda i: (i, 0)),
                  pl.BlockSpec((tm, gated_a.shape[1]), lambda i: (i, 0)),
                  pl.BlockSpec((tm, gated_b.shape[1]), lambda i: (i, 0)),
                  pl.BlockSpec((tm, d), lambda i: (i, 4)),
                  pl.BlockSpec((tm, d), lambda i: (i, 5)),
                  full(w_a.shape), full(w_b.shape), full(w_o.shape)],
        out_specs=pl.BlockSpec((tm, d), lambda i: (i, 0)),
        out_shape=jax.ShapeDtypeStruct((n, d), F32),
        compiler_params=_params("parallel"),
        name="merge",
    )(x, gated_a, gated_b, proj_b, proj_b, w_a, w_b, w_o)


def _cross_kernel(h_ref, nw_ref, wq_ref, kv_ref, wo_ref, o_ref):
    x = h_ref[...]
    d = x.shape[1]
    hd = d // CA_HEADS
    u = _rmsnorm(x, nw_ref[...])
    q = _dot(u, wq_ref[...])
    outs = []
    for h in range(CA_HEADS):
        k_h = kv_ref[:, h * hd:(h + 1) * hd]
        v_h = kv_ref[:, d + h * hd:d + (h + 1) * hd]
        s = _dot_nt(q[:, h * hd:(h + 1) * hd], k_h) * (hd ** -0.5)
        s = s - jnp.max(s, axis=-1, keepdims=True)
        e = jnp.exp(s)
        p = e / jnp.sum(e, axis=-1, keepdims=True)
        outs.append(_dot(p, v_h))
    o = jnp.concatenate(outs, axis=1)
    o_ref[...] = x + _dot(o, wo_ref[...])


def _cross(h, nw, w_q, kv, w_o, bsz, seq, n_mem, ts):
    n, d = h.shape
    ns = seq // ts
    full = lambda shape: pl.BlockSpec(shape, lambda b, s: (0,) * len(shape))
    return pl.pallas_call(
        _cross_kernel,
        grid=(bsz, ns),
        in_specs=[pl.BlockSpec((ts, d), lambda b, s: (b * ns + s, 0)),
                  full(nw.shape), full(w_q.shape),
                  pl.BlockSpec((n_mem, 2 * d), lambda b, s: (b, 0)),
                  full(w_o.shape)],
        out_specs=pl.BlockSpec((ts, d), lambda b, s: (b * ns + s, 0)),
        out_shape=jax.ShapeDtypeStruct((n, d), F32),
        compiler_params=_params("parallel", "parallel"),
        name="cross_attn",
    )(h, nw, w_q, kv, w_o)


def _router_kernel(h_ref, nw_ref, wr_ref, br_ref, u_ref, idx_ref, wt_ref, cnt_ref, base_ref):
    t = h_ref.shape[0]

    @pl.when(pl.program_id(0) == 0)
    def _():
        base_ref[...] = jnp.zeros_like(base_ref)

    u = _rmsnorm(h_ref[...], nw_ref[...])
    u_ref[...] = u
    logits = _dot_f32(u, wr_ref[...]) + br_ref[...]
    lane = lax.broadcasted_iota(I32, (t, LANES), 1)
    lanef = lane.astype(F32)
    big = float(LANES)
    neg = -jnp.inf

    lg = jnp.where((lane >= N_EXPERTS) & (lane < N_EXPERTS + N_GROUPS), logits, neg)
    gmax = jnp.max(lg, axis=-1, keepdims=True)
    grp = jnp.min(jnp.where(lg == gmax, lanef - float(N_EXPERTS), big), axis=-1, keepdims=True)
    p_grp = 1.0 / jnp.sum(jnp.exp(lg - gmax), axis=-1, keepdims=True)

    in_grp = (lane < N_EXPERTS) & ((lane // GROUP_SIZE).astype(F32) == grp)
    le = jnp.where(in_grp, logits, neg)
    m1 = jnp.max(le, axis=-1, keepdims=True)
    i1 = jnp.min(jnp.where(le == m1, lanef, big), axis=-1, keepdims=True)
    le2 = jnp.where(lanef == i1, neg, le)
    m2 = jnp.max(le2, axis=-1, keepdims=True)
    i2 = jnp.min(jnp.where(le2 == m2, lanef, big), axis=-1, keepdims=True)
    ratio = jnp.exp(m2 - m1)
    p1 = 1.0 / (1.0 + ratio)
    p2 = ratio * p1

    oh1 = lanef == i1
    oh2 = lanef == i2
    onehot = jnp.where(oh1 | oh2, 1.0, 0.0)
    r_i = lax.broadcasted_iota(I32, (t, t), 0)
    c_i = lax.broadcasted_iota(I32, (t, t), 1)
    before = jnp.where(r_i > c_i, 1.0, 0.0)
    rank = _dot(before, onehot) + base_ref[...]
    r1 = jnp.sum(jnp.where(oh1, rank, 0.0), axis=-1, keepdims=True)
    r2 = jnp.sum(jnp.where(oh2, rank, 0.0), axis=-1, keepdims=True)
    base_ref[...] = base_ref[...] + jnp.sum(onehot, axis=0, keepdims=True)

    idx = jnp.where(lane == 0, i1, jnp.where(lane == 1, i2, jnp.where(lane == 2, r1, jnp.where(lane == 3, r2, 0.0))))
    idx_ref[...] = idx.astype(I32)
    wt_ref[...] = jnp.where(lane == 0, p_grp * p1, jnp.where(lane == 1, p_grp * p2, 0.0))
    cnt_ref[...] = jnp.broadcast_to(base_ref[...], cnt_ref.shape)


def _router(h, nw, wr, br, t):
    n, d = h.shape
    full = lambda shape: pl.BlockSpec(shape, lambda i: (0,) * len(shape))
    return pl.pallas_call(
        _router_kernel,
        grid=(n // t,),
        in_specs=[pl.BlockSpec((t, d), lambda i: (i, 0)), full(nw.shape), full(wr.shape), full(br.shape)],
        out_specs=[pl.BlockSpec((t, d), lambda i: (i, 0)),
                   pl.BlockSpec((t, LANES), lambda i: (i, 0)),
                   pl.BlockSpec((t, LANES), lambda i: (i, 0)),
                   pl.BlockSpec((SUBLANES, LANES), lambda i: (0, 0))],
        out_shape=[jax.ShapeDtypeStruct((n, d), F32),
                   jax.ShapeDtypeStruct((n, LANES), I32),
                   jax.ShapeDtypeStruct((n, LANES), F32),
                   jax.ShapeDtypeStruct((SUBLANES, LANES), F32)],
        scratch_shapes=[pltpu.VMEM((1, LANES), F32)],
        compiler_params=_params("arbitrary"),
        name="router",
    )(h, nw, wr, br)


def _row_copy(src, src_row, dst, dst_row, sem):
    return pltpu.make_async_copy(src.at[pl.ds(src_row, 1)], dst.at[pl.ds(dst_row, 1)], sem)


def _dispatch_kernel(dest_ref, u_ref, xs_in_ref, xs_ref, sem):
    del xs_in_ref
    t = u_ref.shape[0]

    def start(r, carry):
        _row_copy(u_ref, r, xs_ref, dest_ref[0, 0, 2 * r], sem).start()
        _row_copy(u_ref, r, xs_ref, dest_ref[0, 0, 2 * r + 1], sem).start()
        return carry

    def wait(r, carry):
        _row_copy(u_ref, r, xs_ref, dest_ref[0, 0, 2 * r], sem).wait()
        _row_copy(u_ref, r, xs_ref, dest_ref[0, 0, 2 * r + 1], sem).wait()
        return carry

    lax.fori_loop(0, t, start, 0, unroll=DMA_UNROLL)
    lax.fori_loop(0, t, wait, 0, unroll=DMA_UNROLL)


def _dispatch(dest3, u, xs_zero, t):
    n, d = u.shape
    return pl.pallas_call(
        _dispatch_kernel,
        grid=(n // t,),
        in_specs=[pl.BlockSpec((1, 1, 2 * t), lambda i: (i, 0, 0), memory_space=pltpu.SMEM),
                  pl.BlockSpec((t, d), lambda i: (i, 0)),
                  pl.BlockSpec(memory_space=pl.ANY)],
        out_specs=pl.BlockSpec(memory_space=pl.ANY),
        out_shape=jax.ShapeDtypeStruct(xs_zero.shape, xs_zero.dtype),
        scratch_shapes=[pltpu.SemaphoreType.DMA(())],
        input_output_aliases={2: 0},
        compiler_params=_params("arbitrary"),
        name="moe_dispatch",
    )(dest3, u, xs_zero)


def _expert_kernel(be_ref, nu_ref, xs_ref, wg_ref, wu_ref, wd_ref, y_ref, wg_bf, wu_bf, wd_bf):
    j = pl.program_id(0)
    used = j < nu_ref[0]
    new_expert = (j == 0) | (be_ref[j] != be_ref[jnp.maximum(j - 1, 0)])

    @pl.when(used & new_expert)
    def _():
        wg_bf[...] = wg_ref[0].astype(BF16)
        wu_bf[...] = wu_ref[0].astype(BF16)
        wd_bf[...] = wd_ref[0].astype(BF16)

    @pl.when(used)
    def _():
        x = xs_ref[...].astype(BF16)
        hid = _silu(jnp.dot(x, wg_bf[...], preferred_element_type=F32)) * jnp.dot(
            x, wu_bf[...], preferred_element_type=F32)
        y_ref[...] = _dot(hid, wd_bf[...])

    @pl.when(jnp.logical_not(used))
    def _():
        y_ref[...] = jnp.zeros_like(y_ref)


def _experts(block_e, n_used, xs, w_gate, w_up, w_down):
    rows, d = xs.shape
    de = w_gate.shape[2]
    grid_spec = pltpu.PrefetchScalarGridSpec(
        num_scalar_prefetch=2,
        grid=(rows // MOE_BLOCK,),
        in_specs=[pl.BlockSpec((MOE_BLOCK, d), lambda j, be, nu: (j, 0)),
                  pl.BlockSpec((1, d, de), lambda j, be, nu: (be[j], 0, 0)),
                  pl.BlockSpec((1, d, de), lambda j, be, nu: (be[j], 0, 0)),
                  pl.BlockSpec((1, de, d), lambda j, be, nu: (be[j], 0, 0))],
        out_specs=pl.BlockSpec((MOE_BLOCK, d), lambda j, be, nu: (j, 0)),
        scratch_shapes=[pltpu.VMEM((d, de), BF16), pltpu.VMEM((d, de), BF16), pltpu.VMEM((de, d), BF16)],
    )
    return pl.pallas_call(
        _expert_kernel,
        grid_spec=grid_spec,
        out_shape=jax.ShapeDtypeStruct((rows, d), F32),
        compiler_params=_params("arbitrary"),
        name="moe_experts",
    )(block_e, n_used, xs, w_gate, w_up, w_down)


def _combine_kernel(dest_ref, h_ref, wt_ref, nw_ref, y_ref, o_ref, buf, sem, *, final_norm):
    t = h_ref.shape[0]

    def start(r, carry):
        _row_copy(y_ref, dest_ref[0, 0, 2 * r], buf.at[0], r, sem).start()
        _row_copy(y_ref, dest_ref[0, 0, 2 * r + 1], buf.at[1], r, sem).start()
        return carry

    def wait(r, carry):
        _row_copy(y_ref, dest_ref[0, 0, 2 * r], buf.at[0], r, sem).wait()
        _row_copy(y_ref, dest_ref[0, 0, 2 * r + 1], buf.at[1], r, sem).wait()
        return carry

    lax.fori_loop(0, t, start, 0, unroll=DMA_UNROLL)
    lax.fori_loop(0, t, wait, 0, unroll=DMA_UNROLL)
    wt = wt_ref[...]
    moe = buf[0] * wt[:, 0:1] + buf[1] * wt[:, 1:2]
    out = h_ref[...] + moe
    o_ref[...] = _rmsnorm(out, nw_ref[...]) if final_norm else out


def _combine(dest3, h, wt, nw, y, t, final_norm):
    n, d = h.shape
    return pl.pallas_call(
        functools.partial(_combine_kernel, final_norm=final_norm),
        grid=(n // t,),
        in_specs=[pl.BlockSpec((1, 1, 2 * t), lambda i: (i, 0, 0), memory_space=pltpu.SMEM),
                  pl.BlockSpec((t, d), lambda i: (i, 0)),
                  pl.BlockSpec((t, LANES), lambda i: (i, 0)),
                  pl.BlockSpec((1, d), lambda i: (0, 0)),
                  pl.BlockSpec(memory_space=pl.ANY)],
        out_specs=pl.BlockSpec((t, d), lambda i: (i, 0)),
        out_shape=jax.ShapeDtypeStruct((n, d), F32),
        scratch_shapes=[pltpu.VMEM((2, t, d), F32), pltpu.SemaphoreType.DMA(())],
        compiler_params=_params("arbitrary"),
        name="moe_combine",
    )(dest3, h, wt, nw, y)


def _tile(n, pref):
    return pref if n % pref == 0 else n


def kernel(x, mem, norm1_w, w_in, rnn_conv_w, rnn_conv_b, rglru_wa, rglru_ba, rglru_wx, rglru_bx, rglru_lambda, w_branch_a, dn_conv_w, dn_a_log, dn_dt_bias, dn_norm_w, w_branch_b, w_out, norm2_w, mem_norm_w, w_cq, w_ckv, w_co, norm3_w, w_router_group, b_router_group, w_router_expert, b_router_expert, w_exp_gate, w_exp_up, w_exp_down, norm_f_w):
    bsz, seq, d = x.shape
    n = bsz * seq
    n_mem = mem.shape[1]
    depth = w_in.shape[0]
    d_rnn = rnn_conv_w.shape[2]
    n_heads = dn_a_log.shape[1]
    dn_w = n_heads * HEAD_DIM
    row = lambda v: v.reshape(1, -1).astype(F32)

    h = x.reshape(n, d)
    mem2 = mem.reshape(bsz * n_mem, d)
    tm = _tile(n, 1024)
    for l in range(depth):
        o_rg, o_qkv, o_z = d_rnn, 2 * d_rnn, 2 * d_rnn + 3 * dn_w
        o_a = o_z + dn_w
        o_ga = o_a + 2 * n_heads
        wi = w_in[l]
        w_cat = jnp.concatenate([wi[:, :o_a], wi[:, o_ga:]], axis=1).astype(BF16)
        w_ab = jnp.pad(wi[:, o_a:o_ga], ((0, 0), (0, LANES - 2 * n_heads))).astype(BF16)
        proj_a, proj_b, ab = _in_proj(h, row(norm1_w[l]), w_cat, w_ab, o_qkv, tm, 512)

        gated_a = _rglru(proj_a, bsz, seq, rnn_conv_w[l], row(rnn_conv_b[l]), rglru_wa[l].astype(BF16),
                         row(rglru_ba[l]), rglru_wx[l].astype(BF16), row(rglru_bx[l]), row(rglru_lambda[l]),
                         _tile(seq, 512))

        a_dec = jnp.exp(dn_a_log[l].astype(F32))
        pad_h = lambda v: jnp.pad(v, (0, LANES - n_heads))
        acol, dcol = row(pad_h(a_dec)), row(pad_h(dn_dt_bias[l]))
        arow = jnp.broadcast_to(jnp.pad(a_dec, (0, n_heads))[:, None], (2 * n_heads, GDN_CHUNK))
        drow = jnp.broadcast_to(jnp.pad(dn_dt_bias[l], (0, n_heads))[:, None], (2 * n_heads, GDN_CHUNK))
        abt = ab[:, :2 * n_heads].T
        gated_b = _gdn(proj_b, ab, abt, bsz, seq, dn_conv_w[l], acol, dcol, arow, drow, row(dn_norm_w[l]), n_heads)

        h = _merge(h, gated_a, gated_b, proj_b, w_branch_a[l].astype(BF16), w_branch_b[l].astype(BF16),
                   w_out[l].astype(BF16), _tile(n, 512))

        kv = _norm_mm(mem2, row(mem_norm_w[l]), w_ckv[l].astype(BF16), BF16, _tile(bsz * n_mem, 1024), 512)
        h = _cross(h, row(norm2_w[l]), w_cq[l].astype(BF16), kv, w_co[l].astype(BF16), bsz, seq, n_mem,
                   _tile(seq, 512))

        w_r = jnp.pad(jnp.concatenate([w_router_expert[l], w_router_group[l]], axis=1),
                      ((0, 0), (0, LANES - N_EXPERTS - N_GROUPS)))
        b_r = row(jnp.pad(jnp.concatenate([b_router_expert[l], b_router_group[l]]), (0, LANES - N_EXPERTS - N_GROUPS)))
        u3, idx, wt, cnt = _router(h, row(norm3_w[l]), w_r, b_r, _tile(n, 512))

        counts = cnt[0, :N_EXPERTS].astype(I32)
        padded = (counts + MOE_BLOCK - 1) // MOE_BLOCK * MOE_BLOCK
        pend = jnp.cumsum(padded)
        pstart = pend - padded
        dest = pstart[idx[:, 0:2]] + idx[:, 2:4]
        n_blocks = (2 * n + N_EXPERTS * (MOE_BLOCK - 1)) // MOE_BLOCK
        block_row = jnp.arange(n_blocks, dtype=I32) * MOE_BLOCK
        block_e = jnp.minimum(jnp.sum((pend[None, :] <= block_row[:, None]).astype(I32), axis=1), N_EXPERTS - 1)
        n_used = (pend[-1:] // MOE_BLOCK).astype(I32)
        t_moe = _tile(n, 256)
        dest3 = dest.reshape(n // t_moe, 1, 2 * t_moe)
        xs = _dispatch(dest3, u3, jnp.zeros((n_blocks * MOE_BLOCK, d), F32), t_moe)
        yb = _experts(block_e, n_used, xs, w_exp_gate[l], w_exp_up[l], w_exp_down[l])
        h = _combine(dest3, h, wt, row(norm_f_w), yb, t_moe, final_norm=(l == depth - 1))
    return h.reshape(bsz, seq, d)
```

```python
import functools
import math

import jax
import jax.numpy as jnp
from jax import lax
from jax.experimental import pallas as pl
from jax.experimental.pallas import tpu as pltpu

F32 = jnp.float32
BF16 = jnp.bfloat16
I32 = jnp.int32
HIGHEST = lax.Precision.HIGHEST

NORM_EPS = 1e-6
CONV_TAPS = 4
RNN_BLOCK = 128
RG_POWER = 8.0
HEAD_DIM = 128
GDN_CHUNK = 128
GDN_CHUNKS_PER_STEP = 2
CA_HEADS = 4
N_GROUPS = 8
GROUP_SIZE = 8
N_EXPERTS = N_GROUPS * GROUP_SIZE
MOE_BLOCK = 256
LANES = 128
SUBLANES = 8
DMA_UNROLL = 8
VMEM_LIMIT = 48 * 1024 * 1024


def _params(*semantics):
    return pltpu.CompilerParams(dimension_semantics=semantics, vmem_limit_bytes=VMEM_LIMIT)


def _dot(a, b):
    return jnp.dot(a.astype(BF16), b.astype(BF16), preferred_element_type=F32)


def _dot_nt(a, b):
    return lax.dot_general(a.astype(BF16), b.astype(BF16), (((1,), (1,)), ((), ())),
                           preferred_element_type=F32)


def _dot_tn(a, b):
    return lax.dot_general(a.astype(BF16), b.astype(BF16), (((0,), (0,)), ((), ())),
                           preferred_element_type=F32)


def _dot_f32(a, b):
    return jnp.dot(a, b, precision=HIGHEST, preferred_element_type=F32)


def _split(x):
    hi = x.astype(BF16)
    return hi, (x - hi.astype(F32)).astype(BF16)


def _dot_split(a_parts, b_parts):
    (a_hi, a_lo), (b_hi, b_lo) = a_parts, b_parts
    dot = functools.partial(jnp.dot, preferred_element_type=F32)
    return dot(a_hi, b_hi) + (dot(a_hi, b_lo) + dot(a_lo, b_hi))


def _rmsnorm(x, w):
    return x * lax.rsqrt(jnp.mean(x * x, axis=-1, keepdims=True) + NORM_EPS) * w


def _sigmoid(x):
    return 0.5 * jnp.tanh(0.5 * x) + 0.5


def _silu(x):
    return x * _sigmoid(x)


def _softplus(x):
    return jnp.maximum(x, 0.0) + jnp.log(1.0 + jnp.exp(-jnp.abs(x)))


def _one_minus_exp2(y, exp_y):
    return jnp.tanh(-y) * (1.0 + exp_y * exp_y)


def _gelu_tanh(x):
    return 0.5 * x * (1.0 + jnp.tanh(math.sqrt(2.0 / math.pi) * (x + 0.044715 * (x * x * x))))


def _norm_mm_kernel(x_ref, nw_ref, w_ref, o_ref, u_ref):
    @pl.when(pl.program_id(1) == 0)
    def _():
        u_ref[...] = _rmsnorm(x_ref[...], nw_ref[...]).astype(BF16)

    o_ref[...] = jnp.dot(u_ref[...], w_ref[...], preferred_element_type=F32).astype(o_ref.dtype)


def _norm_mm(x, nw, w, out_dtype, tm, tn):
    n, d = x.shape
    c = w.shape[1]
    return pl.pallas_call(
        _norm_mm_kernel,
        grid=(n // tm, c // tn),
        in_specs=[pl.BlockSpec((tm, d), lambda i, j: (i, 0)),
                  pl.BlockSpec((1, d), lambda i, j: (0, 0)),
                  pl.BlockSpec((d, tn), lambda i, j: (0, j))],
        out_specs=pl.BlockSpec((tm, tn), lambda i, j: (i, j)),
        out_shape=jax.ShapeDtypeStruct((n, c), out_dtype),
        scratch_shapes=[pltpu.VMEM((tm, d), BF16)],
        compiler_params=_params("parallel", "arbitrary"),
        name="norm_mm",
    )(x, nw, w)


def _in_proj_kernel(x_ref, nw_ref, w_ref, wab_ref, oa_ref, ob_ref, ab_ref, u_ref, *, n_a):
    j = pl.program_id(1)

    @pl.when(j == 0)
    def _():
        u_ref[...] = _rmsnorm(x_ref[...], nw_ref[...]).astype(BF16)
        ab_ref[...] = jnp.dot(u_ref[...], wab_ref[...], preferred_element_type=F32)

    @pl.when(j < n_a)
    def _():
        oa_ref[...] = jnp.dot(u_ref[...], w_ref[...], preferred_element_type=F32).astype(oa_ref.dtype)

    @pl.when(j >= n_a)
    def _():
        ob_ref[...] = jnp.dot(u_ref[...], w_ref[...], preferred_element_type=F32).astype(ob_ref.dtype)


def _in_proj(x, nw, w_cat, w_ab, c_a, tm, tn):
    n, d = x.shape
    c = w_cat.shape[1]
    n_a, n_b = c_a // tn, (c - c_a) // tn
    return pl.pallas_call(
        functools.partial(_in_proj_kernel, n_a=n_a),
        grid=(n // tm, n_a + n_b),
        in_specs=[pl.BlockSpec((tm, d), lambda i, j: (i, 0)),
                  pl.BlockSpec((1, d), lambda i, j: (0, 0)),
                  pl.BlockSpec((d, tn), lambda i, j: (0, j)),
                  pl.BlockSpec(w_ab.shape, lambda i, j: (0, 0))],
        out_specs=[pl.BlockSpec((tm, tn), lambda i, j: (i, jnp.minimum(j, n_a - 1))),
                   pl.BlockSpec((tm, tn), lambda i, j: (i, jnp.maximum(j - n_a, 0))),
                   pl.BlockSpec((tm, w_ab.shape[1]), lambda i, j: (i, 0))],
        out_shape=[jax.ShapeDtypeStruct((n, c_a), BF16), jax.ShapeDtypeStruct((n, c - c_a), BF16),
                   jax.ShapeDtypeStruct((n, w_ab.shape[1]), F32)],
        scratch_shapes=[pltpu.VMEM((tm, d), BF16)],
        compiler_params=_params("arbitrary", "arbitrary"),
        name="in_proj",
    )(x, nw, w_cat, w_ab)


def _load_conv_window(x_ref, xbuf, ts):
    @pl.when(pl.program_id(1) == 0)
    def _():
        xbuf[0:SUBLANES, :] = jnp.zeros((SUBLANES, xbuf.shape[1]), F32)

    @pl.when(pl.program_id(1) != 0)
    def _():
        xbuf[0:SUBLANES, :] = xbuf[ts:ts + SUBLANES, :]

    xbuf[SUBLANES:SUBLANES + ts, :] = x_ref[...].astype(F32)


def _causal_conv(xbuf, cw_ref, ts, first_row=0, cols=slice(None)):
    base = first_row + SUBLANES - (CONV_TAPS - 1)
    acc = cw_ref[0:1, cols] * xbuf[base:base + ts, cols]
    for k in range(1, CONV_TAPS):
        acc = acc + cw_ref[k:k + 1, cols] * xbuf[base + k:base + k + ts, cols]
    return acc


def _rglru_kernel(rx_ref, rg_ref, cw_ref, cb_ref, wa_ref, ba_ref, wx_ref, bx_ref, lam_ref, o_ref,
                  xbuf, a_ref, b_ref, carry_ref):
    ts, c = rx_ref.shape
    _load_conv_window(rx_ref, xbuf, ts)

    @pl.when(pl.program_id(1) == 0)
    def _():
        carry_ref[...] = jnp.zeros_like(carry_ref)

    xc = _causal_conv(xbuf, cw_ref, ts) + cb_ref[...]
    neg_sp = -RG_POWER * _softplus(-lam_ref[...])
    for n in range(c // RNN_BLOCK):
        sl = slice(n * RNN_BLOCK, (n + 1) * RNN_BLOCK)
        xb = xc[:, sl]
        r = _sigmoid(_dot(xb, wa_ref[n]) + ba_ref[:, sl])
        i = _sigmoid(_dot(xb, wx_ref[n]) + bx_ref[:, sl])
        log_a = neg_sp[:, sl] * r
        a = jnp.exp(log_a)
        a_ref[:, sl] = a
        b_ref[:, sl] = jnp.sqrt(_one_minus_exp2(log_a, a)) * (i * xb)

    row = lax.broadcasted_iota(I32, (SUBLANES, c), 0)

    def slab(t, carry):
        rows = pl.ds(pl.multiple_of(t * SUBLANES, SUBLANES), SUBLANES)
        a = a_ref[rows, :]
        b = b_ref[rows, :]
        for d in (1, 2, 4):
            a_sh = jnp.where(row >= d, pltpu.roll(a, d, 0), 1.0)
            b_sh = jnp.where(row >= d, pltpu.roll(b, d, 0), 0.0)
            b = a * b_sh + b
            a = a * a_sh
        h = a * carry + b
        b_ref[rows, :] = h
        return h[SUBLANES - 1:SUBLANES, :]

    carry_ref[...] = lax.fori_loop(0, ts // SUBLANES, slab, carry_ref[...])
    o_ref[...] = (_gelu_tanh(rg_ref[...].astype(F32)) * b_ref[...]).astype(o_ref.dtype)


def _rglru(proj_a, bsz, seq, cw, cb, wa, ba, wx, bx, lam, ts):
    c = cw.shape[1]
    ns = seq // ts
    full = lambda shape: pl.BlockSpec(shape, lambda b, s: (0,) * len(shape))
    return pl.pallas_call(
        _rglru_kernel,
        grid=(bsz, ns),
        in_specs=[pl.BlockSpec((ts, c), lambda b, s: (b * ns + s, 0)),
                  pl.BlockSpec((ts, c), lambda b, s: (b * ns + s, 1)),
                  full(cw.shape), full(cb.shape), full(wa.shape), full(ba.shape),
                  full(wx.shape), full(bx.shape), full(lam.shape)],
        out_specs=pl.BlockSpec((ts, c), lambda b, s: (b * ns + s, 0)),
        out_shape=jax.ShapeDtypeStruct((bsz * seq, c), BF16),
        scratch_shapes=[pltpu.VMEM((ts + SUBLANES, c), F32), pltpu.VMEM((ts, c), F32),
                        pltpu.VMEM((ts, c), F32), pltpu.VMEM((1, c), F32)],
        compiler_params=_params("parallel", "arbitrary"),
        name="rglru",
    )(proj_a, proj_a, cw, cb, wa, ba, wx, bx, lam)


def _lane_bcast(x, lane):
    return jnp.broadcast_to(x[:, lane:lane + 1], x.shape)


def _unit_lower_inverses(neg_ls, eye, fillers):
    ts = eye.shape[0]
    run_filler = lambda: next(fillers, lambda: None)()
    ts_mats = [eye + n for n in neg_ls]
    ps = [_dot_split(_split(n), _split(n)) for n in neg_ls]
    run_filler()
    levels = int(math.log2(ts)) - 1
    for lvl in range(levels - 1):
        p_parts = [_split(p) for p in ps]
        lhs_parts = [tuple(jnp.concatenate([a, b], axis=0) for a, b in zip(_split(t), pp))
                     for t, pp in zip(ts_mats, p_parts)]
        both = [_dot_split(lp, pp) for lp, pp in zip(lhs_parts, p_parts)]
        ts_mats = [t + b[:ts] for t, b in zip(ts_mats, both)]
        ps = [b[ts:] for b in both]
        run_filler()
    return [t + _dot_split(_split(t), _split(p)) for t, p in zip(ts_mats, ps)]


def _gdn_kernel(qkv_ref, z_ref, ab_ref, abt_ref, cw_ref, acol_ref, dcol_ref, arow_ref, drow_ref, nw_ref,
                o_ref, xbuf, s_ref):
    ts = GDN_CHUNK
    n_heads = s_ref.shape[0]
    dn_w = n_heads * HEAD_DIM
    _load_conv_window(qkv_ref, xbuf, qkv_ref.shape[0])

    @pl.when(pl.program_id(1) == 0)
    def _():
        s_ref[...] = jnp.zeros_like(s_ref)

    row = lax.broadcasted_iota(I32, (ts, ts), 0)
    col = lax.broadcasted_iota(I32, (ts, ts), 1)
    lower_incl = (row >= col).astype(F32)
    upper_incl = (row <= col).astype(F32)
    eye = (row == col).astype(F32)
    head_cols = lambda h, part: slice(part * dn_w + h * HEAD_DIM, part * dn_w + (h + 1) * HEAD_DIM)
    heads = range(n_heads)

    n_chunks = qkv_ref.shape[0] // ts
    state = [dict(q=[None] * n_heads, k=[None] * n_heads, k_beta=[None] * n_heads, decay=[None] * n_heads,
                  rhs=[None] * n_heads, qd=[None] * n_heads, kd=[None] * n_heads) for _ in range(n_chunks)]

    def prepare_gates(c):
        st = state[c]
        rows = slice(c * ts, (c + 1) * ts)
        ab = ab_ref[rows, :]
        g_col = -acol_ref[...] * _softplus(ab + dcol_ref[...])
        st["beta"] = _sigmoid(ab)
        g_row = -arow_ref[...] * _softplus(abt_ref[:, rows] + drow_ref[...])
        st["cum_col"] = _dot_f32(lower_incl, g_col)
        st["cum_row"] = _dot_f32(g_row, upper_incl)
        last = st["cum_col"][ts - 1:ts, :]
        st["exp_cum"] = jnp.exp(st["cum_col"])
        st["exp_rem"] = jnp.exp(jnp.broadcast_to(last, (ts, LANES)) - st["cum_col"])
        st["exp_last"] = jnp.exp(last)

    def conv_silu(c, h, part):
        return _silu(_causal_conv(xbuf, cw_ref, ts, c * ts, head_cols(h, part)))

    def prepare_key(c, h):
        st = state[c]
        k = conv_silu(c, h, 1)
        k = k * lax.rsqrt(jnp.sum(k * k, axis=-1, keepdims=True) + NORM_EPS)
        st["k"][h], st["k_beta"][h] = k, k * _lane_bcast(st["beta"], n_heads + h)
        st["decay"][h] = jnp.exp(jnp.minimum(_lane_bcast(st["cum_col"], h) - st["cum_row"][h:h + 1, :], 0.0))

    def prepare_query_value(c, h):
        st = state[c]
        q, v = conv_silu(c, h, 0), conv_silu(c, h, 2)
        q = q * (lax.rsqrt(jnp.sum(q * q, axis=-1, keepdims=True) + NORM_EPS) * (HEAD_DIM ** -0.5))
        e_cum = _lane_bcast(st["exp_cum"], h)
        st["q"][h] = q
        st["rhs"][h] = jnp.concatenate([v * _lane_bcast(st["beta"], n_heads + h), st["k_beta"][h] * e_cum], axis=1)
        st["qd"][h] = q * e_cum
        st["kd"][h] = st["k"][h] * _lane_bcast(st["exp_rem"], h)

    def key_thunks(c):
        if c >= n_chunks:
            return []
        return [functools.partial(prepare_gates, c)] + [functools.partial(prepare_key, c, h) for h in heads]

    def solve(c):
        st = state[c]
        kks = [_dot_nt(st["k_beta"][h], st["k"][h]) for h in heads]
        neg_ls = [jnp.where(row > col, -(kks[h] * st["decay"][h]), 0.0) for h in heads]
        thunks = [functools.partial(prepare_query_value, c, h) for h in heads] + key_thunks(c + 1)
        slots = int(math.log2(ts))
        per_slot = -(-len(thunks) // slots)
        groups = [thunks[i:i + per_slot] for i in range(0, len(thunks), per_slot)]
        fillers = iter([functools.partial(lambda g: [t() for t in g], g) for g in groups])
        t_mats = _unit_lower_inverses(neg_ls, eye, fillers)
        for group in fillers:
            group()
        qks = [_dot_nt(st["q"][h], st["k"][h]) for h in heads]
        st["intra"] = [jnp.where(row >= col, qks[h] * st["decay"][h], 0.0) for h in heads]
        st["uw"] = [_dot(t_mats[h], st["rhs"][h]) for h in heads]

    def advance(c):
        st = state[c]
        rows = slice(c * ts, (c + 1) * ts)
        ws_qs = [_dot(jnp.concatenate([st["uw"][h][:, HEAD_DIM:], st["qd"][h]], axis=0), s_ref[h]) for h in heads]
        v_news = [st["uw"][h][:, :HEAD_DIM] - ws_qs[h][:ts] for h in heads]
        outs = [ws_qs[h][ts:] + _dot(st["intra"][h], v_news[h]) for h in heads]
        kvs = [_dot(st["kd"][h].T, v_news[h]) for h in heads]
        for h in heads:
            s_ref[h] = s_ref[h] * _lane_bcast(jnp.broadcast_to(st["exp_last"], (HEAD_DIM, LANES)), h) + kvs[h]
            o = _rmsnorm(outs[h], nw_ref[...]) * _silu(z_ref[rows, head_cols(h, 0)].astype(F32))
            o_ref[rows, head_cols(h, 0)] = o.astype(o_ref.dtype)

    for thunk in key_thunks(0):
        thunk()
    for c in range(n_chunks):
        solve(c)
        advance(c)


def _gdn(proj_b, ab, abt, bsz, seq, cw, acol, dcol, arow, drow, nw, n_heads):
    ts = GDN_CHUNK * GDN_CHUNKS_PER_STEP
    assert seq % ts == 0
    ns = seq // ts
    dn_w = n_heads * HEAD_DIM
    full = lambda shape: pl.BlockSpec(shape, lambda b, s: (0,) * len(shape))
    return pl.pallas_call(
        _gdn_kernel,
        grid=(bsz, ns),
        in_specs=[pl.BlockSpec((ts, 3 * dn_w), lambda b, s: (b * ns + s, 0)),
                  pl.BlockSpec((ts, dn_w), lambda b, s: (b * ns + s, 3)),
                  pl.BlockSpec((ts, LANES), lambda b, s: (b * ns + s, 0)),
                  pl.BlockSpec((2 * n_heads, ts), lambda b, s: (0, b * ns + s)),
                  full(cw.shape), full(acol.shape), full(dcol.shape), full(arow.shape), full(drow.shape),
                  full(nw.shape)],
        out_specs=pl.BlockSpec((ts, dn_w), lambda b, s: (b * ns + s, 0)),
        out_shape=jax.ShapeDtypeStruct((bsz * seq, dn_w), BF16),
        scratch_shapes=[pltpu.VMEM((ts + SUBLANES, 3 * dn_w), F32),
                        pltpu.VMEM((n_heads, HEAD_DIM, HEAD_DIM), F32)],
        compiler_params=_params("parallel", "arbitrary"),
        name="gdn",
    )(proj_b, proj_b, ab, abt, cw, acol, dcol, arow, drow, nw)


def _merge_kernel(x_ref, ya_ref, yb_ref, ga_ref, gb_ref, wa_ref, wb_ref, wo_ref, o_ref):
    y_a = jnp.dot(ya_ref[...], wa_ref[...], preferred_element_type=F32)
    y_b = jnp.dot(yb_ref[...], wb_ref[...], preferred_element_type=F32)
    m = _sigmoid(ga_ref[...].astype(F32)) * y_a + _sigmoid(gb_ref[...].astype(F32)) * y_b
    o_ref[...] = x_ref[...] + _dot(m, wo_ref[...])


def _merge(x, gated_a, gated_b, proj_b, w_a, w_b, w_o, tm):
    n, d = x.shape
    full = lambda shape: pl.BlockSpec(shape, lambda i: (0,) * len(shape))
    return pl.pallas_call(
        _merge_kernel,
        grid=(n // tm,),
        in_specs=[pl.BlockSpec((tm, d), lambda i: (i, 0)),
                  pl.BlockSpec((tm, gated_a.shape[1]), lambda i: (i, 0)),
                  pl.BlockSpec((tm, gated_b.shape[1]), lambda i: (i, 0)),
                  pl.BlockSpec((tm, d), lambda i: (i, 4)),
                  pl.BlockSpec((tm, d), lambda i: (i, 5)),
                  full(w_a.shape), full(w_b.shape), full(w_o.shape)],
        out_specs=pl.BlockSpec((tm, d), lambda i: (i, 0)),
        out_shape=jax.ShapeDtypeStruct((n, d), F32),
        compiler_params=_params("parallel"),
        name="merge",
    )(x, gated_a, gated_b, proj_b, proj_b, w_a, w_b, w_o)


def _cross_kernel(h_ref, nw_ref, wq_ref, kv_ref, wo_ref, o_ref):
    x = h_ref[...]
    d = x.shape[1]
    hd = d // CA_HEADS
    u = _rmsnorm(x, nw_ref[...])
    q = _dot(u, wq_ref[...])
    outs = []
    for h in range(CA_HEADS):
        k_h = kv_ref[:, h * hd:(h + 1) * hd]
        v_h = kv_ref[:, d + h * hd:d + (h + 1) * hd]
        s = _dot_nt(q[:, h * hd:(h + 1) * hd], k_h) * (hd ** -0.5)
        s = s - jnp.max(s, axis=-1, keepdims=True)
        e = jnp.exp(s)
        p = e / jnp.sum(e, axis=-1, keepdims=True)
        outs.append(_dot(p, v_h))
    o = jnp.concatenate(outs, axis=1)
    o_ref[...] = x + _dot(o, wo_ref[...])


def _cross(h, nw, w_q, kv, w_o, bsz, seq, n_mem, ts):
    n, d = h.shape
    ns = seq // ts
    full = lambda shape: pl.BlockSpec(shape, lambda b, s: (0,) * len(shape))
    return pl.pallas_call(
        _cross_kernel,
        grid=(bsz, ns),
        in_specs=[pl.BlockSpec((ts, d), lambda b, s: (b * ns + s, 0)),
                  full(nw.shape), full(w_q.shape),
                  pl.BlockSpec((n_mem, 2 * d), lambda b, s: (b, 0)),
                  full(w_o.shape)],
        out_specs=pl.BlockSpec((ts, d), lambda b, s: (b * ns + s, 0)),
        out_shape=jax.ShapeDtypeStruct((n, d), F32),
        compiler_params=_params("parallel", "parallel"),
        name="cross_attn",
    )(h, nw, w_q, kv, w_o)


def _router_kernel(h_ref, nw_ref, wr_ref, br_ref, u_ref, idx_ref, wt_ref, cnt_ref, base_ref):
    t = h_ref.shape[0]

    @pl.when(pl.program_id(0) == 0)
    def _():
        base_ref[...] = jnp.zeros_like(base_ref)

    u = _rmsnorm(h_ref[...], nw_ref[...])
    u_ref[...] = u
    logits = _dot_split(_split(u), _split(wr_ref[...])) + br_ref[...]
    lane = lax.broadcasted_iota(I32, (t, LANES), 1)
    lanef = lane.astype(F32)
    big = float(LANES)
    neg = -jnp.inf

    lg = jnp.where((lane >= N_EXPERTS) & (lane < N_EXPERTS + N_GROUPS), logits, neg)
    gmax = jnp.max(lg, axis=-1, keepdims=True)
    grp = jnp.min(jnp.where(lg == gmax, lanef - float(N_EXPERTS), big), axis=-1, keepdims=True)
    p_grp = 1.0 / jnp.sum(jnp.exp(lg - gmax), axis=-1, keepdims=True)

    in_grp = (lane < N_EXPERTS) & ((lane // GROUP_SIZE).astype(F32) == grp)
    le = jnp.where(in_grp, logits, neg)
    m1 = jnp.max(le, axis=-1, keepdims=True)
    i1 = jnp.min(jnp.where(le == m1, lanef, big), axis=-1, keepdims=True)
    le2 = jnp.where(lanef == i1, neg, le)
    m2 = jnp.max(le2, axis=-1, keepdims=True)
    i2 = jnp.min(jnp.where(le2 == m2, lanef, big), axis=-1, keepdims=True)
    ratio = jnp.exp(m2 - m1)
    p1 = 1.0 / (1.0 + ratio)
    p2 = ratio * p1

    oh1 = lanef == i1
    oh2 = lanef == i2
    onehot = jnp.where(oh1 | oh2, 1.0, 0.0)
    r_i = lax.broadcasted_iota(I32, (t, t), 0)
    c_i = lax.broadcasted_iota(I32, (t, t), 1)
    before = jnp.where(r_i > c_i, 1.0, 0.0)
    rank = _dot(before, onehot) + base_ref[...]
    r1 = jnp.sum(jnp.where(oh1, rank, 0.0), axis=-1, keepdims=True)
    r2 = jnp.sum(jnp.where(oh2, rank, 0.0), axis=-1, keepdims=True)
    base_ref[...] = base_ref[...] + jnp.sum(onehot, axis=0, keepdims=True)

    idx = jnp.where(lane == 0, i1, jnp.where(lane == 1, i2, jnp.where(lane == 2, r1, jnp.where(lane == 3, r2, 0.0))))
    idx_ref[...] = idx.astype(I32)
    wt_ref[...] = jnp.where(lane == 0, p_grp * p1, jnp.where(lane == 1, p_grp * p2, 0.0))
    cnt_ref[...] = jnp.broadcast_to(base_ref[...], cnt_ref.shape)


def _router(h, nw, wr, br, t):
    n, d = h.shape
    full = lambda shape: pl.BlockSpec(shape, lambda i: (0,) * len(shape))
    return pl.pallas_call(
        _router_kernel,
        grid=(n // t,),
        in_specs=[pl.BlockSpec((t, d), lambda i: (i, 0)), full(nw.shape), full(wr.shape), full(br.shape)],
        out_specs=[pl.BlockSpec((t, d), lambda i: (i, 0)),
                   pl.BlockSpec((t, LANES), lambda i: (i, 0)),
                   pl.BlockSpec((t, LANES), lambda i: (i, 0)),
                   pl.BlockSpec((SUBLANES, LANES), lambda i: (0, 0))],
        out_shape=[jax.ShapeDtypeStruct((n, d), F32),
                   jax.ShapeDtypeStruct((n, LANES), I32),
                   jax.ShapeDtypeStruct((n, LANES), F32),
                   jax.ShapeDtypeStruct((SUBLANES, LANES), F32)],
        scratch_shapes=[pltpu.VMEM((1, LANES), F32)],
        compiler_params=_params("arbitrary"),
        name="router",
    )(h, nw, wr, br)


def _row_copy(src, src_row, dst, dst_row, sem):
    return pltpu.make_async_copy(src.at[pl.ds(src_row, 1)], dst.at[pl.ds(dst_row, 1)], sem)


def _dispatch_kernel(dest_ref, u_ref, xs_in_ref, xs_ref, sem):
    del xs_in_ref
    t = u_ref.shape[0]

    def start(r, carry):
        _row_copy(u_ref, r, xs_ref, dest_ref[0, 0, 2 * r], sem).start()
        _row_copy(u_ref, r, xs_ref, dest_ref[0, 0, 2 * r + 1], sem).start()
        return carry

    def wait(r, carry):
        _row_copy(u_ref, r, xs_ref, dest_ref[0, 0, 2 * r], sem).wait()
        _row_copy(u_ref, r, xs_ref, dest_ref[0, 0, 2 * r + 1], sem).wait()
        return carry

    lax.fori_loop(0, t, start, 0, unroll=DMA_UNROLL)
    lax.fori_loop(0, t, wait, 0, unroll=DMA_UNROLL)


def _dispatch(dest3, u, xs_zero, t):
    n, d = u.shape
    return pl.pallas_call(
        _dispatch_kernel,
        grid=(n // t,),
        in_specs=[pl.BlockSpec((1, 1, 2 * t), lambda i: (i, 0, 0), memory_space=pltpu.SMEM),
                  pl.BlockSpec((t, d), lambda i: (i, 0)),
                  pl.BlockSpec(memory_space=pl.ANY)],
        out_specs=pl.BlockSpec(memory_space=pl.ANY),
        out_shape=jax.ShapeDtypeStruct(xs_zero.shape, xs_zero.dtype),
        scratch_shapes=[pltpu.SemaphoreType.DMA(())],
        input_output_aliases={2: 0},
        compiler_params=_params("arbitrary"),
        name="moe_dispatch",
    )(dest3, u, xs_zero)


def _expert_kernel(be_ref, nu_ref, xs_ref, wg_ref, wu_ref, wd_ref, y_ref, wg_bf, wu_bf, wd_bf):
    j = pl.program_id(0)
    used = j < nu_ref[0]
    new_expert = (j == 0) | (be_ref[j] != be_ref[jnp.maximum(j - 1, 0)])

    @pl.when(used & new_expert)
    def _():
        wg_bf[...] = wg_ref[0].astype(BF16)
        wu_bf[...] = wu_ref[0].astype(BF16)
        wd_bf[...] = wd_ref[0].astype(BF16)

    @pl.when(used)
    def _():
        x = xs_ref[...].astype(BF16)
        hid = _silu(jnp.dot(x, wg_bf[...], preferred_element_type=F32)) * jnp.dot(
            x, wu_bf[...], preferred_element_type=F32)
        y_ref[...] = _dot(hid, wd_bf[...])

    @pl.when(jnp.logical_not(used))
    def _():
        y_ref[...] = jnp.zeros_like(y_ref)


def _experts(block_e, n_used, xs, w_gate, w_up, w_down):
    rows, d = xs.shape
    de = w_gate.shape[2]
    grid_spec = pltpu.PrefetchScalarGridSpec(
        num_scalar_prefetch=2,
        grid=(rows // MOE_BLOCK,),
        in_specs=[pl.BlockSpec((MOE_BLOCK, d), lambda j, be, nu: (j, 0)),
                  pl.BlockSpec((1, d, de), lambda j, be, nu: (be[j], 0, 0)),
                  pl.BlockSpec((1, d, de), lambda j, be, nu: (be[j], 0, 0)),
                  pl.BlockSpec((1, de, d), lambda j, be, nu: (be[j], 0, 0))],
        out_specs=pl.BlockSpec((MOE_BLOCK, d), lambda j, be, nu: (j, 0)),
        scratch_shapes=[pltpu.VMEM((d, de), BF16), pltpu.VMEM((d, de), BF16), pltpu.VMEM((de, d), BF16)],
    )
    return pl.pallas_call(
        _expert_kernel,
        grid_spec=grid_spec,
        out_shape=jax.ShapeDtypeStruct((rows, d), F32),
        compiler_params=_params("arbitrary"),
        name="moe_experts",
    )(block_e, n_used, xs, w_gate, w_up, w_down)


def _combine_kernel(dest_ref, h_ref, wt_ref, nw_ref, y_ref, o_ref, buf, sem, *, final_norm):
    t = h_ref.shape[0]

    def start(r, carry):
        _row_copy(y_ref, dest_ref[0, 0, 2 * r], buf.at[0], r, sem).start()
        _row_copy(y_ref, dest_ref[0, 0, 2 * r + 1], buf.at[1], r, sem).start()
        return carry

    def wait(r, carry):
        _row_copy(y_ref, dest_ref[0, 0, 2 * r], buf.at[0], r, sem).wait()
        _row_copy(y_ref, dest_ref[0, 0, 2 * r + 1], buf.at[1], r, sem).wait()
        return carry

    lax.fori_loop(0, t, start, 0, unroll=DMA_UNROLL)
    lax.fori_loop(0, t, wait, 0, unroll=DMA_UNROLL)
    wt = wt_ref[...]
    moe = buf[0] * wt[:, 0:1] + buf[1] * wt[:, 1:2]
    out = h_ref[...] + moe
    o_ref[...] = _rmsnorm(out, nw_ref[...]) if final_norm else out


def _combine(dest3, h, wt, nw, y, t, final_norm):
    n, d = h.shape
    return pl.pallas_call(
        functools.partial(_combine_kernel, final_norm=final_norm),
        grid=(n // t,),
        in_specs=[pl.BlockSpec((1, 1, 2 * t), lambda i: (i, 0, 0), memory_space=pltpu.SMEM),
                  pl.BlockSpec((t, d), lambda i: (i, 0)),
                  pl.BlockSpec((t, LANES), lambda i: (i, 0)),
                  pl.BlockSpec((1, d), lambda i: (0, 0)),
                  pl.BlockSpec(memory_space=pl.ANY)],
        out_specs=pl.BlockSpec((t, d), lambda i: (i, 0)),
        out_shape=jax.ShapeDtypeStruct((n, d), F32),
        scratch_shapes=[pltpu.VMEM((2, t, d), F32), pltpu.SemaphoreType.DMA(())],
        compiler_params=_params("arbitrary"),
        name="moe_combine",
    )(dest3, h, wt, nw, y)


def _tile(n, pref):
    return pref if n % pref == 0 else n


def kernel(x, mem, norm1_w, w_in, rnn_conv_w, rnn_conv_b, rglru_wa, rglru_ba, rglru_wx, rglru_bx, rglru_lambda, w_branch_a, dn_conv_w, dn_a_log, dn_dt_bias, dn_norm_w, w_branch_b, w_out, norm2_w, mem_norm_w, w_cq, w_ckv, w_co, norm3_w, w_router_group, b_router_group, w_router_expert, b_router_expert, w_exp_gate, w_exp_up, w_exp_down, norm_f_w):
    bsz, seq, d = x.shape
    n = bsz * seq
    n_mem = mem.shape[1]
    depth = w_in.shape[0]
    d_rnn = rnn_conv_w.shape[2]
    n_heads = dn_a_log.shape[1]
    dn_w = n_heads * HEAD_DIM
    row = lambda v: v.reshape(1, -1).astype(F32)

    h = x.reshape(n, d)
    mem2 = mem.reshape(bsz * n_mem, d)
    tm = _tile(n, 2048)
    for l in range(depth):
        o_rg, o_qkv, o_z = d_rnn, 2 * d_rnn, 2 * d_rnn + 3 * dn_w
        o_a = o_z + dn_w
        o_ga = o_a + 2 * n_heads
        wi = w_in[l]
        w_cat = jnp.concatenate([wi[:, :o_a], wi[:, o_ga:]], axis=1).astype(BF16)
        w_ab = jnp.pad(wi[:, o_a:o_ga], ((0, 0), (0, LANES - 2 * n_heads))).astype(BF16)
        proj_a, proj_b, ab = _in_proj(h, row(norm1_w[l]), w_cat, w_ab, o_qkv, tm, 512)

        gated_a = _rglru(proj_a, bsz, seq, rnn_conv_w[l], row(rnn_conv_b[l]), rglru_wa[l].astype(BF16),
                         row(rglru_ba[l]), rglru_wx[l].astype(BF16), row(rglru_bx[l]), row(rglru_lambda[l]),
                         _tile(seq, 512))

        a_dec = jnp.exp(dn_a_log[l].astype(F32))
        pad_h = lambda v: jnp.pad(v, (0, LANES - n_heads))
        acol, dcol = row(pad_h(a_dec)), row(pad_h(dn_dt_bias[l]))
        arow = jnp.broadcast_to(jnp.pad(a_dec, (0, n_heads))[:, None], (2 * n_heads, GDN_CHUNK))
        drow = jnp.broadcast_to(jnp.pad(dn_dt_bias[l], (0, n_heads))[:, None], (2 * n_heads, GDN_CHUNK))
        abt = ab[:, :2 * n_heads].T
        gated_b = _gdn(proj_b, ab, abt, bsz, seq, dn_conv_w[l], acol, dcol, arow, drow, row(dn_norm_w[l]), n_heads)

        h = _merge(h, gated_a, gated_b, proj_b, w_branch_a[l].astype(BF16), w_branch_b[l].astype(BF16),
                   w_out[l].astype(BF16), _tile(n, 512))

        kv = _norm_mm(mem2, row(mem_norm_w[l]), w_ckv[l].astype(BF16), BF16, _tile(bsz * n_mem, 1024), 512)
        h = _cross(h, row(norm2_w[l]), w_cq[l].astype(BF16), kv, w_co[l].astype(BF16), bsz, seq, n_mem,
                   _tile(seq, 512))

        w_r = jnp.pad(jnp.concatenate([w_router_expert[l], w_router_group[l]], axis=1),
                      ((0, 0), (0, LANES - N_EXPERTS - N_GROUPS)))
        b_r = row(jnp.pad(jnp.concatenate([b_router_expert[l], b_router_group[l]]), (0, LANES - N_EXPERTS - N_GROUPS)))
        u3, idx, wt, cnt = _router(h, row(norm3_w[l]), w_r, b_r, _tile(n, 512))

        counts = cnt[0, :N_EXPERTS].astype(I32)
        padded = (counts + MOE_BLOCK - 1) // MOE_BLOCK * MOE_BLOCK
        pend = jnp.cumsum(padded)
        pstart = pend - padded
        dest = pstart[idx[:, 0:2]] + idx[:, 2:4]
        n_blocks = (2 * n + N_EXPERTS * (MOE_BLOCK - 1)) // MOE_BLOCK
        block_row = jnp.arange(n_blocks, dtype=I32) * MOE_BLOCK
        block_e = jnp.minimum(jnp.sum((pend[None, :] <= block_row[:, None]).astype(I32), axis=1), N_EXPERTS - 1)
        n_used = (pend[-1:] // MOE_BLOCK).astype(I32)
        t_moe = _tile(n, 512)
        dest3 = dest.reshape(n // t_moe, 1, 2 * t_moe)
        xs = _dispatch(dest3, u3, jnp.zeros((n_blocks * MOE_BLOCK, d), F32), t_moe)
        yb = _experts(block_e, n_used, xs, w_exp_gate[l], w_exp_up[l], w_exp_down[l])
        h = _combine(dest3, h, wt, row(norm_f_w), yb, t_moe, final_norm=(l == depth - 1))
    return h.reshape(bsz, seq, d)
```

```python
import functools
import math

import jax
import jax.numpy as jnp
from jax import lax
from jax.experimental import pallas as pl
from jax.experimental.pallas import tpu as pltpu

F32 = jnp.float32
BF16 = jnp.bfloat16
I32 = jnp.int32
HIGHEST = lax.Precision.HIGHEST

NORM_EPS = 1e-6
CONV_TAPS = 4
RNN_BLOCK = 128
RG_POWER = 8.0
HEAD_DIM = 128
GDN_CHUNK = 128
GDN_CHUNKS_PER_STEP = 2
CA_HEADS = 4
N_GROUPS = 8
GROUP_SIZE = 8
N_EXPERTS = N_GROUPS * GROUP_SIZE
MOE_BLOCK = 512
LANES = 128
SUBLANES = 8
DMA_UNROLL = 8
VMEM_LIMIT = 48 * 1024 * 1024


def _params(*semantics):
    return pltpu.CompilerParams(dimension_semantics=semantics, vmem_limit_bytes=VMEM_LIMIT)


def _dot(a, b):
    return jnp.dot(a.astype(BF16), b.astype(BF16), preferred_element_type=F32)


def _dot_nt(a, b):
    return lax.dot_general(a.astype(BF16), b.astype(BF16), (((1,), (1,)), ((), ())),
                           preferred_element_type=F32)


def _dot_tn(a, b):
    return lax.dot_general(a.astype(BF16), b.astype(BF16), (((0,), (0,)), ((), ())),
                           preferred_element_type=F32)


def _dot_f32(a, b):
    return jnp.dot(a, b, precision=HIGHEST, preferred_element_type=F32)


def _split(x):
    hi = x.astype(BF16)
    return hi, (x - hi.astype(F32)).astype(BF16)


def _dot_split(a_parts, b_parts):
    (a_hi, a_lo), (b_hi, b_lo) = a_parts, b_parts
    dot = functools.partial(jnp.dot, preferred_element_type=F32)
    return dot(a_hi, b_hi) + (dot(a_hi, b_lo) + dot(a_lo, b_hi))


def _rmsnorm(x, w):
    return x * lax.rsqrt(jnp.mean(x * x, axis=-1, keepdims=True) + NORM_EPS) * w


def _sigmoid(x):
    return 0.5 * jnp.tanh(0.5 * x) + 0.5


def _silu(x):
    return x * _sigmoid(x)


def _softplus(x):
    return jnp.maximum(x, 0.0) + jnp.log(1.0 + jnp.exp(-jnp.abs(x)))


def _one_minus_exp2(y, exp_y):
    return jnp.tanh(-y) * (1.0 + exp_y * exp_y)


def _gelu_tanh(x):
    return 0.5 * x * (1.0 + jnp.tanh(math.sqrt(2.0 / math.pi) * (x + 0.044715 * (x * x * x))))


def _norm_mm_kernel(x_ref, nw_ref, w_ref, o_ref, u_ref):
    @pl.when(pl.program_id(1) == 0)
    def _():
        u_ref[...] = _rmsnorm(x_ref[...], nw_ref[...]).astype(BF16)

    o_ref[...] = jnp.dot(u_ref[...], w_ref[...], preferred_element_type=F32).astype(o_ref.dtype)


def _norm_mm(x, nw, w, out_dtype, tm, tn):
    n, d = x.shape
    c = w.shape[1]
    return pl.pallas_call(
        _norm_mm_kernel,
        grid=(n // tm, c // tn),
        in_specs=[pl.BlockSpec((tm, d), lambda i, j: (i, 0)),
                  pl.BlockSpec((1, d), lambda i, j: (0, 0)),
                  pl.BlockSpec((d, tn), lambda i, j: (0, j))],
        out_specs=pl.BlockSpec((tm, tn), lambda i, j: (i, j)),
        out_shape=jax.ShapeDtypeStruct((n, c), out_dtype),
        scratch_shapes=[pltpu.VMEM((tm, d), BF16)],
        compiler_params=_params("parallel", "arbitrary"),
        name="norm_mm",
    )(x, nw, w)


def _in_proj_kernel(x_ref, nw_ref, w_ref, wab_ref, oa_ref, ob_ref, ab_ref, u_ref, *, n_a):
    j = pl.program_id(1)

    @pl.when(j == 0)
    def _():
        u_ref[...] = _rmsnorm(x_ref[...], nw_ref[...]).astype(BF16)
        ab_ref[...] = jnp.dot(u_ref[...], wab_ref[...], preferred_element_type=F32)

    @pl.when(j < n_a)
    def _():
        oa_ref[...] = jnp.dot(u_ref[...], w_ref[...], preferred_element_type=F32).astype(oa_ref.dtype)

    @pl.when(j >= n_a)
    def _():
        ob_ref[...] = jnp.dot(u_ref[...], w_ref[...], preferred_element_type=F32).astype(ob_ref.dtype)


def _in_proj(x, nw, w_cat, w_ab, c_a, tm, tn):
    n, d = x.shape
    c = w_cat.shape[1]
    n_a, n_b = c_a // tn, (c - c_a) // tn
    return pl.pallas_call(
        functools.partial(_in_proj_kernel, n_a=n_a),
        grid=(n // tm, n_a + n_b),
        in_specs=[pl.BlockSpec((tm, d), lambda i, j: (i, 0)),
                  pl.BlockSpec((1, d), lambda i, j: (0, 0)),
                  pl.BlockSpec((d, tn), lambda i, j: (0, j)),
                  pl.BlockSpec(w_ab.shape, lambda i, j: (0, 0))],
        out_specs=[pl.BlockSpec((tm, tn), lambda i, j: (i, jnp.minimum(j, n_a - 1))),
                   pl.BlockSpec((tm, tn), lambda i, j: (i, jnp.maximum(j - n_a, 0))),
                   pl.BlockSpec((tm, w_ab.shape[1]), lambda i, j: (i, 0))],
        out_shape=[jax.ShapeDtypeStruct((n, c_a), BF16), jax.ShapeDtypeStruct((n, c - c_a), BF16),
                   jax.ShapeDtypeStruct((n, w_ab.shape[1]), F32)],
        scratch_shapes=[pltpu.VMEM((tm, d), BF16)],
        compiler_params=_params("arbitrary", "arbitrary"),
        name="in_proj",
    )(x, nw, w_cat, w_ab)


def _load_conv_window(x_ref, xbuf, ts):
    @pl.when(pl.program_id(1) == 0)
    def _():
        xbuf[0:SUBLANES, :] = jnp.zeros((SUBLANES, xbuf.shape[1]), F32)

    @pl.when(pl.program_id(1) != 0)
    def _():
        xbuf[0:SUBLANES, :] = xbuf[ts:ts + SUBLANES, :]

    xbuf[SUBLANES:SUBLANES + ts, :] = x_ref[...].astype(F32)


def _causal_conv(xbuf, cw_ref, ts, first_row=0, cols=slice(None)):
    base = first_row + SUBLANES - (CONV_TAPS - 1)
    acc = cw_ref[0:1, cols] * xbuf[base:base + ts, cols]
    for k in range(1, CONV_TAPS):
        acc = acc + cw_ref[k:k + 1, cols] * xbuf[base + k:base + k + ts, cols]
    return acc


def _rglru_kernel(rx_ref, rg_ref, cw_ref, cb_ref, wa_ref, ba_ref, wx_ref, bx_ref, lam_ref, o_ref,
                  xbuf, a_ref, b_ref, carry_ref):
    ts, c = rx_ref.shape
    _load_conv_window(rx_ref, xbuf, ts)

    @pl.when(pl.program_id(1) == 0)
    def _():
        carry_ref[...] = jnp.zeros_like(carry_ref)

    xc = _causal_conv(xbuf, cw_ref, ts) + cb_ref[...]
    neg_sp = -RG_POWER * _softplus(-lam_ref[...])
    for n in range(c // RNN_BLOCK):
        sl = slice(n * RNN_BLOCK, (n + 1) * RNN_BLOCK)
        xb = xc[:, sl]
        r = _sigmoid(_dot(xb, wa_ref[n]) + ba_ref[:, sl])
        i = _sigmoid(_dot(xb, wx_ref[n]) + bx_ref[:, sl])
        log_a = neg_sp[:, sl] * r
        a = jnp.exp(log_a)
        a_ref[:, sl] = a
        b_ref[:, sl] = jnp.sqrt(_one_minus_exp2(log_a, a)) * (i * xb)

    row = lax.broadcasted_iota(I32, (SUBLANES, c), 0)

    def slab(t, carry):
        rows = pl.ds(pl.multiple_of(t * SUBLANES, SUBLANES), SUBLANES)
        a = a_ref[rows, :]
        b = b_ref[rows, :]
        for d in (1, 2, 4):
            a_sh = jnp.where(row >= d, pltpu.roll(a, d, 0), 1.0)
            b_sh = jnp.where(row >= d, pltpu.roll(b, d, 0), 0.0)
            b = a * b_sh + b
            a = a * a_sh
        h = a * carry + b
        b_ref[rows, :] = h
        return h[SUBLANES - 1:SUBLANES, :]

    carry_ref[...] = lax.fori_loop(0, ts // SUBLANES, slab, carry_ref[...])
    o_ref[...] = (_gelu_tanh(rg_ref[...].astype(F32)) * b_ref[...]).astype(o_ref.dtype)


def _rglru(proj_a, bsz, seq, cw, cb, wa, ba, wx, bx, lam, ts):
    c = cw.shape[1]
    ns = seq // ts
    full = lambda shape: pl.BlockSpec(shape, lambda b, s: (0,) * len(shape))
    return pl.pallas_call(
        _rglru_kernel,
        grid=(bsz, ns),
        in_specs=[pl.BlockSpec((ts, c), lambda b, s: (b * ns + s, 0)),
                  pl.BlockSpec((ts, c), lambda b, s: (b * ns + s, 1)),
                  full(cw.shape), full(cb.shape), full(wa.shape), full(ba.shape),
                  full(wx.shape), full(bx.shape), full(lam.shape)],
        out_specs=pl.BlockSpec((ts, c), lambda b, s: (b * ns + s, 0)),
        out_shape=jax.ShapeDtypeStruct((bsz * seq, c), BF16),
        scratch_shapes=[pltpu.VMEM((ts + SUBLANES, c), F32), pltpu.VMEM((ts, c), F32),
                        pltpu.VMEM((ts, c), F32), pltpu.VMEM((1, c), F32)],
        compiler_params=_params("parallel", "arbitrary"),
        name="rglru",
    )(proj_a, proj_a, cw, cb, wa, ba, wx, bx, lam)


def _lane_bcast(x, lane):
    return jnp.broadcast_to(x[:, lane:lane + 1], x.shape)


def _unit_lower_inverses(neg_ls, eye, fillers):
    ts = eye.shape[0]
    run_filler = lambda: next(fillers, lambda: None)()
    ts_mats = [eye + n for n in neg_ls]
    ps = [_dot_split(_split(n), _split(n)) for n in neg_ls]
    run_filler()
    levels = int(math.log2(ts)) - 1
    for lvl in range(levels - 1):
        p_parts = [_split(p) for p in ps]
        lhs_parts = [tuple(jnp.concatenate([a, b], axis=0) for a, b in zip(_split(t), pp))
                     for t, pp in zip(ts_mats, p_parts)]
        both = [_dot_split(lp, pp) for lp, pp in zip(lhs_parts, p_parts)]
        ts_mats = [t + b[:ts] for t, b in zip(ts_mats, both)]
        ps = [b[ts:] for b in both]
        run_filler()
    return [t + _dot_split(_split(t), _split(p)) for t, p in zip(ts_mats, ps)]


def _gdn_kernel(qkv_ref, z_ref, ab_ref, abt_ref, cw_ref, acol_ref, dcol_ref, arow_ref, drow_ref, nw_ref,
                o_ref, xbuf, s_ref):
    ts = GDN_CHUNK
    n_heads = s_ref.shape[0]
    dn_w = n_heads * HEAD_DIM
    _load_conv_window(qkv_ref, xbuf, qkv_ref.shape[0])

    @pl.when(pl.program_id(1) == 0)
    def _():
        s_ref[...] = jnp.zeros_like(s_ref)

    row = lax.broadcasted_iota(I32, (ts, ts), 0)
    col = lax.broadcasted_iota(I32, (ts, ts), 1)
    lower_incl = (row >= col).astype(F32)
    upper_incl = (row <= col).astype(F32)
    eye = (row == col).astype(F32)
    head_cols = lambda h, part: slice(part * dn_w + h * HEAD_DIM, part * dn_w + (h + 1) * HEAD_DIM)
    heads = range(n_heads)

    n_chunks = qkv_ref.shape[0] // ts
    state = [dict(q=[None] * n_heads, k=[None] * n_heads, k_beta=[None] * n_heads, decay=[None] * n_heads,
                  rhs=[None] * n_heads, qd=[None] * n_heads, kd=[None] * n_heads) for _ in range(n_chunks)]

    def prepare_gates(c):
        st = state[c]
        rows = slice(c * ts, (c + 1) * ts)
        ab = ab_ref[rows, :]
        g_col = -acol_ref[...] * _softplus(ab + dcol_ref[...])
        st["beta"] = _sigmoid(ab)
        g_row = -arow_ref[...] * _softplus(abt_ref[:, rows] + drow_ref[...])
        st["cum_col"] = _dot_f32(lower_incl, g_col)
        st["cum_row"] = _dot_f32(g_row, upper_incl)
        last = st["cum_col"][ts - 1:ts, :]
        st["exp_cum"] = jnp.exp(st["cum_col"])
        st["exp_rem"] = jnp.exp(jnp.broadcast_to(last, (ts, LANES)) - st["cum_col"])
        st["exp_last"] = jnp.exp(last)

    def conv_silu(c, h, part):
        return _silu(_causal_conv(xbuf, cw_ref, ts, c * ts, head_cols(h, part)))

    def prepare_key(c, h):
        st = state[c]
        k = conv_silu(c, h, 1)
        k = k * lax.rsqrt(jnp.sum(k * k, axis=-1, keepdims=True) + NORM_EPS)
        st["k"][h], st["k_beta"][h] = k, k * _lane_bcast(st["beta"], n_heads + h)
        st["decay"][h] = jnp.exp(jnp.minimum(_lane_bcast(st["cum_col"], h) - st["cum_row"][h:h + 1, :], 0.0))

    def prepare_query_value(c, h):
        st = state[c]
        q, v = conv_silu(c, h, 0), conv_silu(c, h, 2)
        q = q * (lax.rsqrt(jnp.sum(q * q, axis=-1, keepdims=True) + NORM_EPS) * (HEAD_DIM ** -0.5))
        e_cum = _lane_bcast(st["exp_cum"], h)
        st["q"][h] = q
        st["rhs"][h] = jnp.concatenate([v * _lane_bcast(st["beta"], n_heads + h), st["k_beta"][h] * e_cum], axis=1)
        st["qd"][h] = q * e_cum
        st["kd"][h] = st["k"][h] * _lane_bcast(st["exp_rem"], h)

    def key_thunks(c):
        if c >= n_chunks:
            return []
        return [functools.partial(prepare_gates, c)] + [functools.partial(prepare_key, c, h) for h in heads]

    def solve(c):
        st = state[c]
        kks = [_dot_nt(st["k_beta"][h], st["k"][h]) for h in heads]
        neg_ls = [jnp.where(row > col, -(kks[h] * st["decay"][h]), 0.0) for h in heads]
        thunks = [functools.partial(prepare_query_value, c, h) for h in heads] + key_thunks(c + 1)
        slots = int(math.log2(ts))
        per_slot = -(-len(thunks) // slots)
        groups = [thunks[i:i + per_slot] for i in range(0, len(thunks), per_slot)]
        fillers = iter([functools.partial(lambda g: [t() for t in g], g) for g in groups])
        t_mats = _unit_lower_inverses(neg_ls, eye, fillers)
        for group in fillers:
            group()
        qks = [_dot_nt(st["q"][h], st["k"][h]) for h in heads]
        st["intra"] = [jnp.where(row >= col, qks[h] * st["decay"][h], 0.0) for h in heads]
        st["uw"] = [_dot(t_mats[h], st["rhs"][h]) for h in heads]

    def advance(c):
        st = state[c]
        rows = slice(c * ts, (c + 1) * ts)
        ws_qs = [_dot(jnp.concatenate([st["uw"][h][:, HEAD_DIM:], st["qd"][h]], axis=0), s_ref[h]) for h in heads]
        v_news = [st["uw"][h][:, :HEAD_DIM] - ws_qs[h][:ts] for h in heads]
        mixed = [_dot(jnp.concatenate([st["intra"][h], st["kd"][h].T], axis=0), v_news[h]) for h in heads]
        for h in heads:
            s_ref[h] = (s_ref[h] * _lane_bcast(jnp.broadcast_to(st["exp_last"], (HEAD_DIM, LANES)), h)
                        + mixed[h][ts:])
            o = _rmsnorm(ws_qs[h][ts:] + mixed[h][:ts], nw_ref[...]) * _silu(z_ref[rows, head_cols(h, 0)].astype(F32))
            o_ref[rows, head_cols(h, 0)] = o.astype(o_ref.dtype)

    for thunk in key_thunks(0):
        thunk()
    for c in range(n_chunks):
        solve(c)
        advance(c)


def _gdn(proj_b, ab, abt, bsz, seq, cw, acol, dcol, arow, drow, nw, n_heads):
    ts = GDN_CHUNK * GDN_CHUNKS_PER_STEP
    assert seq % ts == 0
    ns = seq // ts
    dn_w = n_heads * HEAD_DIM
    full = lambda shape: pl.BlockSpec(shape, lambda b, s: (0,) * len(shape))
    return pl.pallas_call(
        _gdn_kernel,
        grid=(bsz, ns),
        in_specs=[pl.BlockSpec((ts, 3 * dn_w), lambda b, s: (b * ns + s, 0)),
                  pl.BlockSpec((ts, dn_w), lambda b, s: (b * ns + s, 3)),
                  pl.BlockSpec((ts, LANES), lambda b, s: (b * ns + s, 0)),
                  pl.BlockSpec((2 * n_heads, ts), lambda b, s: (0, b * ns + s)),
                  full(cw.shape), full(acol.shape), full(dcol.shape), full(arow.shape), full(drow.shape),
                  full(nw.shape)],
        out_specs=pl.BlockSpec((ts, dn_w), lambda b, s: (b * ns + s, 0)),
        out_shape=jax.ShapeDtypeStruct((bsz * seq, dn_w), BF16),
        scratch_shapes=[pltpu.VMEM((ts + SUBLANES, 3 * dn_w), F32),
                        pltpu.VMEM((n_heads, HEAD_DIM, HEAD_DIM), F32)],
        compiler_params=_params("parallel", "arbitrary"),
        name="gdn",
    )(proj_b, proj_b, ab, abt, cw, acol, dcol, arow, drow, nw)


def _merge_kernel(x_ref, ya_ref, yb_ref, ga_ref, gb_ref, wa_ref, wb_ref, wo_ref, o_ref):
    y_a = jnp.dot(ya_ref[...], wa_ref[...], preferred_element_type=F32)
    y_b = jnp.dot(yb_ref[...], wb_ref[...], preferred_element_type=F32)
    m = _sigmoid(ga_ref[...].astype(F32)) * y_a + _sigmoid(gb_ref[...].astype(F32)) * y_b
    o_ref[...] = x_ref[...] + _dot(m, wo_ref[...])


def _merge(x, gated_a, gated_b, proj_b, w_a, w_b, w_o, tm):
    n, d = x.shape
    full = lambda shape: pl.BlockSpec(shape, lambda i: (0,) * len(shape))
    return pl.pallas_call(
        _merge_kernel,
        grid=(n // tm,),
        in_specs=[pl.BlockSpec((tm, d), lambda i: (i, 0)),
                  pl.BlockSpec((tm, gated_a.shape[1]), lambda i: (i, 0)),
                  pl.BlockSpec((tm, gated_b.shape[1]), lambda i: (i, 0)),
                  pl.BlockSpec((tm, d), lambda i: (i, 4)),
                  pl.BlockSpec((tm, d), lambda i: (i, 5)),
                  full(w_a.shape), full(w_b.shape), full(w_o.shape)],
        out_specs=pl.BlockSpec((tm, d), lambda i: (i, 0)),
        out_shape=jax.ShapeDtypeStruct((n, d), F32),
        compiler_params=_params("parallel"),
        name="merge",
    )(x, gated_a, gated_b, proj_b, proj_b, w_a, w_b, w_o)


def _cross_kernel(h_ref, nw_ref, wq_ref, kv_ref, wo_ref, o_ref):
    x = h_ref[...]
    d = x.shape[1]
    hd = d // CA_HEADS
    u = _rmsnorm(x, nw_ref[...])
    q = _dot(u, wq_ref[...])
    outs = []
    for h in range(CA_HEADS):
        k_h = kv_ref[:, h * hd:(h + 1) * hd]
        v_h = kv_ref[:, d + h * hd:d + (h + 1) * hd]
        s = _dot_nt(q[:, h * hd:(h + 1) * hd], k_h) * (hd ** -0.5)
        s = s - jnp.max(s, axis=-1, keepdims=True)
        e = jnp.exp(s)
        p = e / jnp.sum(e, axis=-1, keepdims=True)
        outs.append(_dot(p, v_h))
    o = jnp.concatenate(outs, axis=1)
    o_ref[...] = x + _dot(o, wo_ref[...])


def _cross(h, nw, w_q, kv, w_o, bsz, seq, n_mem, ts):
    n, d = h.shape
    ns = seq // ts
    full = lambda shape: pl.BlockSpec(shape, lambda b, s: (0,) * len(shape))
    return pl.pallas_call(
        _cross_kernel,
        grid=(bsz, ns),
        in_specs=[pl.BlockSpec((ts, d), lambda b, s: (b * ns + s, 0)),
                  full(nw.shape), full(w_q.shape),
                  pl.BlockSpec((n_mem, 2 * d), lambda b, s: (b, 0)),
                  full(w_o.shape)],
        out_specs=pl.BlockSpec((ts, d), lambda b, s: (b * ns + s, 0)),
        out_shape=jax.ShapeDtypeStruct((n, d), F32),
        compiler_params=_params("parallel", "parallel"),
        name="cross_attn",
    )(h, nw, w_q, kv, w_o)


def _router_kernel(h_ref, nw_ref, wr_ref, br_ref, u_ref, idx_ref, wt_ref, cnt_ref, base_ref):
    t = h_ref.shape[0]

    @pl.when(pl.program_id(0) == 0)
    def _():
        base_ref[...] = jnp.zeros_like(base_ref)

    u = _rmsnorm(h_ref[...], nw_ref[...])
    u_ref[...] = u
    logits = _dot_split(_split(u), _split(wr_ref[...])) + br_ref[...]
    lane = lax.broadcasted_iota(I32, (t, LANES), 1)
    lanef = lane.astype(F32)
    big = float(LANES)
    neg = -jnp.inf

    lg = jnp.where((lane >= N_EXPERTS) & (lane < N_EXPERTS + N_GROUPS), logits, neg)
    gmax = jnp.max(lg, axis=-1, keepdims=True)
    grp = jnp.min(jnp.where(lg == gmax, lanef - float(N_EXPERTS), big), axis=-1, keepdims=True)
    p_grp = 1.0 / jnp.sum(jnp.exp(lg - gmax), axis=-1, keepdims=True)

    in_grp = (lane < N_EXPERTS) & ((lane // GROUP_SIZE).astype(F32) == grp)
    le = jnp.where(in_grp, logits, neg)
    m1 = jnp.max(le, axis=-1, keepdims=True)
    i1 = jnp.min(jnp.where(le == m1, lanef, big), axis=-1, keepdims=True)
    le2 = jnp.where(lanef == i1, neg, le)
    m2 = jnp.max(le2, axis=-1, keepdims=True)
    i2 = jnp.min(jnp.where(le2 == m2, lanef, big), axis=-1, keepdims=True)
    ratio = jnp.exp(m2 - m1)
    p1 = 1.0 / (1.0 + ratio)
    p2 = ratio * p1

    oh1 = lanef == i1
    oh2 = lanef == i2
    onehot = jnp.where(oh1 | oh2, 1.0, 0.0)
    r_i = lax.broadcasted_iota(I32, (t, t), 0)
    c_i = lax.broadcasted_iota(I32, (t, t), 1)
    before = jnp.where(r_i > c_i, 1.0, 0.0)
    rank = _dot(before, onehot) + base_ref[...]
    r1 = jnp.sum(jnp.where(oh1, rank, 0.0), axis=-1, keepdims=True)
    r2 = jnp.sum(jnp.where(oh2, rank, 0.0), axis=-1, keepdims=True)
    base_ref[...] = base_ref[...] + jnp.sum(onehot, axis=0, keepdims=True)

    idx = jnp.where(lane == 0, i1, jnp.where(lane == 1, i2, jnp.where(lane == 2, r1, jnp.where(lane == 3, r2, 0.0))))
    idx_ref[...] = idx.astype(I32)
    wt_ref[...] = jnp.where(lane == 0, p_grp * p1, jnp.where(lane == 1, p_grp * p2, 0.0))
    cnt_ref[...] = jnp.broadcast_to(base_ref[...], cnt_ref.shape)


def _router(h, nw, wr, br, t):
    n, d = h.shape
    full = lambda shape: pl.BlockSpec(shape, lambda i: (0,) * len(shape))
    return pl.pallas_call(
        _router_kernel,
        grid=(n // t,),
        in_specs=[pl.BlockSpec((t, d), lambda i: (i, 0)), full(nw.shape), full(wr.shape), full(br.shape)],
        out_specs=[pl.BlockSpec((t, d), lambda i: (i, 0)),
                   pl.BlockSpec((t, LANES), lambda i: (i, 0)),
                   pl.BlockSpec((t, LANES), lambda i: (i, 0)),
                   pl.BlockSpec((SUBLANES, LANES), lambda i: (0, 0))],
        out_shape=[jax.ShapeDtypeStruct((n, d), F32),
                   jax.ShapeDtypeStruct((n, LANES), I32),
                   jax.ShapeDtypeStruct((n, LANES), F32),
                   jax.ShapeDtypeStruct((SUBLANES, LANES), F32)],
        scratch_shapes=[pltpu.VMEM((1, LANES), F32)],
        compiler_params=_params("arbitrary"),
        name="router",
    )(h, nw, wr, br)


def _row_copy(src, src_row, dst, dst_row, sem):
    return pltpu.make_async_copy(src.at[pl.ds(src_row, 1)], dst.at[pl.ds(dst_row, 1)], sem)


def _dispatch_kernel(dest_ref, u_ref, xs_in_ref, xs_ref, sem):
    del xs_in_ref
    t = u_ref.shape[0]

    def start(r, carry):
        _row_copy(u_ref, r, xs_ref, dest_ref[0, 0, 2 * r], sem).start()
        _row_copy(u_ref, r, xs_ref, dest_ref[0, 0, 2 * r + 1], sem).start()
        return carry

    def wait(r, carry):
        _row_copy(u_ref, r, xs_ref, dest_ref[0, 0, 2 * r], sem).wait()
        _row_copy(u_ref, r, xs_ref, dest_ref[0, 0, 2 * r + 1], sem).wait()
        return carry

    lax.fori_loop(0, t, start, 0, unroll=DMA_UNROLL)
    lax.fori_loop(0, t, wait, 0, unroll=DMA_UNROLL)


def _dispatch(dest3, u, xs_zero, t):
    n, d = u.shape
    return pl.pallas_call(
        _dispatch_kernel,
        grid=(n // t,),
        in_specs=[pl.BlockSpec((1, 1, 2 * t), lambda i: (i, 0, 0), memory_space=pltpu.SMEM),
                  pl.BlockSpec((t, d), lambda i: (i, 0)),
                  pl.BlockSpec(memory_space=pl.ANY)],
        out_specs=pl.BlockSpec(memory_space=pl.ANY),
        out_shape=jax.ShapeDtypeStruct(xs_zero.shape, xs_zero.dtype),
        scratch_shapes=[pltpu.SemaphoreType.DMA(())],
        input_output_aliases={2: 0},
        compiler_params=_params("arbitrary"),
        name="moe_dispatch",
    )(dest3, u, xs_zero)


def _expert_kernel(be_ref, nu_ref, xs_ref, wg_ref, wu_ref, wd_ref, y_ref, wg_bf, wu_bf, wd_bf):
    j = pl.program_id(0)
    used = j < nu_ref[0]
    new_expert = (j == 0) | (be_ref[j] != be_ref[jnp.maximum(j - 1, 0)])

    @pl.when(used & new_expert)
    def _():
        wg_bf[...] = wg_ref[0].astype(BF16)
        wu_bf[...] = wu_ref[0].astype(BF16)
        wd_bf[...] = wd_ref[0].astype(BF16)

    @pl.when(used)
    def _():
        x = xs_ref[...].astype(BF16)
        hid = _silu(jnp.dot(x, wg_bf[...], preferred_element_type=F32)) * jnp.dot(
            x, wu_bf[...], preferred_element_type=F32)
        y_ref[...] = _dot(hid, wd_bf[...])

    @pl.when(jnp.logical_not(used))
    def _():
        y_ref[...] = jnp.zeros_like(y_ref)


def _experts(block_e, n_used, xs, w_gate, w_up, w_down):
    rows, d = xs.shape
    de = w_gate.shape[2]
    grid_spec = pltpu.PrefetchScalarGridSpec(
        num_scalar_prefetch=2,
        grid=(rows // MOE_BLOCK,),
        in_specs=[pl.BlockSpec((MOE_BLOCK, d), lambda j, be, nu: (j, 0)),
                  pl.BlockSpec((1, d, de), lambda j, be, nu: (be[j], 0, 0)),
                  pl.BlockSpec((1, d, de), lambda j, be, nu: (be[j], 0, 0)),
                  pl.BlockSpec((1, de, d), lambda j, be, nu: (be[j], 0, 0))],
        out_specs=pl.BlockSpec((MOE_BLOCK, d), lambda j, be, nu: (j, 0)),
        scratch_shapes=[pltpu.VMEM((d, de), BF16), pltpu.VMEM((d, de), BF16), pltpu.VMEM((de, d), BF16)],
    )
    return pl.pallas_call(
        _expert_kernel,
        grid_spec=grid_spec,
        out_shape=jax.ShapeDtypeStruct((rows, d), F32),
        compiler_params=_params("arbitrary"),
        name="moe_experts",
    )(block_e, n_used, xs, w_gate, w_up, w_down)


def _combine_kernel(dest_ref, h_ref, wt_ref, nw_ref, y_ref, o_ref, buf, sem, *, final_norm):
    t = h_ref.shape[0]

    def start(r, carry):
        _row_copy(y_ref, dest_ref[0, 0, 2 * r], buf.at[0], r, sem).start()
        _row_copy(y_ref, dest_ref[0, 0, 2 * r + 1], buf.at[1], r, sem).start()
        return carry

    def wait(r, carry):
        _row_copy(y_ref, dest_ref[0, 0, 2 * r], buf.at[0], r, sem).wait()
        _row_copy(y_ref, dest_ref[0, 0, 2 * r + 1], buf.at[1], r, sem).wait()
        return carry

    lax.fori_loop(0, t, start, 0, unroll=DMA_UNROLL)
    lax.fori_loop(0, t, wait, 0, unroll=DMA_UNROLL)
    wt = wt_ref[...]
    moe = buf[0] * wt[:, 0:1] + buf[1] * wt[:, 1:2]
    out = h_ref[...] + moe
    o_ref[...] = _rmsnorm(out, nw_ref[...]) if final_norm else out


def _combine(dest3, h, wt, nw, y, t, final_norm):
    n, d = h.shape
    return pl.pallas_call(
        functools.partial(_combine_kernel, final_norm=final_norm),
        grid=(n // t,),
        in_specs=[pl.BlockSpec((1, 1, 2 * t), lambda i: (i, 0, 0), memory_space=pltpu.SMEM),
                  pl.BlockSpec((t, d), lambda i: (i, 0)),
                  pl.BlockSpec((t, LANES), lambda i: (i, 0)),
                  pl.BlockSpec((1, d), lambda i: (0, 0)),
                  pl.BlockSpec(memory_space=pl.ANY)],
        out_specs=pl.BlockSpec((t, d), lambda i: (i, 0)),
        out_shape=jax.ShapeDtypeStruct((n, d), F32),
        scratch_shapes=[pltpu.VMEM((2, t, d), F32), pltpu.SemaphoreType.DMA(())],
        compiler_params=_params("arbitrary"),
        name="moe_combine",
    )(dest3, h, wt, nw, y)


def _tile(n, pref):
    return pref if n % pref == 0 else n


def kernel(x, mem, norm1_w, w_in, rnn_conv_w, rnn_conv_b, rglru_wa, rglru_ba, rglru_wx, rglru_bx, rglru_lambda, w_branch_a, dn_conv_w, dn_a_log, dn_dt_bias, dn_norm_w, w_branch_b, w_out, norm2_w, mem_norm_w, w_cq, w_ckv, w_co, norm3_w, w_router_group, b_router_group, w_router_expert, b_router_expert, w_exp_gate, w_exp_up, w_exp_down, norm_f_w):
    bsz, seq, d = x.shape
    n = bsz * seq
    n_mem = mem.shape[1]
    depth = w_in.shape[0]
    d_rnn = rnn_conv_w.shape[2]
    n_heads = dn_a_log.shape[1]
    dn_w = n_heads * HEAD_DIM
    row = lambda v: v.reshape(1, -1).astype(F32)

    h = x.reshape(n, d)
    mem2 = mem.reshape(bsz * n_mem, d)
    tm = _tile(n, 2048)
    for l in range(depth):
        o_rg, o_qkv, o_z = d_rnn, 2 * d_rnn, 2 * d_rnn + 3 * dn_w
        o_a = o_z + dn_w
        o_ga = o_a + 2 * n_heads
        wi = w_in[l]
        w_cat = jnp.concatenate([wi[:, :o_a], wi[:, o_ga:]], axis=1).astype(BF16)
        w_ab = jnp.pad(wi[:, o_a:o_ga], ((0, 0), (0, LANES - 2 * n_heads))).astype(BF16)
        proj_a, proj_b, ab = _in_proj(h, row(norm1_w[l]), w_cat, w_ab, o_qkv, tm, 512)

        gated_a = _rglru(proj_a, bsz, seq, rnn_conv_w[l], row(rnn_conv_b[l]), rglru_wa[l].astype(BF16),
                         row(rglru_ba[l]), rglru_wx[l].astype(BF16), row(rglru_bx[l]), row(rglru_lambda[l]),
                         _tile(seq, 512))

        a_dec = jnp.exp(dn_a_log[l].astype(F32))
        pad_h = lambda v: jnp.pad(v, (0, LANES - n_heads))
        acol, dcol = row(pad_h(a_dec)), row(pad_h(dn_dt_bias[l]))
        arow = jnp.broadcast_to(jnp.pad(a_dec, (0, n_heads))[:, None], (2 * n_heads, GDN_CHUNK))
        drow = jnp.broadcast_to(jnp.pad(dn_dt_bias[l], (0, n_heads))[:, None], (2 * n_heads, GDN_CHUNK))
        abt = ab[:, :2 * n_heads].T
        gated_b = _gdn(proj_b, ab, abt, bsz, seq, dn_conv_w[l], acol, dcol, arow, drow, row(dn_norm_w[l]), n_heads)

        h = _merge(h, gated_a, gated_b, proj_b, w_branch_a[l].astype(BF16), w_branch_b[l].astype(BF16),
                   w_out[l].astype(BF16), _tile(n, 512))

        kv = _norm_mm(mem2, row(mem_norm_w[l]), w_ckv[l].astype(BF16), BF16, _tile(bsz * n_mem, 1024), 512)
        h = _cross(h, row(norm2_w[l]), w_cq[l].astype(BF16), kv, w_co[l].astype(BF16), bsz, seq, n_mem,
                   _tile(seq, 512))

        w_r = jnp.pad(jnp.concatenate([w_router_expert[l], w_router_group[l]], axis=1),
                      ((0, 0), (0, LANES - N_EXPERTS - N_GROUPS)))
        b_r = row(jnp.pad(jnp.concatenate([b_router_expert[l], b_router_group[l]]), (0, LANES - N_EXPERTS - N_GROUPS)))
        u3, idx, wt, cnt = _router(h, row(norm3_w[l]), w_r, b_r, _tile(n, 512))

        counts = cnt[0, :N_EXPERTS].astype(I32)
        padded = (counts + MOE_BLOCK - 1) // MOE_BLOCK * MOE_BLOCK
        pend = jnp.cumsum(padded)
        pstart = pend - padded
        is_expert = idx[:, 0:2, None] == jnp.arange(N_EXPERTS, dtype=I32)
        dest = jnp.sum(jnp.where(is_expert, pstart, 0), axis=-1) + idx[:, 2:4]
        n_blocks = (2 * n + N_EXPERTS * (MOE_BLOCK - 1)) // MOE_BLOCK
        block_row = jnp.arange(n_blocks, dtype=I32) * MOE_BLOCK
        block_e = jnp.minimum(jnp.sum((pend[None, :] <= block_row[:, None]).astype(I32), axis=1), N_EXPERTS - 1)
        n_used = (pend[-1:] // MOE_BLOCK).astype(I32)
        t_moe = _tile(n, 512)
        dest3 = dest.reshape(n // t_moe, 1, 2 * t_moe)
        xs = _dispatch(dest3, u3, jnp.zeros((n_blocks * MOE_BLOCK, d), F32), t_moe)
        yb = _experts(block_e, n_used, xs, w_exp_gate[l], w_exp_up[l], w_exp_down[l])
        h = _combine(dest3, h, wt, row(norm_f_w), yb, t_moe, final_norm=(l == depth - 1))
    return h.reshape(bsz, seq, d)
```

```python
import functools
import math

import jax
import jax.numpy as jnp
from jax import lax
from jax.experimental import pallas as pl
from jax.experimental.pallas import tpu as pltpu

F32 = jnp.float32
BF16 = jnp.bfloat16
I32 = jnp.int32
HIGHEST = lax.Precision.HIGHEST

NORM_EPS = 1e-6
CONV_TAPS = 4
RNN_BLOCK = 128
RG_POWER = 8.0
HEAD_DIM = 128
GDN_CHUNK = 128
GDN_CHUNKS_PER_STEP = 2
CA_HEADS = 4
N_GROUPS = 8
GROUP_SIZE = 8
N_EXPERTS = N_GROUPS * GROUP_SIZE
MOE_BLOCK = 512
LANES = 128
SUBLANES = 8
DMA_UNROLL = 8
VMEM_LIMIT = 48 * 1024 * 1024


def _params(*semantics):
    return pltpu.CompilerParams(dimension_semantics=semantics, vmem_limit_bytes=VMEM_LIMIT)


def _dot(a, b):
    return jnp.dot(a.astype(BF16), b.astype(BF16), preferred_element_type=F32)


def _dot_nt(a, b):
    return lax.dot_general(a.astype(BF16), b.astype(BF16), (((1,), (1,)), ((), ())),
                           preferred_element_type=F32)


def _dot_tn(a, b):
    return lax.dot_general(a.astype(BF16), b.astype(BF16), (((0,), (0,)), ((), ())),
                           preferred_element_type=F32)


def _dot_f32(a, b):
    return jnp.dot(a, b, precision=HIGHEST, preferred_element_type=F32)


def _split(x):
    hi = x.astype(BF16)
    return hi, (x - hi.astype(F32)).astype(BF16)


def _dot_split(a_parts, b_parts):
    (a_hi, a_lo), (b_hi, b_lo) = a_parts, b_parts
    dot = functools.partial(jnp.dot, preferred_element_type=F32)
    return dot(a_hi, b_hi) + (dot(a_hi, b_lo) + dot(a_lo, b_hi))


def _rmsnorm(x, w):
    return x * lax.rsqrt(jnp.mean(x * x, axis=-1, keepdims=True) + NORM_EPS) * w


def _sigmoid(x):
    return 0.5 * jnp.tanh(0.5 * x) + 0.5


def _silu(x):
    return x * _sigmoid(x)


def _softplus(x):
    return jnp.maximum(x, 0.0) + jnp.log(1.0 + jnp.exp(-jnp.abs(x)))


def _one_minus_exp2(y, exp_y):
    return jnp.tanh(-y) * (1.0 + exp_y * exp_y)


def _gelu_tanh(x):
    return 0.5 * x * (1.0 + jnp.tanh(math.sqrt(2.0 / math.pi) * (x + 0.044715 * (x * x * x))))


def _norm_mm_kernel(x_ref, nw_ref, w_ref, o_ref, u_ref):
    @pl.when(pl.program_id(1) == 0)
    def _():
        u_ref[...] = _rmsnorm(x_ref[...], nw_ref[...]).astype(BF16)

    o_ref[...] = jnp.dot(u_ref[...], w_ref[...], preferred_element_type=F32).astype(o_ref.dtype)


def _norm_mm(x, nw, w, out_dtype, tm, tn):
    n, d = x.shape
    c = w.shape[1]
    return pl.pallas_call(
        _norm_mm_kernel,
        grid=(n // tm, c // tn),
        in_specs=[pl.BlockSpec((tm, d), lambda i, j: (i, 0)),
                  pl.BlockSpec((1, d), lambda i, j: (0, 0)),
                  pl.BlockSpec((d, tn), lambda i, j: (0, j))],
        out_specs=pl.BlockSpec((tm, tn), lambda i, j: (i, j)),
        out_shape=jax.ShapeDtypeStruct((n, c), out_dtype),
        scratch_shapes=[pltpu.VMEM((tm, d), BF16)],
        compiler_params=_params("parallel", "arbitrary"),
        name="norm_mm",
    )(x, nw, w)


def _in_proj_kernel(x_ref, nw_ref, w_ref, wab_ref, oa_ref, ob_ref, ab_ref, u_ref, *, n_a):
    j = pl.program_id(1)

    @pl.when(j == 0)
    def _():
        u_ref[...] = _rmsnorm(x_ref[...], nw_ref[...]).astype(BF16)
        ab_ref[...] = jnp.dot(u_ref[...], wab_ref[...], preferred_element_type=F32)

    @pl.when(j < n_a)
    def _():
        oa_ref[...] = jnp.dot(u_ref[...], w_ref[...], preferred_element_type=F32).astype(oa_ref.dtype)

    @pl.when(j >= n_a)
    def _():
        ob_ref[...] = jnp.dot(u_ref[...], w_ref[...], preferred_element_type=F32).astype(ob_ref.dtype)


def _in_proj(x, nw, w_cat, w_ab, c_a, tm, tn):
    n, d = x.shape
    c = w_cat.shape[1]
    n_a, n_b = c_a // tn, (c - c_a) // tn
    return pl.pallas_call(
        functools.partial(_in_proj_kernel, n_a=n_a),
        grid=(n // tm, n_a + n_b),
        in_specs=[pl.BlockSpec((tm, d), lambda i, j: (i, 0)),
                  pl.BlockSpec((1, d), lambda i, j: (0, 0)),
                  pl.BlockSpec((d, tn), lambda i, j: (0, j)),
                  pl.BlockSpec(w_ab.shape, lambda i, j: (0, 0))],
        out_specs=[pl.BlockSpec((tm, tn), lambda i, j: (i, jnp.minimum(j, n_a - 1))),
                   pl.BlockSpec((tm, tn), lambda i, j: (i, jnp.maximum(j - n_a, 0))),
                   pl.BlockSpec((tm, w_ab.shape[1]), lambda i, j: (i, 0))],
        out_shape=[jax.ShapeDtypeStruct((n, c_a), BF16), jax.ShapeDtypeStruct((n, c - c_a), BF16),
                   jax.ShapeDtypeStruct((n, w_ab.shape[1]), F32)],
        scratch_shapes=[pltpu.VMEM((tm, d), BF16)],
        compiler_params=_params("arbitrary", "arbitrary"),
        name="in_proj",
    )(x, nw, w_cat, w_ab)


def _load_conv_window(x_ref, xbuf, ts):
    @pl.when(pl.program_id(1) == 0)
    def _():
        xbuf[0:SUBLANES, :] = jnp.zeros((SUBLANES, xbuf.shape[1]), F32)

    @pl.when(pl.program_id(1) != 0)
    def _():
        xbuf[0:SUBLANES, :] = xbuf[ts:ts + SUBLANES, :]

    xbuf[SUBLANES:SUBLANES + ts, :] = x_ref[...].astype(F32)


def _causal_conv(xbuf, cw_ref, ts, first_row=0, cols=slice(None)):
    base = first_row + SUBLANES - (CONV_TAPS - 1)
    acc = cw_ref[0:1, cols] * xbuf[base:base + ts, cols]
    for k in range(1, CONV_TAPS):
        acc = acc + cw_ref[k:k + 1, cols] * xbuf[base + k:base + k + ts, cols]
    return acc


def _rglru_kernel(rx_ref, rg_ref, cw_ref, cb_ref, wa_ref, ba_ref, wx_ref, bx_ref, lam_ref, o_ref,
                  xbuf, a_ref, b_ref, carry_ref):
    ts, c = rx_ref.shape
    _load_conv_window(rx_ref, xbuf, ts)

    @pl.when(pl.program_id(1) == 0)
    def _():
        carry_ref[...] = jnp.zeros_like(carry_ref)

    xc = _causal_conv(xbuf, cw_ref, ts) + cb_ref[...]
    neg_sp = -RG_POWER * _softplus(-lam_ref[...])
    for n in range(c // RNN_BLOCK):
        sl = slice(n * RNN_BLOCK, (n + 1) * RNN_BLOCK)
        xb = xc[:, sl]
        r = _sigmoid(_dot(xb, wa_ref[n]) + ba_ref[:, sl])
        i = _sigmoid(_dot(xb, wx_ref[n]) + bx_ref[:, sl])
        log_a = neg_sp[:, sl] * r
        a = jnp.exp(log_a)
        a_ref[:, sl] = a
        b_ref[:, sl] = jnp.sqrt(_one_minus_exp2(log_a, a)) * (i * xb)

    row = lax.broadcasted_iota(I32, (SUBLANES, c), 0)

    def slab(t, carry):
        rows = pl.ds(pl.multiple_of(t * SUBLANES, SUBLANES), SUBLANES)
        a = a_ref[rows, :]
        b = b_ref[rows, :]
        for d in (1, 2, 4):
            a_sh = jnp.where(row >= d, pltpu.roll(a, d, 0), 1.0)
            b_sh = jnp.where(row >= d, pltpu.roll(b, d, 0), 0.0)
            b = a * b_sh + b
            a = a * a_sh
        h = a * carry + b
        b_ref[rows, :] = h
        return h[SUBLANES - 1:SUBLANES, :]

    carry_ref[...] = lax.fori_loop(0, ts // SUBLANES, slab, carry_ref[...])
    o_ref[...] = (_gelu_tanh(rg_ref[...].astype(F32)) * b_ref[...]).astype(o_ref.dtype)


def _rglru(proj_a, bsz, seq, cw, cb, wa, ba, wx, bx, lam, ts):
    c = cw.shape[1]
    ns = seq // ts
    full = lambda shape: pl.BlockSpec(shape, lambda b, s: (0,) * len(shape))
    return pl.pallas_call(
        _rglru_kernel,
        grid=(bsz, ns),
        in_specs=[pl.BlockSpec((ts, c), lambda b, s: (b * ns + s, 0)),
                  pl.BlockSpec((ts, c), lambda b, s: (b * ns + s, 1)),
                  full(cw.shape), full(cb.shape), full(wa.shape), full(ba.shape),
                  full(wx.shape), full(bx.shape), full(lam.shape)],
        out_specs=pl.BlockSpec((ts, c), lambda b, s: (b * ns + s, 0)),
        out_shape=jax.ShapeDtypeStruct((bsz * seq, c), BF16),
        scratch_shapes=[pltpu.VMEM((ts + SUBLANES, c), F32), pltpu.VMEM((ts, c), F32),
                        pltpu.VMEM((ts, c), F32), pltpu.VMEM((1, c), F32)],
        compiler_params=_params("parallel", "arbitrary"),
        name="rglru",
    )(proj_a, proj_a, cw, cb, wa, ba, wx, bx, lam)


def _lane_bcast(x, lane):
    return jnp.broadcast_to(x[:, lane:lane + 1], x.shape)


def _unit_lower_inverses(neg_ls, eye, fillers):
    ts = eye.shape[0]
    run_filler = lambda: next(fillers, lambda: None)()
    ts_mats = [eye + n for n in neg_ls]
    ps = [_dot_split(_split(n), _split(n)) for n in neg_ls]
    run_filler()
    levels = int(math.log2(ts)) - 1
    for lvl in range(levels - 1):
        p_parts = [_split(p) for p in ps]
        lhs_parts = [tuple(jnp.concatenate([a, b], axis=0) for a, b in zip(_split(t), pp))
                     for t, pp in zip(ts_mats, p_parts)]
        both = [_dot_split(lp, pp) for lp, pp in zip(lhs_parts, p_parts)]
        ts_mats = [t + b[:ts] for t, b in zip(ts_mats, both)]
        ps = [b[ts:] for b in both]
        run_filler()
    return [t + _dot_split(_split(t), _split(p)) for t, p in zip(ts_mats, ps)]


def _gdn_kernel(qkv_ref, z_ref, ab_ref, abt_ref, cw_ref, acol_ref, dcol_ref, arow_ref, drow_ref, nw_ref,
                o_ref, xbuf, s_ref):
    ts = GDN_CHUNK
    n_heads = s_ref.shape[0]
    dn_w = n_heads * HEAD_DIM
    _load_conv_window(qkv_ref, xbuf, qkv_ref.shape[0])

    @pl.when(pl.program_id(1) == 0)
    def _():
        s_ref[...] = jnp.zeros_like(s_ref)

    row = lax.broadcasted_iota(I32, (ts, ts), 0)
    col = lax.broadcasted_iota(I32, (ts, ts), 1)
    lower_incl = (row >= col).astype(F32)
    upper_incl = (row <= col).astype(F32)
    eye = (row == col).astype(F32)
    head_cols = lambda h, part: slice(part * dn_w + h * HEAD_DIM, part * dn_w + (h + 1) * HEAD_DIM)
    heads = range(n_heads)

    n_chunks = qkv_ref.shape[0] // ts
    state = [dict(q=[None] * n_heads, k=[None] * n_heads, k_beta=[None] * n_heads, decay=[None] * n_heads,
                  rhs=[None] * n_heads, qd=[None] * n_heads, kd=[None] * n_heads) for _ in range(n_chunks)]

    def prepare_gates(c):
        st = state[c]
        rows = slice(c * ts, (c + 1) * ts)
        ab = ab_ref[rows, :]
        g_col = -acol_ref[...] * _softplus(ab + dcol_ref[...])
        st["beta"] = _sigmoid(ab)
        g_row = -arow_ref[...] * _softplus(abt_ref[:, rows] + drow_ref[...])
        st["cum_col"] = _dot_f32(lower_incl, g_col)
        st["cum_row"] = _dot_f32(g_row, upper_incl)
        last = st["cum_col"][ts - 1:ts, :]
        st["exp_cum"] = jnp.exp(st["cum_col"])
        st["exp_rem"] = jnp.exp(jnp.broadcast_to(last, (ts, LANES)) - st["cum_col"])
        st["exp_last"] = jnp.exp(last)

    def conv_silu(c, h, part):
        return _silu(_causal_conv(xbuf, cw_ref, ts, c * ts, head_cols(h, part)))

    def prepare_key(c, h):
        st = state[c]
        k = conv_silu(c, h, 1)
        k = k * lax.rsqrt(jnp.sum(k * k, axis=-1, keepdims=True) + NORM_EPS)
        st["k"][h], st["k_beta"][h] = k, k * _lane_bcast(st["beta"], n_heads + h)
        st["decay"][h] = jnp.exp(jnp.minimum(_lane_bcast(st["cum_col"], h) - st["cum_row"][h:h + 1, :], 0.0))

    def prepare_query_value(c, h):
        st = state[c]
        q, v = conv_silu(c, h, 0), conv_silu(c, h, 2)
        q = q * (lax.rsqrt(jnp.sum(q * q, axis=-1, keepdims=True) + NORM_EPS) * (HEAD_DIM ** -0.5))
        e_cum = _lane_bcast(st["exp_cum"], h)
        st["q"][h] = q
        st["rhs"][h] = jnp.concatenate([v * _lane_bcast(st["beta"], n_heads + h), st["k_beta"][h] * e_cum], axis=1)
        st["qd"][h] = q * e_cum
        st["kd"][h] = st["k"][h] * _lane_bcast(st["exp_rem"], h)

    def key_thunks(c):
        if c >= n_chunks:
            return []
        return [functools.partial(prepare_gates, c)] + [functools.partial(prepare_key, c, h) for h in heads]

    def solve(c):
        st = state[c]
        kks = [_dot_nt(st["k_beta"][h], st["k"][h]) for h in heads]
        neg_ls = [jnp.where(row > col, -(kks[h] * st["decay"][h]), 0.0) for h in heads]
        thunks = [functools.partial(prepare_query_value, c, h) for h in heads] + key_thunks(c + 1)
        slots = int(math.log2(ts))
        per_slot = -(-len(thunks) // slots)
        groups = [thunks[i:i + per_slot] for i in range(0, len(thunks), per_slot)]
        fillers = iter([functools.partial(lambda g: [t() for t in g], g) for g in groups])
        t_mats = _unit_lower_inverses(neg_ls, eye, fillers)
        for group in fillers:
            group()
        qks = [_dot_nt(st["q"][h], st["k"][h]) for h in heads]
        st["intra"] = [jnp.where(row >= col, qks[h] * st["decay"][h], 0.0) for h in heads]
        st["uw"] = [_dot(t_mats[h], st["rhs"][h]) for h in heads]

    def advance(c):
        st = state[c]
        rows = slice(c * ts, (c + 1) * ts)
        ws_qs = [_dot(jnp.concatenate([st["uw"][h][:, HEAD_DIM:], st["qd"][h]], axis=0), s_ref[h]) for h in heads]
        v_news = [st["uw"][h][:, :HEAD_DIM] - ws_qs[h][:ts] for h in heads]
        mixed = [_dot(jnp.concatenate([st["intra"][h], st["kd"][h].T], axis=0), v_news[h]) for h in heads]
        for h in heads:
            s_ref[h] = (s_ref[h] * _lane_bcast(jnp.broadcast_to(st["exp_last"], (HEAD_DIM, LANES)), h)
                        + mixed[h][ts:])
            o = _rmsnorm(ws_qs[h][ts:] + mixed[h][:ts], nw_ref[...]) * _silu(z_ref[rows, head_cols(h, 0)].astype(F32))
            o_ref[rows, head_cols(h, 0)] = o.astype(o_ref.dtype)

    for thunk in key_thunks(0):
        thunk()
    for c in range(n_chunks):
        solve(c)
        advance(c)


def _gdn(proj_b, ab, abt, bsz, seq, cw, acol, dcol, arow, drow, nw, n_heads):
    ts = GDN_CHUNK * GDN_CHUNKS_PER_STEP
    assert seq % ts == 0
    ns = seq // ts
    dn_w = n_heads * HEAD_DIM
    full = lambda shape: pl.BlockSpec(shape, lambda b, s: (0,) * len(shape))
    return pl.pallas_call(
        _gdn_kernel,
        grid=(bsz, ns),
        in_specs=[pl.BlockSpec((ts, 3 * dn_w), lambda b, s: (b * ns + s, 0)),
                  pl.BlockSpec((ts, dn_w), lambda b, s: (b * ns + s, 3)),
                  pl.BlockSpec((ts, LANES), lambda b, s: (b * ns + s, 0)),
                  pl.BlockSpec((2 * n_heads, ts), lambda b, s: (0, b * ns + s)),
                  full(cw.shape), full(acol.shape), full(dcol.shape), full(arow.shape), full(drow.shape),
                  full(nw.shape)],
        out_specs=pl.BlockSpec((ts, dn_w), lambda b, s: (b * ns + s, 0)),
        out_shape=jax.ShapeDtypeStruct((bsz * seq, dn_w), BF16),
        scratch_shapes=[pltpu.VMEM((ts + SUBLANES, 3 * dn_w), F32),
                        pltpu.VMEM((n_heads, HEAD_DIM, HEAD_DIM), F32)],
        compiler_params=_params("parallel", "arbitrary"),
        name="gdn",
    )(proj_b, proj_b, ab, abt, cw, acol, dcol, arow, drow, nw)


def _merge_kernel(x_ref, ya_ref, yb_ref, ga_ref, gb_ref, wa_ref, wb_ref, wo_ref, o_ref):
    y_a = jnp.dot(ya_ref[...], wa_ref[...], preferred_element_type=F32)
    y_b = jnp.dot(yb_ref[...], wb_ref[...], preferred_element_type=F32)
    m = _sigmoid(ga_ref[...].astype(F32)) * y_a + _sigmoid(gb_ref[...].astype(F32)) * y_b
    o_ref[...] = x_ref[...] + _dot(m, wo_ref[...])


def _merge(x, gated_a, gated_b, proj_b, w_a, w_b, w_o, tm):
    n, d = x.shape
    full = lambda shape: pl.BlockSpec(shape, lambda i: (0,) * len(shape))
    return pl.pallas_call(
        _merge_kernel,
        grid=(n // tm,),
        in_specs=[pl.BlockSpec((tm, d), lambda i: (i, 0)),
                  pl.BlockSpec((tm, gated_a.shape[1]), lambda i: (i, 0)),
                  pl.BlockSpec((tm, gated_b.shape[1]), lambda i: (i, 0)),
                  pl.BlockSpec((tm, d), lambda i: (i, 4)),
                  pl.BlockSpec((tm, d), lambda i: (i, 5)),
                  full(w_a.shape), full(w_b.shape), full(w_o.shape)],
        out_specs=pl.BlockSpec((tm, d), lambda i: (i, 0)),
        out_shape=jax.ShapeDtypeStruct((n, d), F32),
        compiler_params=_params("parallel"),
        name="merge",
    )(x, gated_a, gated_b, proj_b, proj_b, w_a, w_b, w_o)


def _cross_kernel(h_ref, nw_ref, wq_ref, kv_ref, wo_ref, o_ref):
    x = h_ref[...]
    d = x.shape[1]
    hd = d // CA_HEADS
    u = _rmsnorm(x, nw_ref[...])
    q = _dot(u, wq_ref[...])
    outs = []
    for h in range(CA_HEADS):
        k_h = kv_ref[:, h * hd:(h + 1) * hd]
        v_h = kv_ref[:, d + h * hd:d + (h + 1) * hd]
        s = _dot_nt(q[:, h * hd:(h + 1) * hd], k_h) * (hd ** -0.5)
        s = s - jnp.max(s, axis=-1, keepdims=True)
        e = jnp.exp(s)
        p = e / jnp.sum(e, axis=-1, keepdims=True)
        outs.append(_dot(p, v_h))
    o = jnp.concatenate(outs, axis=1)
    o_ref[...] = x + _dot(o, wo_ref[...])


def _cross(h, nw, w_q, kv, w_o, bsz, seq, n_mem, ts):
    n, d = h.shape
    ns = seq // ts
    full = lambda shape: pl.BlockSpec(shape, lambda b, s: (0,) * len(shape))
    return pl.pallas_call(
        _cross_kernel,
        grid=(bsz, ns),
        in_specs=[pl.BlockSpec((ts, d), lambda b, s: (b * ns + s, 0)),
                  full(nw.shape), full(w_q.shape),
                  pl.BlockSpec((n_mem, 2 * d), lambda b, s: (b, 0)),
                  full(w_o.shape)],
        out_specs=pl.BlockSpec((ts, d), lambda b, s: (b * ns + s, 0)),
        out_shape=jax.ShapeDtypeStruct((n, d), F32),
        compiler_params=_params("parallel", "parallel"),
        name="cross_attn",
    )(h, nw, w_q, kv, w_o)


def _router_kernel(h_ref, nw_ref, wr_ref, br_ref, u_ref, idx_ref, wt_ref, cnt_ref, base_ref):
    t = h_ref.shape[0]

    @pl.when(pl.program_id(0) == 0)
    def _():
        base_ref[...] = jnp.zeros_like(base_ref)

    u = _rmsnorm(h_ref[...], nw_ref[...])
    _to_row_tiles(u, u_ref)
    logits = _dot_split(_split(u), _split(wr_ref[...])) + br_ref[...]
    lane = lax.broadcasted_iota(I32, (t, LANES), 1)
    lanef = lane.astype(F32)
    big = float(LANES)
    neg = -jnp.inf

    lg = jnp.where((lane >= N_EXPERTS) & (lane < N_EXPERTS + N_GROUPS), logits, neg)
    gmax = jnp.max(lg, axis=-1, keepdims=True)
    grp = jnp.min(jnp.where(lg == gmax, lanef - float(N_EXPERTS), big), axis=-1, keepdims=True)
    p_grp = 1.0 / jnp.sum(jnp.exp(lg - gmax), axis=-1, keepdims=True)

    in_grp = (lane < N_EXPERTS) & ((lane // GROUP_SIZE).astype(F32) == grp)
    le = jnp.where(in_grp, logits, neg)
    m1 = jnp.max(le, axis=-1, keepdims=True)
    i1 = jnp.min(jnp.where(le == m1, lanef, big), axis=-1, keepdims=True)
    le2 = jnp.where(lanef == i1, neg, le)
    m2 = jnp.max(le2, axis=-1, keepdims=True)
    i2 = jnp.min(jnp.where(le2 == m2, lanef, big), axis=-1, keepdims=True)
    ratio = jnp.exp(m2 - m1)
    p1 = 1.0 / (1.0 + ratio)
    p2 = ratio * p1

    oh1 = lanef == i1
    oh2 = lanef == i2
    onehot = jnp.where(oh1 | oh2, 1.0, 0.0)
    r_i = lax.broadcasted_iota(I32, (t, t), 0)
    c_i = lax.broadcasted_iota(I32, (t, t), 1)
    before = jnp.where(r_i > c_i, 1.0, 0.0)
    rank = _dot(before, onehot) + base_ref[...]
    r1 = jnp.sum(jnp.where(oh1, rank, 0.0), axis=-1, keepdims=True)
    r2 = jnp.sum(jnp.where(oh2, rank, 0.0), axis=-1, keepdims=True)
    base_ref[...] = base_ref[...] + jnp.sum(onehot, axis=0, keepdims=True)

    idx = jnp.where(lane == 0, i1, jnp.where(lane == 1, i2, jnp.where(lane == 2, r1, jnp.where(lane == 3, r2, 0.0))))
    idx_ref[...] = idx.astype(I32)
    wt_ref[...] = jnp.where(lane == 0, p_grp * p1, jnp.where(lane == 1, p_grp * p2, 0.0))
    cnt_ref[...] = jnp.broadcast_to(base_ref[...], cnt_ref.shape)


def _router(h, nw, wr, br, t):
    n, d = h.shape
    full = lambda shape: pl.BlockSpec(shape, lambda i: (0,) * len(shape))
    return pl.pallas_call(
        _router_kernel,
        grid=(n // t,),
        in_specs=[pl.BlockSpec((t, d), lambda i: (i, 0)), full(nw.shape), full(wr.shape), full(br.shape)],
        out_specs=[pl.BlockSpec((t, d // LANES, LANES), lambda i: (i, 0, 0)),
                   pl.BlockSpec((t, LANES), lambda i: (i, 0)),
                   pl.BlockSpec((t, LANES), lambda i: (i, 0)),
                   pl.BlockSpec((SUBLANES, LANES), lambda i: (0, 0))],
        out_shape=[jax.ShapeDtypeStruct((n, d // LANES, LANES), F32),
                   jax.ShapeDtypeStruct((n, LANES), I32),
                   jax.ShapeDtypeStruct((n, LANES), F32),
                   jax.ShapeDtypeStruct((SUBLANES, LANES), F32)],
        scratch_shapes=[pltpu.VMEM((1, LANES), F32)],
        compiler_params=_params("arbitrary"),
        name="router",
    )(h, nw, wr, br)


def _row_copy(src, src_row, dst, dst_row, sem):
    return pltpu.make_async_copy(src.at[pl.ds(src_row, 1)], dst.at[pl.ds(dst_row, 1)], sem)


def _to_row_tiles(x, tiles_ref):
    for c in range(tiles_ref.shape[1]):
        tiles_ref[:, c, :] = x[:, c * LANES:(c + 1) * LANES]


def _from_row_tiles(tiles_ref):
    return jnp.concatenate([tiles_ref[:, c, :] for c in range(tiles_ref.shape[1])], axis=1)


def _dispatch_kernel(dest_ref, u_ref, xs_in_ref, xs_ref, sem):
    del xs_in_ref
    t = u_ref.shape[0]

    def copy(r, slot):
        return pltpu.make_async_copy(u_ref.at[r], xs_ref.at[dest_ref[0, 0, 2 * r + slot]], sem)

    def start(r, carry):
        copy(r, 0).start()
        copy(r, 1).start()
        return carry

    def wait(r, carry):
        copy(r, 0).wait()
        copy(r, 1).wait()
        return carry

    lax.fori_loop(0, t, start, 0, unroll=DMA_UNROLL)
    lax.fori_loop(0, t, wait, 0, unroll=DMA_UNROLL)


def _dispatch(dest3, u_tiles, xs_zero, t):
    n = u_tiles.shape[0]
    return pl.pallas_call(
        _dispatch_kernel,
        grid=(n // t,),
        in_specs=[pl.BlockSpec((1, 1, 2 * t), lambda i: (i, 0, 0), memory_space=pltpu.SMEM),
                  pl.BlockSpec((t,) + u_tiles.shape[1:], lambda i: (i, 0, 0)),
                  pl.BlockSpec(memory_space=pl.ANY)],
        out_specs=pl.BlockSpec(memory_space=pl.ANY),
        out_shape=jax.ShapeDtypeStruct(xs_zero.shape, xs_zero.dtype),
        scratch_shapes=[pltpu.SemaphoreType.DMA(())],
        input_output_aliases={2: 0},
        compiler_params=_params("arbitrary"),
        name="moe_dispatch",
    )(dest3, u_tiles, xs_zero)


def _expert_kernel(be_ref, nu_ref, xs_ref, wg_ref, wu_ref, wd_ref, y_ref, wg_bf, wu_bf, wd_bf):
    j = pl.program_id(0)
    used = j < nu_ref[0]
    new_expert = (j == 0) | (be_ref[j] != be_ref[jnp.maximum(j - 1, 0)])

    @pl.when(used & new_expert)
    def _():
        wg_bf[...] = wg_ref[0].astype(BF16)
        wu_bf[...] = wu_ref[0].astype(BF16)
        wd_bf[...] = wd_ref[0].astype(BF16)

    @pl.when(used)
    def _():
        x = _from_row_tiles(xs_ref).astype(BF16)
        hid = _silu(jnp.dot(x, wg_bf[...], preferred_element_type=F32)) * jnp.dot(
            x, wu_bf[...], preferred_element_type=F32)
        y_ref[...] = _dot(hid, wd_bf[...])

    @pl.when(jnp.logical_not(used))
    def _():
        y_ref[...] = jnp.zeros_like(y_ref)


def _experts(block_e, n_used, xs, w_gate, w_up, w_down):
    rows = xs.shape[0]
    d, de = w_gate.shape[1], w_gate.shape[2]
    grid_spec = pltpu.PrefetchScalarGridSpec(
        num_scalar_prefetch=2,
        grid=(rows // MOE_BLOCK,),
        in_specs=[pl.BlockSpec((MOE_BLOCK,) + xs.shape[1:], lambda j, be, nu: (j, 0, 0)),
                  pl.BlockSpec((1, d, de), lambda j, be, nu: (be[j], 0, 0)),
                  pl.BlockSpec((1, d, de), lambda j, be, nu: (be[j], 0, 0)),
                  pl.BlockSpec((1, de, d), lambda j, be, nu: (be[j], 0, 0))],
        out_specs=pl.BlockSpec((MOE_BLOCK, d), lambda j, be, nu: (j, 0)),
        scratch_shapes=[pltpu.VMEM((d, de), BF16), pltpu.VMEM((d, de), BF16), pltpu.VMEM((de, d), BF16)],
    )
    return pl.pallas_call(
        _expert_kernel,
        grid_spec=grid_spec,
        out_shape=jax.ShapeDtypeStruct((rows, d), F32),
        compiler_params=_params("arbitrary"),
        name="moe_experts",
    )(block_e, n_used, xs, w_gate, w_up, w_down)


def _combine_kernel(dest_ref, h_ref, wt_ref, nw_ref, y_ref, o_ref, buf, sem, *, final_norm):
    t = h_ref.shape[0]

    def start(r, carry):
        _row_copy(y_ref, dest_ref[0, 0, 2 * r], buf.at[0], r, sem).start()
        _row_copy(y_ref, dest_ref[0, 0, 2 * r + 1], buf.at[1], r, sem).start()
        return carry

    def wait(r, carry):
        _row_copy(y_ref, dest_ref[0, 0, 2 * r], buf.at[0], r, sem).wait()
        _row_copy(y_ref, dest_ref[0, 0, 2 * r + 1], buf.at[1], r, sem).wait()
        return carry

    lax.fori_loop(0, t, start, 0, unroll=DMA_UNROLL)
    lax.fori_loop(0, t, wait, 0, unroll=DMA_UNROLL)
    wt = wt_ref[...]
    moe = buf[0] * wt[:, 0:1] + buf[1] * wt[:, 1:2]
    out = h_ref[...] + moe
    o_ref[...] = _rmsnorm(out, nw_ref[...]) if final_norm else out


def _combine(dest3, h, wt, nw, y, t, final_norm):
    n, d = h.shape
    return pl.pallas_call(
        functools.partial(_combine_kernel, final_norm=final_norm),
        grid=(n // t,),
        in_specs=[pl.BlockSpec((1, 1, 2 * t), lambda i: (i, 0, 0), memory_space=pltpu.SMEM),
                  pl.BlockSpec((t, d), lambda i: (i, 0)),
                  pl.BlockSpec((t, LANES), lambda i: (i, 0)),
                  pl.BlockSpec((1, d), lambda i: (0, 0)),
                  pl.BlockSpec(memory_space=pl.ANY)],
        out_specs=pl.BlockSpec((t, d), lambda i: (i, 0)),
        out_shape=jax.ShapeDtypeStruct((n, d), F32),
        scratch_shapes=[pltpu.VMEM((2, t, d), F32), pltpu.SemaphoreType.DMA(())],
        compiler_params=_params("arbitrary"),
        name="moe_combine",
    )(dest3, h, wt, nw, y)


def _tile(n, pref):
    return pref if n % pref == 0 else n


def kernel(x, mem, norm1_w, w_in, rnn_conv_w, rnn_conv_b, rglru_wa, rglru_ba, rglru_wx, rglru_bx, rglru_lambda, w_branch_a, dn_conv_w, dn_a_log, dn_dt_bias, dn_norm_w, w_branch_b, w_out, norm2_w, mem_norm_w, w_cq, w_ckv, w_co, norm3_w, w_router_group, b_router_group, w_router_expert, b_router_expert, w_exp_gate, w_exp_up, w_exp_down, norm_f_w):
    bsz, seq, d = x.shape
    n = bsz * seq
    n_mem = mem.shape[1]
    depth = w_in.shape[0]
    d_rnn = rnn_conv_w.shape[2]
    n_heads = dn_a_log.shape[1]
    dn_w = n_heads * HEAD_DIM
    row = lambda v: v.reshape(1, -1).astype(F32)

    h = x.reshape(n, d)
    mem2 = mem.reshape(bsz * n_mem, d)
    tm = _tile(n, 2048)
    for l in range(depth):
        o_rg, o_qkv, o_z = d_rnn, 2 * d_rnn, 2 * d_rnn + 3 * dn_w
        o_a = o_z + dn_w
        o_ga = o_a + 2 * n_heads
        wi = w_in[l]
        w_cat = jnp.concatenate([wi[:, :o_a], wi[:, o_ga:]], axis=1).astype(BF16)
        w_ab = jnp.pad(wi[:, o_a:o_ga], ((0, 0), (0, LANES - 2 * n_heads))).astype(BF16)
        proj_a, proj_b, ab = _in_proj(h, row(norm1_w[l]), w_cat, w_ab, o_qkv, tm, 512)

        gated_a = _rglru(proj_a, bsz, seq, rnn_conv_w[l], row(rnn_conv_b[l]), rglru_wa[l].astype(BF16),
                         row(rglru_ba[l]), rglru_wx[l].astype(BF16), row(rglru_bx[l]), row(rglru_lambda[l]),
                         _tile(seq, 512))

        a_dec = jnp.exp(dn_a_log[l].astype(F32))
        pad_h = lambda v: jnp.pad(v, (0, LANES - n_heads))
        acol, dcol = row(pad_h(a_dec)), row(pad_h(dn_dt_bias[l]))
        arow = jnp.broadcast_to(jnp.pad(a_dec, (0, n_heads))[:, None], (2 * n_heads, GDN_CHUNK))
        drow = jnp.broadcast_to(jnp.pad(dn_dt_bias[l], (0, n_heads))[:, None], (2 * n_heads, GDN_CHUNK))
        abt = ab[:, :2 * n_heads].T
        gated_b = _gdn(proj_b, ab, abt, bsz, seq, dn_conv_w[l], acol, dcol, arow, drow, row(dn_norm_w[l]), n_heads)

        h = _merge(h, gated_a, gated_b, proj_b, w_branch_a[l].astype(BF16), w_branch_b[l].astype(BF16),
                   w_out[l].astype(BF16), _tile(n, 512))

        kv = _norm_mm(mem2, row(mem_norm_w[l]), w_ckv[l].astype(BF16), BF16, _tile(bsz * n_mem, 1024), 512)
        h = _cross(h, row(norm2_w[l]), w_cq[l].astype(BF16), kv, w_co[l].astype(BF16), bsz, seq, n_mem,
                   _tile(seq, 512))

        w_r = jnp.pad(jnp.concatenate([w_router_expert[l], w_router_group[l]], axis=1),
                      ((0, 0), (0, LANES - N_EXPERTS - N_GROUPS)))
        b_r = row(jnp.pad(jnp.concatenate([b_router_expert[l], b_router_group[l]]), (0, LANES - N_EXPERTS - N_GROUPS)))
        u3, idx, wt, cnt = _router(h, row(norm3_w[l]), w_r, b_r, _tile(n, 512))

        counts = cnt[0, :N_EXPERTS].astype(I32)
        padded = (counts + MOE_BLOCK - 1) // MOE_BLOCK * MOE_BLOCK
        pend = jnp.cumsum(padded)
        pstart = pend - padded
        is_expert = idx[:, 0:2, None] == jnp.arange(N_EXPERTS, dtype=I32)
        dest = jnp.sum(jnp.where(is_expert, pstart, 0), axis=-1) + idx[:, 2:4]
        n_blocks = (2 * n + N_EXPERTS * (MOE_BLOCK - 1)) // MOE_BLOCK
        block_row = jnp.arange(n_blocks, dtype=I32) * MOE_BLOCK
        block_e = jnp.minimum(jnp.sum((pend[None, :] <= block_row[:, None]).astype(I32), axis=1), N_EXPERTS - 1)
        n_used = (pend[-1:] // MOE_BLOCK).astype(I32)
        t_moe = _tile(n, 512)
        dest3 = dest.reshape(n // t_moe, 1, 2 * t_moe)
        xs = _dispatch(dest3, u3, jnp.zeros((n_blocks * MOE_BLOCK, SUBLANES, LANES), F32), t_moe)
        yb = _experts(block_e, n_used, xs, w_exp_gate[l], w_exp_up[l], w_exp_down[l])
        h = _combine(dest3, h, wt, row(norm_f_w), yb, t_moe, final_norm=(l == depth - 1))
    return h.reshape(bsz, seq, d)
```

```python
import functools
import math

import jax
import jax.numpy as jnp
from jax import lax
from jax.experimental import pallas as pl
from jax.experimental.pallas import tpu as pltpu

F32 = jnp.float32
BF16 = jnp.bfloat16
I32 = jnp.int32
HIGHEST = lax.Precision.HIGHEST

NORM_EPS = 1e-6
CONV_TAPS = 4
RNN_BLOCK = 128
RG_POWER = 8.0
HEAD_DIM = 128
GDN_CHUNK = 128
GDN_CHUNKS_PER_STEP = 2
CA_HEADS = 4
N_GROUPS = 8
GROUP_SIZE = 8
N_EXPERTS = N_GROUPS * GROUP_SIZE
MOE_BLOCK = 512
LANES = 128
SUBLANES = 8
DMA_UNROLL = 8
VMEM_LIMIT = 48 * 1024 * 1024


def _params(*semantics):
    return pltpu.CompilerParams(dimension_semantics=semantics, vmem_limit_bytes=VMEM_LIMIT)


def _dot(a, b):
    return jnp.dot(a.astype(BF16), b.astype(BF16), preferred_element_type=F32)


def _dot_nt(a, b):
    return lax.dot_general(a.astype(BF16), b.astype(BF16), (((1,), (1,)), ((), ())),
                           preferred_element_type=F32)


def _dot_tn(a, b):
    return lax.dot_general(a.astype(BF16), b.astype(BF16), (((0,), (0,)), ((), ())),
                           preferred_element_type=F32)


def _dot_f32(a, b):
    return jnp.dot(a, b, precision=HIGHEST, preferred_element_type=F32)


def _split(x):
    hi = x.astype(BF16)
    return hi, (x - hi.astype(F32)).astype(BF16)


def _dot_split(a_parts, b_parts):
    (a_hi, a_lo), (b_hi, b_lo) = a_parts, b_parts
    dot = functools.partial(jnp.dot, preferred_element_type=F32)
    return dot(a_hi, b_hi) + (dot(a_hi, b_lo) + dot(a_lo, b_hi))


def _rmsnorm(x, w):
    return x * lax.rsqrt(jnp.mean(x * x, axis=-1, keepdims=True) + NORM_EPS) * w


def _sigmoid(x):
    return 0.5 * jnp.tanh(0.5 * x) + 0.5


def _silu(x):
    return x * _sigmoid(x)


def _softplus(x):
    return jnp.maximum(x, 0.0) + jnp.log(1.0 + jnp.exp(-jnp.abs(x)))


def _one_minus_exp2(y, exp_y):
    return jnp.tanh(-y) * (1.0 + exp_y * exp_y)


def _gelu_tanh(x):
    return 0.5 * x * (1.0 + jnp.tanh(math.sqrt(2.0 / math.pi) * (x + 0.044715 * (x * x * x))))


def _norm_mm_kernel(x_ref, nw_ref, w_ref, o_ref, u_ref):
    @pl.when(pl.program_id(1) == 0)
    def _():
        u_ref[...] = _rmsnorm(x_ref[...], nw_ref[...]).astype(BF16)

    o_ref[...] = jnp.dot(u_ref[...], w_ref[...], preferred_element_type=F32).astype(o_ref.dtype)


def _norm_mm(x, nw, w, out_dtype, tm, tn):
    n, d = x.shape
    c = w.shape[1]
    return pl.pallas_call(
        _norm_mm_kernel,
        grid=(n // tm, c // tn),
        in_specs=[pl.BlockSpec((tm, d), lambda i, j: (i, 0)),
                  pl.BlockSpec((1, d), lambda i, j: (0, 0)),
                  pl.BlockSpec((d, tn), lambda i, j: (0, j))],
        out_specs=pl.BlockSpec((tm, tn), lambda i, j: (i, j)),
        out_shape=jax.ShapeDtypeStruct((n, c), out_dtype),
        scratch_shapes=[pltpu.VMEM((tm, d), BF16)],
        compiler_params=_params("parallel", "arbitrary"),
        name="norm_mm",
    )(x, nw, w)


def _in_proj_kernel(x_ref, nw_ref, w_ref, wab_ref, oa_ref, ob_ref, ab_ref, u_ref, *, n_a):
    j = pl.program_id(1)

    @pl.when(j == 0)
    def _():
        u_ref[...] = _rmsnorm(x_ref[...], nw_ref[...]).astype(BF16)
        ab_ref[...] = jnp.dot(u_ref[...], wab_ref[...], preferred_element_type=F32)

    @pl.when(j < n_a)
    def _():
        oa_ref[...] = jnp.dot(u_ref[...], w_ref[...], preferred_element_type=F32).astype(oa_ref.dtype)

    @pl.when(j >= n_a)
    def _():
        ob_ref[...] = jnp.dot(u_ref[...], w_ref[...], preferred_element_type=F32).astype(ob_ref.dtype)


def _in_proj(x, nw, w_cat, w_ab, c_a, tm, tn):
    n, d = x.shape
    c = w_cat.shape[1]
    n_a, n_b = c_a // tn, (c - c_a) // tn
    return pl.pallas_call(
        functools.partial(_in_proj_kernel, n_a=n_a),
        grid=(n // tm, n_a + n_b),
        in_specs=[pl.BlockSpec((tm, d), lambda i, j: (i, 0)),
                  pl.BlockSpec((1, d), lambda i, j: (0, 0)),
                  pl.BlockSpec((d, tn), lambda i, j: (0, j)),
                  pl.BlockSpec(w_ab.shape, lambda i, j: (0, 0))],
        out_specs=[pl.BlockSpec((tm, tn), lambda i, j: (i, jnp.minimum(j, n_a - 1))),
                   pl.BlockSpec((tm, tn), lambda i, j: (i, jnp.maximum(j - n_a, 0))),
                   pl.BlockSpec((tm, w_ab.shape[1]), lambda i, j: (i, 0))],
        out_shape=[jax.ShapeDtypeStruct((n, c_a), BF16), jax.ShapeDtypeStruct((n, c - c_a), BF16),
                   jax.ShapeDtypeStruct((n, w_ab.shape[1]), F32)],
        scratch_shapes=[pltpu.VMEM((tm, d), BF16)],
        compiler_params=_params("arbitrary", "arbitrary"),
        name="in_proj",
    )(x, nw, w_cat, w_ab)


def _load_conv_window(x_ref, xbuf, ts):
    @pl.when(pl.program_id(1) == 0)
    def _():
        xbuf[0:SUBLANES, :] = jnp.zeros((SUBLANES, xbuf.shape[1]), F32)

    @pl.when(pl.program_id(1) != 0)
    def _():
        xbuf[0:SUBLANES, :] = xbuf[ts:ts + SUBLANES, :]

    xbuf[SUBLANES:SUBLANES + ts, :] = x_ref[...].astype(F32)


def _causal_conv(xbuf, cw_ref, ts, first_row=0, cols=slice(None)):
    base = first_row + SUBLANES - (CONV_TAPS - 1)
    acc = cw_ref[0:1, cols] * xbuf[base:base + ts, cols]
    for k in range(1, CONV_TAPS):
        acc = acc + cw_ref[k:k + 1, cols] * xbuf[base + k:base + k + ts, cols]
    return acc


def _rglru_kernel(rx_ref, rg_ref, cw_ref, cb_ref, wa_ref, ba_ref, wx_ref, bx_ref, lam_ref, o_ref,
                  xbuf, a_ref, b_ref, carry_ref):
    ts, c = rx_ref.shape
    _load_conv_window(rx_ref, xbuf, ts)

    @pl.when(pl.program_id(1) == 0)
    def _():
        carry_ref[...] = jnp.zeros_like(carry_ref)

    xc = _causal_conv(xbuf, cw_ref, ts) + cb_ref[...]
    neg_sp = -RG_POWER * _softplus(-lam_ref[...])
    for n in range(c // RNN_BLOCK):
        sl = slice(n * RNN_BLOCK, (n + 1) * RNN_BLOCK)
        xb = xc[:, sl]
        r = _sigmoid(_dot(xb, wa_ref[n]) + ba_ref[:, sl])
        i = _sigmoid(_dot(xb, wx_ref[n]) + bx_ref[:, sl])
        log_a = neg_sp[:, sl] * r
        a = jnp.exp(log_a)
        a_ref[:, sl] = a
        b_ref[:, sl] = jnp.sqrt(_one_minus_exp2(log_a, a)) * (i * xb)

    row = lax.broadcasted_iota(I32, (SUBLANES, c), 0)

    def slab(t, carry):
        rows = pl.ds(pl.multiple_of(t * SUBLANES, SUBLANES), SUBLANES)
        a = a_ref[rows, :]
        b = b_ref[rows, :]
        for d in (1, 2, 4):
            a_sh = jnp.where(row >= d, pltpu.roll(a, d, 0), 1.0)
            b_sh = jnp.where(row >= d, pltpu.roll(b, d, 0), 0.0)
            b = a * b_sh + b
            a = a * a_sh
        h = a * carry + b
        b_ref[rows, :] = h
        return h[SUBLANES - 1:SUBLANES, :]

    carry_ref[...] = lax.fori_loop(0, ts // SUBLANES, slab, carry_ref[...])
    o_ref[...] = (_gelu_tanh(rg_ref[...].astype(F32)) * b_ref[...]).astype(o_ref.dtype)


def _rglru(proj_a, bsz, seq, cw, cb, wa, ba, wx, bx, lam, ts):
    c = cw.shape[1]
    ns = seq // ts
    full = lambda shape: pl.BlockSpec(shape, lambda b, s: (0,) * len(shape))
    return pl.pallas_call(
        _rglru_kernel,
        grid=(bsz, ns),
        in_specs=[pl.BlockSpec((ts, c), lambda b, s: (b * ns + s, 0)),
                  pl.BlockSpec((ts, c), lambda b, s: (b * ns + s, 1)),
                  full(cw.shape), full(cb.shape), full(wa.shape), full(ba.shape),
                  full(wx.shape), full(bx.shape), full(lam.shape)],
        out_specs=pl.BlockSpec((ts, c), lambda b, s: (b * ns + s, 0)),
        out_shape=jax.ShapeDtypeStruct((bsz * seq, c), BF16),
        scratch_shapes=[pltpu.VMEM((ts + SUBLANES, c), F32), pltpu.VMEM((ts, c), F32),
                        pltpu.VMEM((ts, c), F32), pltpu.VMEM((1, c), F32)],
        compiler_params=_params("parallel", "arbitrary"),
        name="rglru",
    )(proj_a, proj_a, cw, cb, wa, ba, wx, bx, lam)


def _lane_bcast(x, lane):
    return jnp.broadcast_to(x[:, lane:lane + 1], x.shape)


def _unit_lower_inverses(neg_ls, eye, fillers):
    ts = eye.shape[0]
    run_filler = lambda: next(fillers, lambda: None)()
    ts_mats = [eye + n for n in neg_ls]
    ps = [_dot_split(_split(n), _split(n)) for n in neg_ls]
    run_filler()
    levels = int(math.log2(ts)) - 1
    for lvl in range(levels - 1):
        p_parts = [_split(p) for p in ps]
        lhs_parts = [tuple(jnp.concatenate([a, b], axis=0) for a, b in zip(_split(t), pp))
                     for t, pp in zip(ts_mats, p_parts)]
        both = [_dot_split(lp, pp) for lp, pp in zip(lhs_parts, p_parts)]
        ts_mats = [t + b[:ts] for t, b in zip(ts_mats, both)]
        ps = [b[ts:] for b in both]
        run_filler()
    return [t + _dot_split(_split(t), _split(p)) for t, p in zip(ts_mats, ps)]


def _gdn_kernel(qkv_ref, z_ref, ab_ref, abt_ref, cw_ref, acol_ref, dcol_ref, arow_ref, drow_ref, nw_ref,
                o_ref, xbuf, s_ref):
    ts = GDN_CHUNK
    n_heads = s_ref.shape[0]
    dn_w = n_heads * HEAD_DIM
    _load_conv_window(qkv_ref, xbuf, qkv_ref.shape[0])

    @pl.when(pl.program_id(1) == 0)
    def _():
        s_ref[...] = jnp.zeros_like(s_ref)

    row = lax.broadcasted_iota(I32, (ts, ts), 0)
    col = lax.broadcasted_iota(I32, (ts, ts), 1)
    lower_incl = (row >= col).astype(F32)
    upper_incl = (row <= col).astype(F32)
    eye = (row == col).astype(F32)
    head_cols = lambda h, part: slice(part * dn_w + h * HEAD_DIM, part * dn_w + (h + 1) * HEAD_DIM)
    heads = range(n_heads)

    n_chunks = qkv_ref.shape[0] // ts
    state = [dict(q=[None] * n_heads, k=[None] * n_heads, k_beta=[None] * n_heads, decay=[None] * n_heads,
                  rhs=[None] * n_heads, qd=[None] * n_heads, kd=[None] * n_heads) for _ in range(n_chunks)]

    def prepare_gates(c):
        st = state[c]
        rows = slice(c * ts, (c + 1) * ts)
        ab = ab_ref[rows, :]
        g_col = -acol_ref[...] * _softplus(ab + dcol_ref[...])
        st["beta"] = _sigmoid(ab)
        g_row = -arow_ref[...] * _softplus(abt_ref[:, rows] + drow_ref[...])
        st["cum_col"] = _dot_f32(lower_incl, g_col)
        st["cum_row"] = _dot_f32(g_row, upper_incl)
        last = st["cum_col"][ts - 1:ts, :]
        st["exp_cum"] = jnp.exp(st["cum_col"])
        st["exp_rem"] = jnp.exp(jnp.broadcast_to(last, (ts, LANES)) - st["cum_col"])
        st["exp_last"] = jnp.exp(last)

    def conv_silu(c, h, part):
        return _silu(_causal_conv(xbuf, cw_ref, ts, c * ts, head_cols(h, part)))

    def prepare_key(c, h):
        st = state[c]
        k = conv_silu(c, h, 1)
        k = k * lax.rsqrt(jnp.sum(k * k, axis=-1, keepdims=True) + NORM_EPS)
        st["k"][h], st["k_beta"][h] = k, k * _lane_bcast(st["beta"], n_heads + h)
        st["decay"][h] = jnp.exp(jnp.minimum(_lane_bcast(st["cum_col"], h) - st["cum_row"][h:h + 1, :], 0.0))

    def prepare_query_value(c, h):
        st = state[c]
        q, v = conv_silu(c, h, 0), conv_silu(c, h, 2)
        q = q * (lax.rsqrt(jnp.sum(q * q, axis=-1, keepdims=True) + NORM_EPS) * (HEAD_DIM ** -0.5))
        e_cum = _lane_bcast(st["exp_cum"], h)
        st["q"][h] = q
        st["rhs"][h] = jnp.concatenate([v * _lane_bcast(st["beta"], n_heads + h), st["k_beta"][h] * e_cum], axis=1)
        st["qd"][h] = q * e_cum
        st["kd"][h] = st["k"][h] * _lane_bcast(st["exp_rem"], h)

    def key_thunks(c):
        if c >= n_chunks:
            return []
        return [functools.partial(prepare_gates, c)] + [functools.partial(prepare_key, c, h) for h in heads]

    def solve(c):
        st = state[c]
        kks = [_dot_nt(st["k_beta"][h], st["k"][h]) for h in heads]
        neg_ls = [jnp.where(row > col, -(kks[h] * st["decay"][h]), 0.0) for h in heads]
        thunks = [functools.partial(prepare_query_value, c, h) for h in heads] + key_thunks(c + 1)
        slots = int(math.log2(ts))
        per_slot = -(-len(thunks) // slots)
        groups = [thunks[i:i + per_slot] for i in range(0, len(thunks), per_slot)]
        fillers = iter([functools.partial(lambda g: [t() for t in g], g) for g in groups])
        t_mats = _unit_lower_inverses(neg_ls, eye, fillers)
        for group in fillers:
            group()
        qks = [_dot_nt(st["q"][h], st["k"][h]) for h in heads]
        st["intra"] = [jnp.where(row >= col, qks[h] * st["decay"][h], 0.0) for h in heads]
        st["uw"] = [_dot(t_mats[h], st["rhs"][h]) for h in heads]

    def advance(c):
        st = state[c]
        rows = slice(c * ts, (c + 1) * ts)
        ws_qs = [_dot(jnp.concatenate([st["uw"][h][:, HEAD_DIM:], st["qd"][h]], axis=0), s_ref[h]) for h in heads]
        v_news = [st["uw"][h][:, :HEAD_DIM] - ws_qs[h][:ts] for h in heads]
        mixed = [_dot(jnp.concatenate([st["intra"][h], st["kd"][h].T], axis=0), v_news[h]) for h in heads]
        for h in heads:
            s_ref[h] = (s_ref[h] * _lane_bcast(jnp.broadcast_to(st["exp_last"], (HEAD_DIM, LANES)), h)
                        + mixed[h][ts:])
            o = _rmsnorm(ws_qs[h][ts:] + mixed[h][:ts], nw_ref[...]) * _silu(z_ref[rows, head_cols(h, 0)].astype(F32))
            o_ref[rows, head_cols(h, 0)] = o.astype(o_ref.dtype)

    for thunk in key_thunks(0):
        thunk()
    for c in range(n_chunks):
        solve(c)
        advance(c)


def _gdn(proj_b, ab, abt, bsz, seq, cw, acol, dcol, arow, drow, nw, n_heads):
    ts = GDN_CHUNK * GDN_CHUNKS_PER_STEP
    assert seq % ts == 0
    ns = seq // ts
    dn_w = n_heads * HEAD_DIM
    full = lambda shape: pl.BlockSpec(shape, lambda b, s: (0,) * len(shape))
    return pl.pallas_call(
        _gdn_kernel,
        grid=(bsz, ns),
        in_specs=[pl.BlockSpec((ts, 3 * dn_w), lambda b, s: (b * ns + s, 0)),
                  pl.BlockSpec((ts, dn_w), lambda b, s: (b * ns + s, 3)),
                  pl.BlockSpec((ts, LANES), lambda b, s: (b * ns + s, 0)),
                  pl.BlockSpec((2 * n_heads, ts), lambda b, s: (0, b * ns + s)),
                  full(cw.shape), full(acol.shape), full(dcol.shape), full(arow.shape), full(drow.shape),
                  full(nw.shape)],
        out_specs=pl.BlockSpec((ts, dn_w), lambda b, s: (b * ns + s, 0)),
        out_shape=jax.ShapeDtypeStruct((bsz * seq, dn_w), BF16),
        scratch_shapes=[pltpu.VMEM((ts + SUBLANES, 3 * dn_w), F32),
                        pltpu.VMEM((n_heads, HEAD_DIM, HEAD_DIM), F32)],
        compiler_params=_params("parallel", "arbitrary"),
        name="gdn",
    )(proj_b, proj_b, ab, abt, cw, acol, dcol, arow, drow, nw)


def _merge_cross_kernel(x_ref, ya_ref, yb_ref, ga_ref, gb_ref, wa_ref, wb_ref, wo_ref,
                        nw_ref, wq_ref, kv_ref, wco_ref, o_ref):
    y_a = jnp.dot(ya_ref[...], wa_ref[...], preferred_element_type=F32)
    y_b = jnp.dot(yb_ref[...], wb_ref[...], preferred_element_type=F32)
    m = _sigmoid(ga_ref[...].astype(F32)) * y_a + _sigmoid(gb_ref[...].astype(F32)) * y_b
    x = x_ref[...] + _dot(m, wo_ref[...])

    d = x.shape[1]
    hd = d // CA_HEADS
    q = _dot(_rmsnorm(x, nw_ref[...]), wq_ref[...])
    outs = []
    for h in range(CA_HEADS):
        k_h = kv_ref[:, h * hd:(h + 1) * hd]
        v_h = kv_ref[:, d + h * hd:d + (h + 1) * hd]
        s = _dot_nt(q[:, h * hd:(h + 1) * hd], k_h) * (hd ** -0.5)
        s = s - jnp.max(s, axis=-1, keepdims=True)
        e = jnp.exp(s)
        p = e / jnp.sum(e, axis=-1, keepdims=True)
        outs.append(_dot(p, v_h))
    o_ref[...] = x + _dot(jnp.concatenate(outs, axis=1), wco_ref[...])


def _merge_cross(x, gated_a, gated_b, proj_b, w_a, w_b, w_o, nw, w_q, kv, w_co, bsz, seq, n_mem, ts):
    n, d = x.shape
    ns = seq // ts
    full = lambda shape: pl.BlockSpec(shape, lambda b, s: (0,) * len(shape))
    rows = lambda width, col: pl.BlockSpec((ts, width), lambda b, s: (b * ns + s, col))
    return pl.pallas_call(
        _merge_cross_kernel,
        grid=(bsz, ns),
        in_specs=[rows(d, 0), rows(gated_a.shape[1], 0), rows(gated_b.shape[1], 0), rows(d, 4), rows(d, 5),
                  full(w_a.shape), full(w_b.shape), full(w_o.shape), full(nw.shape), full(w_q.shape),
                  pl.BlockSpec((n_mem, 2 * d), lambda b, s: (b, 0)),
                  full(w_co.shape)],
        out_specs=rows(d, 0),
        out_shape=jax.ShapeDtypeStruct((n, d), F32),
        compiler_params=_params("parallel", "parallel"),
        name="merge_cross",
    )(x, gated_a, gated_b, proj_b, proj_b, w_a, w_b, w_o, nw, w_q, kv, w_co)


def _router_kernel(h_ref, nw_ref, wr_ref, br_ref, u_ref, idx_ref, wt_ref, cnt_ref, base_ref):
    t = h_ref.shape[0]

    @pl.when(pl.program_id(0) == 0)
    def _():
        base_ref[...] = jnp.zeros_like(base_ref)

    u = _rmsnorm(h_ref[...], nw_ref[...])
    u_ref[...] = u
    logits = _dot_split(_split(u), _split(wr_ref[...])) + br_ref[...]
    lane = lax.broadcasted_iota(I32, (t, LANES), 1)
    lanef = lane.astype(F32)
    big = float(LANES)
    neg = -jnp.inf

    lg = jnp.where((lane >= N_EXPERTS) & (lane < N_EXPERTS + N_GROUPS), logits, neg)
    gmax = jnp.max(lg, axis=-1, keepdims=True)
    grp = jnp.min(jnp.where(lg == gmax, lanef - float(N_EXPERTS), big), axis=-1, keepdims=True)
    p_grp = 1.0 / jnp.sum(jnp.exp(lg - gmax), axis=-1, keepdims=True)

    in_grp = (lane < N_EXPERTS) & ((lane // GROUP_SIZE).astype(F32) == grp)
    le = jnp.where(in_grp, logits, neg)
    m1 = jnp.max(le, axis=-1, keepdims=True)
    i1 = jnp.min(jnp.where(le == m1, lanef, big), axis=-1, keepdims=True)
    le2 = jnp.where(lanef == i1, neg, le)
    m2 = jnp.max(le2, axis=-1, keepdims=True)
    i2 = jnp.min(jnp.where(le2 == m2, lanef, big), axis=-1, keepdims=True)
    ratio = jnp.exp(m2 - m1)
    p1 = 1.0 / (1.0 + ratio)
    p2 = ratio * p1

    oh1 = lanef == i1
    oh2 = lanef == i2
    onehot = jnp.where(oh1 | oh2, 1.0, 0.0)
    r_i = lax.broadcasted_iota(I32, (t, t), 0)
    c_i = lax.broadcasted_iota(I32, (t, t), 1)
    before = jnp.where(r_i > c_i, 1.0, 0.0)
    rank = _dot(before, onehot) + base_ref[...]
    r1 = jnp.sum(jnp.where(oh1, rank, 0.0), axis=-1, keepdims=True)
    r2 = jnp.sum(jnp.where(oh2, rank, 0.0), axis=-1, keepdims=True)
    base_ref[...] = base_ref[...] + jnp.sum(onehot, axis=0, keepdims=True)

    idx = jnp.where(lane == 0, i1, jnp.where(lane == 1, i2, jnp.where(lane == 2, r1, jnp.where(lane == 3, r2, 0.0))))
    idx_ref[...] = idx.astype(I32)
    wt_ref[...] = jnp.where(lane == 0, p_grp * p1, jnp.where(lane == 1, p_grp * p2, 0.0))
    cnt_ref[...] = jnp.broadcast_to(base_ref[...], cnt_ref.shape)


def _router(h, nw, wr, br, t):
    n, d = h.shape
    full = lambda shape: pl.BlockSpec(shape, lambda i: (0,) * len(shape))
    return pl.pallas_call(
        _router_kernel,
        grid=(n // t,),
        in_specs=[pl.BlockSpec((t, d), lambda i: (i, 0)), full(nw.shape), full(wr.shape), full(br.shape)],
        out_specs=[pl.BlockSpec((t, d), lambda i: (i, 0)),
                   pl.BlockSpec((t, LANES), lambda i: (i, 0)),
                   pl.BlockSpec((t, LANES), lambda i: (i, 0)),
                   pl.BlockSpec((SUBLANES, LANES), lambda i: (0, 0))],
        out_shape=[jax.ShapeDtypeStruct((n, d), F32),
                   jax.ShapeDtypeStruct((n, LANES), I32),
                   jax.ShapeDtypeStruct((n, LANES), F32),
                   jax.ShapeDtypeStruct((SUBLANES, LANES), F32)],
        scratch_shapes=[pltpu.VMEM((1, LANES), F32)],
        compiler_params=_params("arbitrary"),
        name="router",
    )(h, nw, wr, br)


def _zero_block_kernel(last_ref, o_ref):
    del last_ref
    o_ref[...] = jnp.zeros_like(o_ref)


def _zeroed_tail_blocks(last_block, rows, d):
    grid_spec = pltpu.PrefetchScalarGridSpec(
        num_scalar_prefetch=1,
        grid=(last_block.shape[0],),
        in_specs=[],
        out_specs=pl.BlockSpec((MOE_BLOCK, d), lambda e, last: (last[e], 0)),
    )
    return pl.pallas_call(
        _zero_block_kernel,
        grid_spec=grid_spec,
        out_shape=jax.ShapeDtypeStruct((rows, d), F32),
        compiler_params=_params("arbitrary"),
        name="moe_zero_tails",
    )(last_block)


def _row_copy(src, src_row, dst, dst_row, sem):
    return pltpu.make_async_copy(src.at[pl.ds(src_row, 1)], dst.at[pl.ds(dst_row, 1)], sem)


def _dispatch_kernel(dest_ref, u_ref, xs_in_ref, xs_ref, sem):
    del xs_in_ref
    t = u_ref.shape[0]

    def start(r, carry):
        _row_copy(u_ref, r, xs_ref, dest_ref[0, 0, 2 * r], sem).start()
        _row_copy(u_ref, r, xs_ref, dest_ref[0, 0, 2 * r + 1], sem).start()
        return carry

    def wait(r, carry):
        _row_copy(u_ref, r, xs_ref, dest_ref[0, 0, 2 * r], sem).wait()
        _row_copy(u_ref, r, xs_ref, dest_ref[0, 0, 2 * r + 1], sem).wait()
        return carry

    lax.fori_loop(0, t, start, 0, unroll=DMA_UNROLL)
    lax.fori_loop(0, t, wait, 0, unroll=DMA_UNROLL)


def _dispatch(dest3, u, xs_zero, t):
    n, d = u.shape
    return pl.pallas_call(
        _dispatch_kernel,
        grid=(n // t,),
        in_specs=[pl.BlockSpec((1, 1, 2 * t), lambda i: (i, 0, 0), memory_space=pltpu.SMEM),
                  pl.BlockSpec((t, d), lambda i: (i, 0)),
                  pl.BlockSpec(memory_space=pl.ANY)],
        out_specs=pl.BlockSpec(memory_space=pl.ANY),
        out_shape=jax.ShapeDtypeStruct(xs_zero.shape, xs_zero.dtype),
        scratch_shapes=[pltpu.SemaphoreType.DMA(())],
        input_output_aliases={2: 0},
        compiler_params=_params("arbitrary"),
        name="moe_dispatch",
    )(dest3, u, xs_zero)


def _expert_kernel(be_ref, nu_ref, xs_ref, wg_ref, wu_ref, wd_ref, y_ref, wg_bf, wu_bf, wd_bf):
    j = pl.program_id(0)
    used = j < nu_ref[0]
    new_expert = (j == 0) | (be_ref[j] != be_ref[jnp.maximum(j - 1, 0)])

    @pl.when(used & new_expert)
    def _():
        wg_bf[...] = wg_ref[0].astype(BF16)
        wu_bf[...] = wu_ref[0].astype(BF16)
        wd_bf[...] = wd_ref[0].astype(BF16)

    @pl.when(used)
    def _():
        x = xs_ref[...].astype(BF16)
        hid = _silu(jnp.dot(x, wg_bf[...], preferred_element_type=F32)) * jnp.dot(
            x, wu_bf[...], preferred_element_type=F32)
        y_ref[...] = _dot(hid, wd_bf[...])

    @pl.when(jnp.logical_not(used))
    def _():
        y_ref[...] = jnp.zeros_like(y_ref)


def _experts(block_e, n_used, xs, w_gate, w_up, w_down):
    rows, d = xs.shape
    de = w_gate.shape[2]
    grid_spec = pltpu.PrefetchScalarGridSpec(
        num_scalar_prefetch=2,
        grid=(rows // MOE_BLOCK,),
        in_specs=[pl.BlockSpec((MOE_BLOCK, d), lambda j, be, nu: (jnp.minimum(j, nu[0] - 1), 0)),
                  pl.BlockSpec((1, d, de), lambda j, be, nu: (be[j], 0, 0)),
                  pl.BlockSpec((1, d, de), lambda j, be, nu: (be[j], 0, 0)),
                  pl.BlockSpec((1, de, d), lambda j, be, nu: (be[j], 0, 0))],
        out_specs=pl.BlockSpec((MOE_BLOCK, d), lambda j, be, nu: (j, 0)),
        scratch_shapes=[pltpu.VMEM((d, de), BF16), pltpu.VMEM((d, de), BF16), pltpu.VMEM((de, d), BF16)],
    )
    return pl.pallas_call(
        _expert_kernel,
        grid_spec=grid_spec,
        out_shape=jax.ShapeDtypeStruct((rows, d), F32),
        compiler_params=_params("arbitrary"),
        name="moe_experts",
    )(block_e, n_used, xs, w_gate, w_up, w_down)


def _combine_kernel(dest_ref, h_ref, wt_ref, nw_ref, y_ref, o_ref, buf, sem, *, final_norm):
    t = h_ref.shape[0]

    def start(r, carry):
        _row_copy(y_ref, dest_ref[0, 0, 2 * r], buf.at[0], r, sem).start()
        _row_copy(y_ref, dest_ref[0, 0, 2 * r + 1], buf.at[1], r, sem).start()
        return carry

    def wait(r, carry):
        _row_copy(y_ref, dest_ref[0, 0, 2 * r], buf.at[0], r, sem).wait()
        _row_copy(y_ref, dest_ref[0, 0, 2 * r + 1], buf.at[1], r, sem).wait()
        return carry

    lax.fori_loop(0, t, start, 0, unroll=DMA_UNROLL)
    lax.fori_loop(0, t, wait, 0, unroll=DMA_UNROLL)
    wt = wt_ref[...]
    moe = buf[0] * wt[:, 0:1] + buf[1] * wt[:, 1:2]
    out = h_ref[...] + moe
    o_ref[...] = _rmsnorm(out, nw_ref[...]) if final_norm else out


def _combine(dest3, h, wt, nw, y, t, final_norm):
    n, d = h.shape
    return pl.pallas_call(
        functools.partial(_combine_kernel, final_norm=final_norm),
        grid=(n // t,),
        in_specs=[pl.BlockSpec((1, 1, 2 * t), lambda i: (i, 0, 0), memory_space=pltpu.SMEM),
                  pl.BlockSpec((t, d), lambda i: (i, 0)),
                  pl.BlockSpec((t, LANES), lambda i: (i, 0)),
                  pl.BlockSpec((1, d), lambda i: (0, 0)),
                  pl.BlockSpec(memory_space=pl.ANY)],
        out_specs=pl.BlockSpec((t, d), lambda i: (i, 0)),
        out_shape=jax.ShapeDtypeStruct((n, d), F32),
        scratch_shapes=[pltpu.VMEM((2, t, d), F32), pltpu.SemaphoreType.DMA(())],
        compiler_params=_params("arbitrary"),
        name="moe_combine",
    )(dest3, h, wt, nw, y)


def _tile(n, pref):
    return pref if n % pref == 0 else n


def kernel(x, mem, norm1_w, w_in, rnn_conv_w, rnn_conv_b, rglru_wa, rglru_ba, rglru_wx, rglru_bx, rglru_lambda, w_branch_a, dn_conv_w, dn_a_log, dn_dt_bias, dn_norm_w, w_branch_b, w_out, norm2_w, mem_norm_w, w_cq, w_ckv, w_co, norm3_w, w_router_group, b_router_group, w_router_expert, b_router_expert, w_exp_gate, w_exp_up, w_exp_down, norm_f_w):
    bsz, seq, d = x.shape
    n = bsz * seq
    n_mem = mem.shape[1]
    depth = w_in.shape[0]
    d_rnn = rnn_conv_w.shape[2]
    n_heads = dn_a_log.shape[1]
    dn_w = n_heads * HEAD_DIM
    row = lambda v: v.reshape(1, -1).astype(F32)

    h = x.reshape(n, d)
    mem2 = mem.reshape(bsz * n_mem, d)
    tm = _tile(n, 2048)
    for l in range(depth):
        o_rg, o_qkv, o_z = d_rnn, 2 * d_rnn, 2 * d_rnn + 3 * dn_w
        o_a = o_z + dn_w
        o_ga = o_a + 2 * n_heads
        wi = w_in[l]
        w_cat = jnp.concatenate([wi[:, :o_a], wi[:, o_ga:]], axis=1).astype(BF16)
        w_ab = jnp.pad(wi[:, o_a:o_ga], ((0, 0), (0, LANES - 2 * n_heads))).astype(BF16)
        proj_a, proj_b, ab = _in_proj(h, row(norm1_w[l]), w_cat, w_ab, o_qkv, tm, 512)

        gated_a = _rglru(proj_a, bsz, seq, rnn_conv_w[l], row(rnn_conv_b[l]), rglru_wa[l].astype(BF16),
                         row(rglru_ba[l]), rglru_wx[l].astype(BF16), row(rglru_bx[l]), row(rglru_lambda[l]),
                         _tile(seq, 512))

        a_dec = jnp.exp(dn_a_log[l].astype(F32))
        pad_h = lambda v: jnp.pad(v, (0, LANES - n_heads))
        acol, dcol = row(pad_h(a_dec)), row(pad_h(dn_dt_bias[l]))
        arow = jnp.broadcast_to(jnp.pad(a_dec, (0, n_heads))[:, None], (2 * n_heads, GDN_CHUNK))
        drow = jnp.broadcast_to(jnp.pad(dn_dt_bias[l], (0, n_heads))[:, None], (2 * n_heads, GDN_CHUNK))
        abt = ab[:, :2 * n_heads].T
        gated_b = _gdn(proj_b, ab, abt, bsz, seq, dn_conv_w[l], acol, dcol, arow, drow, row(dn_norm_w[l]), n_heads)

        kv = _norm_mm(mem2, row(mem_norm_w[l]), w_ckv[l].astype(BF16), BF16, _tile(bsz * n_mem, 1024), 512)
        h = _merge_cross(h, gated_a, gated_b, proj_b, w_branch_a[l].astype(BF16), w_branch_b[l].astype(BF16),
                         w_out[l].astype(BF16), row(norm2_w[l]), w_cq[l].astype(BF16), kv, w_co[l].astype(BF16),
                         bsz, seq, n_mem, _tile(seq, 512))

        w_r = jnp.pad(jnp.concatenate([w_router_expert[l], w_router_group[l]], axis=1),
                      ((0, 0), (0, LANES - N_EXPERTS - N_GROUPS)))
        b_r = row(jnp.pad(jnp.concatenate([b_router_expert[l], b_router_group[l]]), (0, LANES - N_EXPERTS - N_GROUPS)))
        u3, idx, wt, cnt = _router(h, row(norm3_w[l]), w_r, b_r, _tile(n, 512))

        counts = cnt[0, :N_EXPERTS].astype(I32)
        padded = (counts + MOE_BLOCK - 1) // MOE_BLOCK * MOE_BLOCK
        pend = jnp.cumsum(padded)
        pstart = pend - padded
        is_expert = idx[:, 0:2, None] == jnp.arange(N_EXPERTS, dtype=I32)
        dest = jnp.sum(jnp.where(is_expert, pstart, 0), axis=-1) + idx[:, 2:4]
        n_blocks = (2 * n + N_EXPERTS * (MOE_BLOCK - 1)) // MOE_BLOCK
        block_row = jnp.arange(n_blocks, dtype=I32) * MOE_BLOCK
        block_e = jnp.minimum(jnp.sum((pend[None, :] <= block_row[:, None]).astype(I32), axis=1), N_EXPERTS - 1)
        n_used = (pend[-1:] // MOE_BLOCK).astype(I32)
        t_moe = _tile(n, 512)
        dest3 = dest.reshape(n // t_moe, 1, 2 * t_moe)
        last_block = jnp.maximum(pend // MOE_BLOCK - 1, 0).astype(I32)
        xs = _dispatch(dest3, u3, _zeroed_tail_blocks(last_block, n_blocks * MOE_BLOCK, d), t_moe)
        yb = _experts(block_e, n_used, xs, w_exp_gate[l], w_exp_up[l], w_exp_down[l])
        h = _combine(dest3, h, wt, row(norm_f_w), yb, t_moe, final_norm=(l == depth - 1))
    return h.reshape(bsz, seq, d)
```

```python
import functools
import math

import jax
import jax.numpy as jnp
from jax import lax
from jax.experimental import pallas as pl
from jax.experimental.pallas import tpu as pltpu

F32 = jnp.float32
BF16 = jnp.bfloat16
I32 = jnp.int32
HIGHEST = lax.Precision.HIGHEST

NORM_EPS = 1e-6
CONV_TAPS = 4
RNN_BLOCK = 128
RG_POWER = 8.0
HEAD_DIM = 128
GDN_CHUNK = 128
GDN_CHUNKS_PER_STEP = 2
CA_HEADS = 4
N_GROUPS = 8
GROUP_SIZE = 8
N_EXPERTS = N_GROUPS * GROUP_SIZE
MOE_BLOCK = 512
LANES = 128
SUBLANES = 8
DMA_UNROLL = 8
VMEM_LIMIT = 48 * 1024 * 1024


def _params(*semantics):
    return pltpu.CompilerParams(dimension_semantics=semantics, vmem_limit_bytes=VMEM_LIMIT)


def _dot(a, b):
    return jnp.dot(a.astype(BF16), b.astype(BF16), preferred_element_type=F32)


def _dot_nt(a, b):
    return lax.dot_general(a.astype(BF16), b.astype(BF16), (((1,), (1,)), ((), ())),
                           preferred_element_type=F32)


def _dot_tn(a, b):
    return lax.dot_general(a.astype(BF16), b.astype(BF16), (((0,), (0,)), ((), ())),
                           preferred_element_type=F32)


def _dot_f32(a, b):
    return jnp.dot(a, b, precision=HIGHEST, preferred_element_type=F32)


def _split(x):
    hi = x.astype(BF16)
    return hi, (x - hi.astype(F32)).astype(BF16)


def _dot_split(a_parts, b_parts):
    (a_hi, a_lo), (b_hi, b_lo) = a_parts, b_parts
    dot = functools.partial(jnp.dot, preferred_element_type=F32)
    return dot(a_hi, b_hi) + (dot(a_hi, b_lo) + dot(a_lo, b_hi))


def _rmsnorm(x, w):
    return x * lax.rsqrt(jnp.mean(x * x, axis=-1, keepdims=True) + NORM_EPS) * w


def _sigmoid(x):
    return 0.5 * jnp.tanh(0.5 * x) + 0.5


def _silu(x):
    return x * _sigmoid(x)


def _softplus(x):
    return jnp.maximum(x, 0.0) + jnp.log(1.0 + jnp.exp(-jnp.abs(x)))


def _one_minus_exp2(y, exp_y):
    return jnp.tanh(-y) * (1.0 + exp_y * exp_y)


def _gelu_tanh(x):
    return 0.5 * x * (1.0 + jnp.tanh(math.sqrt(2.0 / math.pi) * (x + 0.044715 * (x * x * x))))


def _norm_mm_kernel(x_ref, nw_ref, w_ref, o_ref, u_ref):
    @pl.when(pl.program_id(1) == 0)
    def _():
        u_ref[...] = _rmsnorm(x_ref[...], nw_ref[...]).astype(BF16)

    o_ref[...] = jnp.dot(u_ref[...], w_ref[...], preferred_element_type=F32).astype(o_ref.dtype)


def _norm_mm(x, nw, w, out_dtype, tm, tn):
    n, d = x.shape
    c = w.shape[1]
    return pl.pallas_call(
        _norm_mm_kernel,
        grid=(n // tm, c // tn),
        in_specs=[pl.BlockSpec((tm, d), lambda i, j: (i, 0)),
                  pl.BlockSpec((1, d), lambda i, j: (0, 0)),
                  pl.BlockSpec((d, tn), lambda i, j: (0, j))],
        out_specs=pl.BlockSpec((tm, tn), lambda i, j: (i, j)),
        out_shape=jax.ShapeDtypeStruct((n, c), out_dtype),
        scratch_shapes=[pltpu.VMEM((tm, d), BF16)],
        compiler_params=_params("parallel", "arbitrary"),
        name="norm_mm",
    )(x, nw, w)


def _in_proj_kernel(x_ref, nw_ref, w_ref, wab_ref, oa_ref, ob_ref, ab_ref, u_ref, *, n_a):
    j = pl.program_id(1)

    @pl.when(j == 0)
    def _():
        u_ref[...] = _rmsnorm(x_ref[...], nw_ref[...]).astype(BF16)
        ab_ref[...] = jnp.dot(u_ref[...], wab_ref[...], preferred_element_type=F32)

    @pl.when(j < n_a)
    def _():
        oa_ref[...] = jnp.dot(u_ref[...], w_ref[...], preferred_element_type=F32).astype(oa_ref.dtype)

    @pl.when(j >= n_a)
    def _():
        ob_ref[...] = jnp.dot(u_ref[...], w_ref[...], preferred_element_type=F32).astype(ob_ref.dtype)


def _in_proj(x, nw, w_cat, w_ab, c_a, tm, tn):
    n, d = x.shape
    c = w_cat.shape[1]
    n_a, n_b = c_a // tn, (c - c_a) // tn
    return pl.pallas_call(
        functools.partial(_in_proj_kernel, n_a=n_a),
        grid=(n // tm, n_a + n_b),
        in_specs=[pl.BlockSpec((tm, d), lambda i, j: (i, 0)),
                  pl.BlockSpec((1, d), lambda i, j: (0, 0)),
                  pl.BlockSpec((d, tn), lambda i, j: (0, j)),
                  pl.BlockSpec(w_ab.shape, lambda i, j: (0, 0))],
        out_specs=[pl.BlockSpec((tm, tn), lambda i, j: (i, jnp.minimum(j, n_a - 1))),
                   pl.BlockSpec((tm, tn), lambda i, j: (i, jnp.maximum(j - n_a, 0))),
                   pl.BlockSpec((tm, w_ab.shape[1]), lambda i, j: (i, 0))],
        out_shape=[jax.ShapeDtypeStruct((n, c_a), BF16), jax.ShapeDtypeStruct((n, c - c_a), BF16),
                   jax.ShapeDtypeStruct((n, w_ab.shape[1]), F32)],
        scratch_shapes=[pltpu.VMEM((tm, d), BF16)],
        compiler_params=_params("arbitrary", "arbitrary"),
        name="in_proj",
    )(x, nw, w_cat, w_ab)


def _load_conv_window(x_ref, xbuf, ts):
    @pl.when(pl.program_id(1) == 0)
    def _():
        xbuf[0:SUBLANES, :] = jnp.zeros((SUBLANES, xbuf.shape[1]), F32)

    @pl.when(pl.program_id(1) != 0)
    def _():
        xbuf[0:SUBLANES, :] = xbuf[ts:ts + SUBLANES, :]

    xbuf[SUBLANES:SUBLANES + ts, :] = x_ref[...].astype(F32)


def _causal_conv(xbuf, cw_ref, ts, first_row=0, cols=slice(None)):
    base = first_row + SUBLANES - (CONV_TAPS - 1)
    acc = cw_ref[0:1, cols] * xbuf[base:base + ts, cols]
    for k in range(1, CONV_TAPS):
        acc = acc + cw_ref[k:k + 1, cols] * xbuf[base + k:base + k + ts, cols]
    return acc


def _rglru_kernel(rx_ref, rg_ref, cw_ref, cb_ref, wa_ref, ba_ref, wx_ref, bx_ref, lam_ref, o_ref,
                  xbuf, a_ref, b_ref, carry_ref):
    ts, c = rx_ref.shape
    _load_conv_window(rx_ref, xbuf, ts)

    @pl.when(pl.program_id(1) == 0)
    def _():
        carry_ref[...] = jnp.zeros_like(carry_ref)

    xc = _causal_conv(xbuf, cw_ref, ts) + cb_ref[...]
    neg_sp = -RG_POWER * _softplus(-lam_ref[...])
    for n in range(c // RNN_BLOCK):
        sl = slice(n * RNN_BLOCK, (n + 1) * RNN_BLOCK)
        xb = xc[:, sl]
        r = _sigmoid(_dot(xb, wa_ref[n]) + ba_ref[:, sl])
        i = _sigmoid(_dot(xb, wx_ref[n]) + bx_ref[:, sl])
        log_a = neg_sp[:, sl] * r
        a = jnp.exp(log_a)
        a_ref[:, sl] = a
        b_ref[:, sl] = jnp.sqrt(_one_minus_exp2(log_a, a)) * (i * xb)

    row = lax.broadcasted_iota(I32, (SUBLANES, c), 0)

    def slab(t, carry):
        rows = pl.ds(pl.multiple_of(t * SUBLANES, SUBLANES), SUBLANES)
        a = a_ref[rows, :]
        b = b_ref[rows, :]
        for d in (1, 2, 4):
            a_sh = jnp.where(row >= d, pltpu.roll(a, d, 0), 1.0)
            b_sh = jnp.where(row >= d, pltpu.roll(b, d, 0), 0.0)
            b = a * b_sh + b
            a = a * a_sh
        h = a * carry + b
        b_ref[rows, :] = h
        return h[SUBLANES - 1:SUBLANES, :]

    carry_ref[...] = lax.fori_loop(0, ts // SUBLANES, slab, carry_ref[...])
    o_ref[...] = (_gelu_tanh(rg_ref[...].astype(F32)) * b_ref[...]).astype(o_ref.dtype)


def _rglru(proj_a, bsz, seq, cw, cb, wa, ba, wx, bx, lam, ts):
    c = cw.shape[1]
    ns = seq // ts
    full = lambda shape: pl.BlockSpec(shape, lambda b, s: (0,) * len(shape))
    return pl.pallas_call(
        _rglru_kernel,
        grid=(bsz, ns),
        in_specs=[pl.BlockSpec((ts, c), lambda b, s: (b * ns + s, 0)),
                  pl.BlockSpec((ts, c), lambda b, s: (b * ns + s, 1)),
                  full(cw.shape), full(cb.shape), full(wa.shape), full(ba.shape),
                  full(wx.shape), full(bx.shape), full(lam.shape)],
        out_specs=pl.BlockSpec((ts, c), lambda b, s: (b * ns + s, 0)),
        out_shape=jax.ShapeDtypeStruct((bsz * seq, c), BF16),
        scratch_shapes=[pltpu.VMEM((ts + SUBLANES, c), F32), pltpu.VMEM((ts, c), F32),
                        pltpu.VMEM((ts, c), F32), pltpu.VMEM((1, c), F32)],
        compiler_params=_params("parallel", "arbitrary"),
        name="rglru",
    )(proj_a, proj_a, cw, cb, wa, ba, wx, bx, lam)


def _lane_bcast(x, lane):
    return jnp.broadcast_to(x[:, lane:lane + 1], x.shape)


def _unit_lower_inverses(neg_ls, fillers):
    ts = neg_ls[0].shape[0]
    half = ts // 2
    run_filler = lambda: next(fillers, lambda: None)()
    dot = lambda a, b: _dot_split(_split(a), _split(b))
    lane = lax.broadcasted_iota(I32, (half, ts), 1)
    left = lane < half
    left_of = lambda x: jnp.where(left, x, 0.0)
    right_of = lambda x: jnp.where(left, 0.0, x)
    diag = lambda ab: jnp.concatenate([left_of(ab), right_of(ab)], axis=0)
    eye_pair = ((lane & (half - 1)) == lax.broadcasted_iota(I32, (half, ts), 0)).astype(F32)

    pairs = [n[:half] + right_of(n[half:]) for n in neg_ls]
    couplings = [left_of(n[half:]) for n in neg_ls]
    ts_mats = [eye_pair + ab for ab in pairs]
    ps = [dot(ab, diag(ab)) for ab in pairs]
    run_filler()
    levels = int(math.log2(half)) - 1
    for lvl in range(levels - 1):
        both = [dot(jnp.concatenate([t, p], axis=0), diag(p)) for t, p in zip(ts_mats, ps)]
        ts_mats = [t + b[:half] for t, b in zip(ts_mats, both)]
        ps = [b[half:] for b in both]
        run_filler()
    ts_mats = [t + dot(t, diag(p)) for t, p in zip(ts_mats, ps)]
    run_filler()
    c_ta = [dot(c, diag(t)) for c, t in zip(couplings, ts_mats)]
    run_filler()
    tb_c_ta = [dot(t, jnp.concatenate([jnp.zeros_like(m), m], axis=0)) for t, m in zip(ts_mats, c_ta)]
    return [jnp.concatenate([left_of(t), jnp.where(left, x, t)], axis=0) for t, x in zip(ts_mats, tb_c_ta)]


def _gdn_kernel(qkv_ref, z_ref, ab_ref, abt_ref, cw_ref, acol_ref, dcol_ref, arow_ref, drow_ref, nw_ref,
                o_ref, xbuf, s_ref):
    ts = GDN_CHUNK
    n_heads = s_ref.shape[0]
    dn_w = n_heads * HEAD_DIM
    _load_conv_window(qkv_ref, xbuf, qkv_ref.shape[0])

    @pl.when(pl.program_id(1) == 0)
    def _():
        s_ref[...] = jnp.zeros_like(s_ref)

    row = lax.broadcasted_iota(I32, (ts, ts), 0)
    col = lax.broadcasted_iota(I32, (ts, ts), 1)
    lower_incl = (row >= col).astype(F32)
    upper_incl = (row <= col).astype(F32)
    head_cols = lambda h, part: slice(part * dn_w + h * HEAD_DIM, part * dn_w + (h + 1) * HEAD_DIM)
    heads = range(n_heads)

    n_chunks = qkv_ref.shape[0] // ts
    state = [dict(q=[None] * n_heads, k=[None] * n_heads, k_beta=[None] * n_heads, decay=[None] * n_heads,
                  rhs=[None] * n_heads, qd=[None] * n_heads, kd=[None] * n_heads) for _ in range(n_chunks)]

    def prepare_gates(c):
        st = state[c]
        rows = slice(c * ts, (c + 1) * ts)
        ab = ab_ref[rows, :]
        g_col = -acol_ref[...] * _softplus(ab + dcol_ref[...])
        st["beta"] = _sigmoid(ab)
        g_row = -arow_ref[...] * _softplus(abt_ref[:, rows] + drow_ref[...])
        st["cum_col"] = _dot_f32(lower_incl, g_col)
        st["cum_row"] = _dot_f32(g_row, upper_incl)
        last = st["cum_col"][ts - 1:ts, :]
        st["exp_cum"] = jnp.exp(st["cum_col"])
        st["exp_rem"] = jnp.exp(jnp.broadcast_to(last, (ts, LANES)) - st["cum_col"])
        st["exp_last"] = jnp.exp(last)

    def conv_silu(c, h, part):
        return _silu(_causal_conv(xbuf, cw_ref, ts, c * ts, head_cols(h, part)))

    def prepare_key(c, h):
        st = state[c]
        k = conv_silu(c, h, 1)
        k = k * lax.rsqrt(jnp.sum(k * k, axis=-1, keepdims=True) + NORM_EPS)
        st["k"][h], st["k_beta"][h] = k, k * _lane_bcast(st["beta"], n_heads + h)
        st["decay"][h] = jnp.exp(jnp.minimum(_lane_bcast(st["cum_col"], h) - st["cum_row"][h:h + 1, :], 0.0))

    def prepare_query_value(c, h):
        st = state[c]
        q, v = conv_silu(c, h, 0), conv_silu(c, h, 2)
        q = q * (lax.rsqrt(jnp.sum(q * q, axis=-1, keepdims=True) + NORM_EPS) * (HEAD_DIM ** -0.5))
        e_cum = _lane_bcast(st["exp_cum"], h)
        st["q"][h] = q
        st["rhs"][h] = jnp.concatenate([v * _lane_bcast(st["beta"], n_heads + h), st["k_beta"][h] * e_cum], axis=1)
        st["qd"][h] = q * e_cum
        st["kd"][h] = st["k"][h] * _lane_bcast(st["exp_rem"], h)

    def key_thunks(c):
        if c >= n_chunks:
            return []
        return [functools.partial(prepare_gates, c)] + [functools.partial(prepare_key, c, h) for h in heads]

    def solve(c):
        st = state[c]
        kks = [_dot_nt(st["k_beta"][h], st["k"][h]) for h in heads]
        neg_ls = [jnp.where(row > col, -(kks[h] * st["decay"][h]), 0.0) for h in heads]
        thunks = [functools.partial(prepare_query_value, c, h) for h in heads] + key_thunks(c + 1)
        slots = int(math.log2(ts))
        per_slot = -(-len(thunks) // slots)
        groups = [thunks[i:i + per_slot] for i in range(0, len(thunks), per_slot)]
        fillers = iter([functools.partial(lambda g: [t() for t in g], g) for g in groups])
        t_mats = _unit_lower_inverses(neg_ls, fillers)
        for group in fillers:
            group()
        qks = [_dot_nt(st["q"][h], st["k"][h]) for h in heads]
        st["intra"] = [jnp.where(row >= col, qks[h] * st["decay"][h], 0.0) for h in heads]
        st["uw"] = [_dot(t_mats[h], st["rhs"][h]) for h in heads]

    def advance(c):
        st = state[c]
        rows = slice(c * ts, (c + 1) * ts)
        ws_qs = [_dot(jnp.concatenate([st["uw"][h][:, HEAD_DIM:], st["qd"][h]], axis=0), s_ref[h]) for h in heads]
        v_news = [st["uw"][h][:, :HEAD_DIM] - ws_qs[h][:ts] for h in heads]
        mixed = [_dot(jnp.concatenate([st["intra"][h], st["kd"][h].T], axis=0), v_news[h]) for h in heads]
        for h in heads:
            s_ref[h] = (s_ref[h] * _lane_bcast(jnp.broadcast_to(st["exp_last"], (HEAD_DIM, LANES)), h)
                        + mixed[h][ts:])
            o = _rmsnorm(ws_qs[h][ts:] + mixed[h][:ts], nw_ref[...]) * _silu(z_ref[rows, head_cols(h, 0)].astype(F32))
            o_ref[rows, head_cols(h, 0)] = o.astype(o_ref.dtype)

    for thunk in key_thunks(0):
        thunk()
    for c in range(n_chunks):
        solve(c)
        advance(c)


def _gdn(proj_b, ab, abt, bsz, seq, cw, acol, dcol, arow, drow, nw, n_heads):
    ts = GDN_CHUNK * GDN_CHUNKS_PER_STEP
    assert seq % ts == 0
    ns = seq // ts
    dn_w = n_heads * HEAD_DIM
    full = lambda shape: pl.BlockSpec(shape, lambda b, s: (0,) * len(shape))
    return pl.pallas_call(
        _gdn_kernel,
        grid=(bsz, ns),
        in_specs=[pl.BlockSpec((ts, 3 * dn_w), lambda b, s: (b * ns + s, 0)),
                  pl.BlockSpec((ts, dn_w), lambda b, s: (b * ns + s, 3)),
                  pl.BlockSpec((ts, LANES), lambda b, s: (b * ns + s, 0)),
                  pl.BlockSpec((2 * n_heads, ts), lambda b, s: (0, b * ns + s)),
                  full(cw.shape), full(acol.shape), full(dcol.shape), full(arow.shape), full(drow.shape),
                  full(nw.shape)],
        out_specs=pl.BlockSpec((ts, dn_w), lambda b, s: (b * ns + s, 0)),
        out_shape=jax.ShapeDtypeStruct((bsz * seq, dn_w), BF16),
        scratch_shapes=[pltpu.VMEM((ts + SUBLANES, 3 * dn_w), F32),
                        pltpu.VMEM((n_heads, HEAD_DIM, HEAD_DIM), F32)],
        compiler_params=_params("parallel", "arbitrary"),
        name="gdn",
    )(proj_b, proj_b, ab, abt, cw, acol, dcol, arow, drow, nw)


def _merge_cross_kernel(x_ref, ya_ref, yb_ref, ga_ref, gb_ref, wa_ref, wb_ref, wo_ref,
                        nw_ref, wq_ref, kv_ref, wco_ref, o_ref):
    y_a = jnp.dot(ya_ref[...], wa_ref[...], preferred_element_type=F32)
    y_b = jnp.dot(yb_ref[...], wb_ref[...], preferred_element_type=F32)
    m = _sigmoid(ga_ref[...].astype(F32)) * y_a + _sigmoid(gb_ref[...].astype(F32)) * y_b
    x = x_ref[...] + _dot(m, wo_ref[...])

    d = x.shape[1]
    hd = d // CA_HEADS
    q = _dot(_rmsnorm(x, nw_ref[...]), wq_ref[...])
    outs = []
    for h in range(CA_HEADS):
        k_h = kv_ref[:, h * hd:(h + 1) * hd]
        v_h = kv_ref[:, d + h * hd:d + (h + 1) * hd]
        s = _dot_nt(q[:, h * hd:(h + 1) * hd], k_h) * (hd ** -0.5)
        s = s - jnp.max(s, axis=-1, keepdims=True)
        e = jnp.exp(s)
        p = e / jnp.sum(e, axis=-1, keepdims=True)
        outs.append(_dot(p, v_h))
    o_ref[...] = x + _dot(jnp.concatenate(outs, axis=1), wco_ref[...])


def _merge_cross(x, gated_a, gated_b, proj_b, w_a, w_b, w_o, nw, w_q, kv, w_co, bsz, seq, n_mem, ts):
    n, d = x.shape
    ns = seq // ts
    full = lambda shape: pl.BlockSpec(shape, lambda b, s: (0,) * len(shape))
    rows = lambda width, col: pl.BlockSpec((ts, width), lambda b, s: (b * ns + s, col))
    return pl.pallas_call(
        _merge_cross_kernel,
        grid=(bsz, ns),
        in_specs=[rows(d, 0), rows(gated_a.shape[1], 0), rows(gated_b.shape[1], 0), rows(d, 4), rows(d, 5),
                  full(w_a.shape), full(w_b.shape), full(w_o.shape), full(nw.shape), full(w_q.shape),
                  pl.BlockSpec((n_mem, 2 * d), lambda b, s: (b, 0)),
                  full(w_co.shape)],
        out_specs=rows(d, 0),
        out_shape=jax.ShapeDtypeStruct((n, d), F32),
        compiler_params=_params("parallel", "parallel"),
        name="merge_cross",
    )(x, gated_a, gated_b, proj_b, proj_b, w_a, w_b, w_o, nw, w_q, kv, w_co)


def _router_kernel(h_ref, nw_ref, wr_ref, br_ref, u_ref, idx_ref, wt_ref, cnt_ref, base_ref):
    t = h_ref.shape[0]

    @pl.when(pl.program_id(0) == 0)
    def _():
        base_ref[...] = jnp.zeros_like(base_ref)

    u = _rmsnorm(h_ref[...], nw_ref[...])
    u_ref[...] = u
    logits = _dot_split(_split(u), _split(wr_ref[...])) + br_ref[...]
    lane = lax.broadcasted_iota(I32, (t, LANES), 1)
    lanef = lane.astype(F32)
    big = float(LANES)
    neg = -jnp.inf

    lg = jnp.where((lane >= N_EXPERTS) & (lane < N_EXPERTS + N_GROUPS), logits, neg)
    gmax = jnp.max(lg, axis=-1, keepdims=True)
    grp = jnp.min(jnp.where(lg == gmax, lanef - float(N_EXPERTS), big), axis=-1, keepdims=True)
    p_grp = 1.0 / jnp.sum(jnp.exp(lg - gmax), axis=-1, keepdims=True)

    in_grp = (lane < N_EXPERTS) & ((lane // GROUP_SIZE).astype(F32) == grp)
    le = jnp.where(in_grp, logits, neg)
    m1 = jnp.max(le, axis=-1, keepdims=True)
    i1 = jnp.min(jnp.where(le == m1, lanef, big), axis=-1, keepdims=True)
    le2 = jnp.where(lanef == i1, neg, le)
    m2 = jnp.max(le2, axis=-1, keepdims=True)
    i2 = jnp.min(jnp.where(le2 == m2, lanef, big), axis=-1, keepdims=True)
    ratio = jnp.exp(m2 - m1)
    p1 = 1.0 / (1.0 + ratio)
    p2 = ratio * p1

    oh1 = lanef == i1
    oh2 = lanef == i2
    onehot = jnp.where(oh1 | oh2, 1.0, 0.0)
    r_i = lax.broadcasted_iota(I32, (t, t), 0)
    c_i = lax.broadcasted_iota(I32, (t, t), 1)
    before = jnp.where(r_i > c_i, 1.0, 0.0)
    rank = _dot(before, onehot) + base_ref[...]
    r1 = jnp.sum(jnp.where(oh1, rank, 0.0), axis=-1, keepdims=True)
    r2 = jnp.sum(jnp.where(oh2, rank, 0.0), axis=-1, keepdims=True)
    base_ref[...] = base_ref[...] + jnp.sum(onehot, axis=0, keepdims=True)

    idx = jnp.where(lane == 0, i1, jnp.where(lane == 1, i2, jnp.where(lane == 2, r1, jnp.where(lane == 3, r2, 0.0))))
    idx_ref[...] = idx.astype(I32)
    wt_ref[...] = jnp.where(lane == 0, p_grp * p1, jnp.where(lane == 1, p_grp * p2, 0.0))
    cnt_ref[...] = jnp.broadcast_to(base_ref[...], cnt_ref.shape)


def _router(h, nw, wr, br, t):
    n, d = h.shape
    full = lambda shape: pl.BlockSpec(shape, lambda i: (0,) * len(shape))
    return pl.pallas_call(
        _router_kernel,
        grid=(n // t,),
        in_specs=[pl.BlockSpec((t, d), lambda i: (i, 0)), full(nw.shape), full(wr.shape), full(br.shape)],
        out_specs=[pl.BlockSpec((t, d), lambda i: (i, 0)),
                   pl.BlockSpec((t, LANES), lambda i: (i, 0)),
                   pl.BlockSpec((t, LANES), lambda i: (i, 0)),
                   pl.BlockSpec((SUBLANES, LANES), lambda i: (0, 0))],
        out_shape=[jax.ShapeDtypeStruct((n, d), F32),
                   jax.ShapeDtypeStruct((n, LANES), I32),
                   jax.ShapeDtypeStruct((n, LANES), F32),
                   jax.ShapeDtypeStruct((SUBLANES, LANES), F32)],
        scratch_shapes=[pltpu.VMEM((1, LANES), F32)],
        compiler_params=_params("arbitrary"),
        name="router",
    )(h, nw, wr, br)


def _zero_block_kernel(last_ref, o_ref):
    del last_ref
    o_ref[...] = jnp.zeros_like(o_ref)


def _zeroed_tail_blocks(last_block, rows, d):
    grid_spec = pltpu.PrefetchScalarGridSpec(
        num_scalar_prefetch=1,
        grid=(last_block.shape[0],),
        in_specs=[],
        out_specs=pl.BlockSpec((MOE_BLOCK, d), lambda e, last: (last[e], 0)),
    )
    return pl.pallas_call(
        _zero_block_kernel,
        grid_spec=grid_spec,
        out_shape=jax.ShapeDtypeStruct((rows, d), F32),
        compiler_params=_params("arbitrary"),
        name="moe_zero_tails",
    )(last_block)


def _row_copy(src, src_row, dst, dst_row, sem):
    return pltpu.make_async_copy(src.at[pl.ds(src_row, 1)], dst.at[pl.ds(dst_row, 1)], sem)


def _dispatch_kernel(dest_ref, u_ref, xs_in_ref, xs_ref, sem):
    del xs_in_ref
    t = u_ref.shape[0]

    def start(r, carry):
        _row_copy(u_ref, r, xs_ref, dest_ref[0, 0, 2 * r], sem).start()
        _row_copy(u_ref, r, xs_ref, dest_ref[0, 0, 2 * r + 1], sem).start()
        return carry

    def wait(r, carry):
        _row_copy(u_ref, r, xs_ref, dest_ref[0, 0, 2 * r], sem).wait()
        _row_copy(u_ref, r, xs_ref, dest_ref[0, 0, 2 * r + 1], sem).wait()
        return carry

    lax.fori_loop(0, t, start, 0, unroll=DMA_UNROLL)
    lax.fori_loop(0, t, wait, 0, unroll=DMA_UNROLL)


def _dispatch(dest3, u, xs_zero, t):
    n, d = u.shape
    return pl.pallas_call(
        _dispatch_kernel,
        grid=(n // t,),
        in_specs=[pl.BlockSpec((1, 1, 2 * t), lambda i: (i, 0, 0), memory_space=pltpu.SMEM),
                  pl.BlockSpec((t, d), lambda i: (i, 0)),
                  pl.BlockSpec(memory_space=pl.ANY)],
        out_specs=pl.BlockSpec(memory_space=pl.ANY),
        out_shape=jax.ShapeDtypeStruct(xs_zero.shape, xs_zero.dtype),
        scratch_shapes=[pltpu.SemaphoreType.DMA(())],
        input_output_aliases={2: 0},
        compiler_params=_params("arbitrary"),
        name="moe_dispatch",
    )(dest3, u, xs_zero)


def _expert_kernel(be_ref, nu_ref, xs_ref, wg_ref, wu_ref, wd_ref, y_ref, wg_bf, wu_bf, wd_bf):
    j = pl.program_id(0)
    used = j < nu_ref[0]
    new_expert = (j == 0) | (be_ref[j] != be_ref[jnp.maximum(j - 1, 0)])

    @pl.when(used & new_expert)
    def _():
        wg_bf[...] = wg_ref[0].astype(BF16)
        wu_bf[...] = wu_ref[0].astype(BF16)
        wd_bf[...] = wd_ref[0].astype(BF16)

    @pl.when(used)
    def _():
        x = xs_ref[...].astype(BF16)
        hid = _silu(jnp.dot(x, wg_bf[...], preferred_element_type=F32)) * jnp.dot(
            x, wu_bf[...], preferred_element_type=F32)
        y_ref[...] = _dot(hid, wd_bf[...])

    @pl.when(jnp.logical_not(used))
    def _():
        y_ref[...] = jnp.zeros_like(y_ref)


def _experts(block_e, n_used, xs, w_gate, w_up, w_down):
    rows, d = xs.shape
    de = w_gate.shape[2]
    grid_spec = pltpu.PrefetchScalarGridSpec(
        num_scalar_prefetch=2,
        grid=(rows // MOE_BLOCK,),
        in_specs=[pl.BlockSpec((MOE_BLOCK, d), lambda j, be, nu: (jnp.minimum(j, nu[0] - 1), 0)),
                  pl.BlockSpec((1, d, de), lambda j, be, nu: (be[j], 0, 0)),
                  pl.BlockSpec((1, d, de), lambda j, be, nu: (be[j], 0, 0)),
                  pl.BlockSpec((1, de, d), lambda j, be, nu: (be[j], 0, 0))],
        out_specs=pl.BlockSpec((MOE_BLOCK, d), lambda j, be, nu: (j, 0)),
        scratch_shapes=[pltpu.VMEM((d, de), BF16), pltpu.VMEM((d, de), BF16), pltpu.VMEM((de, d), BF16)],
    )
    return pl.pallas_call(
        _expert_kernel,
        grid_spec=grid_spec,
        out_shape=jax.ShapeDtypeStruct((rows, d), F32),
        compiler_params=_params("arbitrary"),
        name="moe_experts",
    )(block_e, n_used, xs, w_gate, w_up, w_down)


def _combine_kernel(dest_ref, h_ref, wt_ref, nw_ref, y_ref, o_ref, buf, sem, *, final_norm):
    t = h_ref.shape[0]

    def start(r, carry):
        _row_copy(y_ref, dest_ref[0, 0, 2 * r], buf.at[0], r, sem).start()
        _row_copy(y_ref, dest_ref[0, 0, 2 * r + 1], buf.at[1], r, sem).start()
        return carry

    def wait(r, carry):
        _row_copy(y_ref, dest_ref[0, 0, 2 * r], buf.at[0], r, sem).wait()
        _row_copy(y_ref, dest_ref[0, 0, 2 * r + 1], buf.at[1], r, sem).wait()
        return carry

    lax.fori_loop(0, t, start, 0, unroll=DMA_UNROLL)
    lax.fori_loop(0, t, wait, 0, unroll=DMA_UNROLL)
    wt = wt_ref[...]
    moe = buf[0] * wt[:, 0:1] + buf[1] * wt[:, 1:2]
    out = h_ref[...] + moe
    o_ref[...] = _rmsnorm(out, nw_ref[...]) if final_norm else out


def _combine(dest3, h, wt, nw, y, t, final_norm):
    n, d = h.shape
    return pl.pallas_call(
        functools.partial(_combine_kernel, final_norm=final_norm),
        grid=(n // t,),
        in_specs=[pl.BlockSpec((1, 1, 2 * t), lambda i: (i, 0, 0), memory_space=pltpu.SMEM),
                  pl.BlockSpec((t, d), lambda i: (i, 0)),
                  pl.BlockSpec((t, LANES), lambda i: (i, 0)),
                  pl.BlockSpec((1, d), lambda i: (0, 0)),
                  pl.BlockSpec(memory_space=pl.ANY)],
        out_specs=pl.BlockSpec((t, d), lambda i: (i, 0)),
        out_shape=jax.ShapeDtypeStruct((n, d), F32),
        scratch_shapes=[pltpu.VMEM((2, t, d), F32), pltpu.SemaphoreType.DMA(())],
        compiler_params=_params("arbitrary"),
        name="moe_combine",
    )(dest3, h, wt, nw, y)


def _tile(n, pref):
    return pref if n % pref == 0 else n


def kernel(x, mem, norm1_w, w_in, rnn_conv_w, rnn_conv_b, rglru_wa, rglru_ba, rglru_wx, rglru_bx, rglru_lambda, w_branch_a, dn_conv_w, dn_a_log, dn_dt_bias, dn_norm_w, w_branch_b, w_out, norm2_w, mem_norm_w, w_cq, w_ckv, w_co, norm3_w, w_router_group, b_router_group, w_router_expert, b_router_expert, w_exp_gate, w_exp_up, w_exp_down, norm_f_w):
    bsz, seq, d = x.shape
    n = bsz * seq
    n_mem = mem.shape[1]
    depth = w_in.shape[0]
    d_rnn = rnn_conv_w.shape[2]
    n_heads = dn_a_log.shape[1]
    dn_w = n_heads * HEAD_DIM
    row = lambda v: v.reshape(1, -1).astype(F32)

    h = x.reshape(n, d)
    mem2 = mem.reshape(bsz * n_mem, d)
    tm = _tile(n, 2048)
    for l in range(depth):
        o_rg, o_qkv, o_z = d_rnn, 2 * d_rnn, 2 * d_rnn + 3 * dn_w
        o_a = o_z + dn_w
        o_ga = o_a + 2 * n_heads
        wi = w_in[l]
        w_cat = jnp.concatenate([wi[:, :o_a], wi[:, o_ga:]], axis=1).astype(BF16)
        w_ab = jnp.pad(wi[:, o_a:o_ga], ((0, 0), (0, LANES - 2 * n_heads))).astype(BF16)
        proj_a, proj_b, ab = _in_proj(h, row(norm1_w[l]), w_cat, w_ab, o_qkv, tm, 512)

        gated_a = _rglru(proj_a, bsz, seq, rnn_conv_w[l], row(rnn_conv_b[l]), rglru_wa[l].astype(BF16),
                         row(rglru_ba[l]), rglru_wx[l].astype(BF16), row(rglru_bx[l]), row(rglru_lambda[l]),
                         _tile(seq, 512))

        a_dec = jnp.exp(dn_a_log[l].astype(F32))
        pad_h = lambda v: jnp.pad(v, (0, LANES - n_heads))
        acol, dcol = row(pad_h(a_dec)), row(pad_h(dn_dt_bias[l]))
        arow = jnp.broadcast_to(jnp.pad(a_dec, (0, n_heads))[:, None], (2 * n_heads, GDN_CHUNK))
        drow = jnp.broadcast_to(jnp.pad(dn_dt_bias[l], (0, n_heads))[:, None], (2 * n_heads, GDN_CHUNK))
        abt = ab[:, :2 * n_heads].T
        gated_b = _gdn(proj_b, ab, abt, bsz, seq, dn_conv_w[l], acol, dcol, arow, drow, row(dn_norm_w[l]), n_heads)

        kv = _norm_mm(mem2, row(mem_norm_w[l]), w_ckv[l].astype(BF16), BF16, _tile(bsz * n_mem, 1024), 512)
        h = _merge_cross(h, gated_a, gated_b, proj_b, w_branch_a[l].astype(BF16), w_branch_b[l].astype(BF16),
                         w_out[l].astype(BF16), row(norm2_w[l]), w_cq[l].astype(BF16), kv, w_co[l].astype(BF16),
                         bsz, seq, n_mem, _tile(seq, 512))

        w_r = jnp.pad(jnp.concatenate([w_router_expert[l], w_router_group[l]], axis=1),
                      ((0, 0), (0, LANES - N_EXPERTS - N_GROUPS)))
        b_r = row(jnp.pad(jnp.concatenate([b_router_expert[l], b_router_group[l]]), (0, LANES - N_EXPERTS - N_GROUPS)))
        u3, idx, wt, cnt = _router(h, row(norm3_w[l]), w_r, b_r, _tile(n, 512))

        counts = cnt[0, :N_EXPERTS].astype(I32)
        padded = (counts + MOE_BLOCK - 1) // MOE_BLOCK * MOE_BLOCK
        pend = jnp.cumsum(padded)
        pstart = pend - padded
        is_expert = idx[:, 0:2, None] == jnp.arange(N_EXPERTS, dtype=I32)
        dest = jnp.sum(jnp.where(is_expert, pstart, 0), axis=-1) + idx[:, 2:4]
        n_blocks = (2 * n + N_EXPERTS * (MOE_BLOCK - 1)) // MOE_BLOCK
        block_row = jnp.arange(n_blocks, dtype=I32) * MOE_BLOCK
        block_e = jnp.minimum(jnp.sum((pend[None, :] <= block_row[:, None]).astype(I32), axis=1), N_EXPERTS - 1)
        n_used = (pend[-1:] // MOE_BLOCK).astype(I32)
        t_moe = _tile(n, 512)
        dest3 = dest.reshape(n // t_moe, 1, 2 * t_moe)
        last_block = jnp.maximum(pend // MOE_BLOCK - 1, 0).astype(I32)
        xs = _dispatch(dest3, u3, _zeroed_tail_blocks(last_block, n_blocks * MOE_BLOCK, d), t_moe)
        yb = _experts(block_e, n_used, xs, w_exp_gate[l], w_exp_up[l], w_exp_down[l])
        h = _combine(dest3, h, wt, row(norm_f_w), yb, t_moe, final_norm=(l == depth - 1))
    return h.reshape(bsz, seq, d)
```

```python
import functools
import math

import jax
import jax.numpy as jnp
from jax import lax
from jax.experimental import pallas as pl
from jax.experimental.pallas import tpu as pltpu

F32 = jnp.float32
BF16 = jnp.bfloat16
I32 = jnp.int32
HIGHEST = lax.Precision.HIGHEST

NORM_EPS = 1e-6
CONV_TAPS = 4
RNN_BLOCK = 128
RG_POWER = 8.0
HEAD_DIM = 128
GDN_CHUNK = 128
GDN_CHUNKS_PER_STEP = 2
CA_HEADS = 4
N_GROUPS = 8
GROUP_SIZE = 8
N_EXPERTS = N_GROUPS * GROUP_SIZE
MOE_BLOCK = 512
LANES = 128
SUBLANES = 8
DMA_UNROLL = 8
VMEM_LIMIT = 48 * 1024 * 1024


def _params(*semantics):
    return pltpu.CompilerParams(dimension_semantics=semantics, vmem_limit_bytes=VMEM_LIMIT)


def _dot(a, b):
    return jnp.dot(a.astype(BF16), b.astype(BF16), preferred_element_type=F32)


def _dot_nt(a, b):
    return lax.dot_general(a.astype(BF16), b.astype(BF16), (((1,), (1,)), ((), ())),
                           preferred_element_type=F32)


def _dot_tn(a, b):
    return lax.dot_general(a.astype(BF16), b.astype(BF16), (((0,), (0,)), ((), ())),
                           preferred_element_type=F32)


def _dot_f32(a, b):
    return jnp.dot(a, b, precision=HIGHEST, preferred_element_type=F32)


def _split(x):
    hi = x.astype(BF16)
    return hi, (x - hi.astype(F32)).astype(BF16)


def _dot_split(a_parts, b_parts):
    (a_hi, a_lo), (b_hi, b_lo) = a_parts, b_parts
    dot = functools.partial(jnp.dot, preferred_element_type=F32)
    return dot(a_hi, b_hi) + (dot(a_hi, b_lo) + dot(a_lo, b_hi))


def _rmsnorm(x, w):
    return x * lax.rsqrt(jnp.mean(x * x, axis=-1, keepdims=True) + NORM_EPS) * w


def _sigmoid(x):
    return 0.5 * jnp.tanh(0.5 * x) + 0.5


def _silu(x):
    half = 0.5 * x
    return half + half * jnp.tanh(half)


def _softplus(x):
    return jnp.maximum(x, 0.0) + jnp.log(1.0 + jnp.exp(-jnp.abs(x)))


def _one_minus_exp2(y, exp_y):
    return jnp.tanh(-y) * (1.0 + exp_y * exp_y)


def _gelu_tanh(x):
    return 0.5 * x * (1.0 + jnp.tanh(math.sqrt(2.0 / math.pi) * (x + 0.044715 * (x * x * x))))


def _norm_mm_kernel(x_ref, nw_ref, w_ref, o_ref, u_ref):
    @pl.when(pl.program_id(1) == 0)
    def _():
        u_ref[...] = _rmsnorm(x_ref[...], nw_ref[...]).astype(BF16)

    o_ref[...] = jnp.dot(u_ref[...], w_ref[...], preferred_element_type=F32).astype(o_ref.dtype)


def _norm_mm(x, nw, w, out_dtype, tm, tn):
    n, d = x.shape
    c = w.shape[1]
    return pl.pallas_call(
        _norm_mm_kernel,
        grid=(n // tm, c // tn),
        in_specs=[pl.BlockSpec((tm, d), lambda i, j: (i, 0)),
                  pl.BlockSpec((1, d), lambda i, j: (0, 0)),
                  pl.BlockSpec((d, tn), lambda i, j: (0, j))],
        out_specs=pl.BlockSpec((tm, tn), lambda i, j: (i, j)),
        out_shape=jax.ShapeDtypeStruct((n, c), out_dtype),
        scratch_shapes=[pltpu.VMEM((tm, d), BF16)],
        compiler_params=_params("parallel", "arbitrary"),
        name="norm_mm",
    )(x, nw, w)


def _in_proj_kernel(x_ref, nw_ref, w_ref, wab_ref, oa_ref, ob_ref, ab_ref, u_ref, *, n_a):
    j = pl.program_id(1)

    @pl.when(j == 0)
    def _():
        u_ref[...] = _rmsnorm(x_ref[...], nw_ref[...]).astype(BF16)
        ab_ref[...] = jnp.dot(u_ref[...], wab_ref[...], preferred_element_type=F32)

    @pl.when(j < n_a)
    def _():
        oa_ref[...] = jnp.dot(u_ref[...], w_ref[...], preferred_element_type=F32).astype(oa_ref.dtype)

    @pl.when(j >= n_a)
    def _():
        ob_ref[...] = jnp.dot(u_ref[...], w_ref[...], preferred_element_type=F32).astype(ob_ref.dtype)


def _in_proj(x, nw, w_cat, w_ab, c_a, tm, tn):
    n, d = x.shape
    c = w_cat.shape[1]
    n_a, n_b = c_a // tn, (c - c_a) // tn
    return pl.pallas_call(
        functools.partial(_in_proj_kernel, n_a=n_a),
        grid=(n // tm, n_a + n_b),
        in_specs=[pl.BlockSpec((tm, d), lambda i, j: (i, 0)),
                  pl.BlockSpec((1, d), lambda i, j: (0, 0)),
                  pl.BlockSpec((d, tn), lambda i, j: (0, j)),
                  pl.BlockSpec(w_ab.shape, lambda i, j: (0, 0))],
        out_specs=[pl.BlockSpec((tm, tn), lambda i, j: (i, jnp.minimum(j, n_a - 1))),
                   pl.BlockSpec((tm, tn), lambda i, j: (i, jnp.maximum(j - n_a, 0))),
                   pl.BlockSpec((tm, w_ab.shape[1]), lambda i, j: (i, 0))],
        out_shape=[jax.ShapeDtypeStruct((n, c_a), BF16), jax.ShapeDtypeStruct((n, c - c_a), BF16),
                   jax.ShapeDtypeStruct((n, w_ab.shape[1]), F32)],
        scratch_shapes=[pltpu.VMEM((tm, d), BF16)],
        compiler_params=_params("arbitrary", "arbitrary"),
        name="in_proj",
    )(x, nw, w_cat, w_ab)


def _load_conv_window(x_ref, xbuf, ts):
    @pl.when(pl.program_id(1) == 0)
    def _():
        xbuf[0:SUBLANES, :] = jnp.zeros((SUBLANES, xbuf.shape[1]), F32)

    @pl.when(pl.program_id(1) != 0)
    def _():
        xbuf[0:SUBLANES, :] = xbuf[ts:ts + SUBLANES, :]

    xbuf[SUBLANES:SUBLANES + ts, :] = x_ref[...].astype(F32)


def _causal_conv(xbuf, cw_ref, ts, first_row=0, cols=slice(None)):
    base = first_row + SUBLANES - (CONV_TAPS - 1)
    acc = cw_ref[0:1, cols] * xbuf[base:base + ts, cols]
    for k in range(1, CONV_TAPS):
        acc = acc + cw_ref[k:k + 1, cols] * xbuf[base + k:base + k + ts, cols]
    return acc


def _rglru_kernel(rx_ref, rg_ref, cw_ref, cb_ref, wa_ref, ba_ref, wx_ref, bx_ref, lam_ref, o_ref,
                  xbuf, a_ref, b_ref, carry_ref):
    ts, c = rx_ref.shape
    _load_conv_window(rx_ref, xbuf, ts)

    @pl.when(pl.program_id(1) == 0)
    def _():
        carry_ref[...] = jnp.zeros_like(carry_ref)

    xc = _causal_conv(xbuf, cw_ref, ts) + cb_ref[...]
    neg_sp = -RG_POWER * _softplus(-lam_ref[...])
    for n in range(c // RNN_BLOCK):
        sl = slice(n * RNN_BLOCK, (n + 1) * RNN_BLOCK)
        xb = xc[:, sl]
        r = _sigmoid(_dot(xb, wa_ref[n]) + ba_ref[:, sl])
        i = _sigmoid(_dot(xb, wx_ref[n]) + bx_ref[:, sl])
        log_a = neg_sp[:, sl] * r
        a = jnp.exp(log_a)
        a_ref[:, sl] = a
        b_ref[:, sl] = jnp.sqrt(_one_minus_exp2(log_a, a)) * (i * xb)

    row = lax.broadcasted_iota(I32, (SUBLANES, c), 0)

    def slab(t, carry):
        rows = pl.ds(pl.multiple_of(t * SUBLANES, SUBLANES), SUBLANES)
        a = a_ref[rows, :]
        b = b_ref[rows, :]
        for d in (1, 2, 4):
            a_sh = jnp.where(row >= d, pltpu.roll(a, d, 0), 1.0)
            b_sh = jnp.where(row >= d, pltpu.roll(b, d, 0), 0.0)
            b = a * b_sh + b
            a = a * a_sh
        h = a * carry + b
        b_ref[rows, :] = h
        return h[SUBLANES - 1:SUBLANES, :]

    carry_ref[...] = lax.fori_loop(0, ts // SUBLANES, slab, carry_ref[...])
    o_ref[...] = (_gelu_tanh(rg_ref[...].astype(F32)) * b_ref[...]).astype(o_ref.dtype)


def _rglru(proj_a, bsz, seq, cw, cb, wa, ba, wx, bx, lam, ts):
    c = cw.shape[1]
    ns = seq // ts
    full = lambda shape: pl.BlockSpec(shape, lambda b, s: (0,) * len(shape))
    return pl.pallas_call(
        _rglru_kernel,
        grid=(bsz, ns),
        in_specs=[pl.BlockSpec((ts, c), lambda b, s: (b * ns + s, 0)),
                  pl.BlockSpec((ts, c), lambda b, s: (b * ns + s, 1)),
                  full(cw.shape), full(cb.shape), full(wa.shape), full(ba.shape),
                  full(wx.shape), full(bx.shape), full(lam.shape)],
        out_specs=pl.BlockSpec((ts, c), lambda b, s: (b * ns + s, 0)),
        out_shape=jax.ShapeDtypeStruct((bsz * seq, c), BF16),
        scratch_shapes=[pltpu.VMEM((ts + SUBLANES, c), F32), pltpu.VMEM((ts, c), F32),
                        pltpu.VMEM((ts, c), F32), pltpu.VMEM((1, c), F32)],
        compiler_params=_params("parallel", "arbitrary"),
        name="rglru",
    )(proj_a, proj_a, cw, cb, wa, ba, wx, bx, lam)


def _lane_bcast(x, lane):
    return jnp.broadcast_to(x[:, lane:lane + 1], x.shape)


def _unit_lower_inverses(neg_ls, fillers):
    ts = neg_ls[0].shape[0]
    half = ts // 2
    run_filler = lambda: next(fillers, lambda: None)()
    lane = lax.broadcasted_iota(I32, (half, ts), 1)
    left = lane < half
    left_of = lambda x: jnp.where(left, x, 0.0)
    right_of = lambda x: jnp.where(left, 0.0, x)
    left_mask, right_mask = left_of(1.0).astype(BF16), right_of(1.0).astype(BF16)
    stack = lambda top, bottom: tuple(jnp.concatenate([a, b], axis=0) for a, b in zip(top, bottom))
    diag = lambda parts: stack([p * left_mask for p in parts], [p * right_mask for p in parts])
    eye_pair = ((lane & (half - 1)) == lax.broadcasted_iota(I32, (half, ts), 0)).astype(F32)

    pairs = [n[:half] + right_of(n[half:]) for n in neg_ls]
    couplings = [left_of(n[half:]) for n in neg_ls]
    ts_mats = [eye_pair + ab for ab in pairs]
    pair_parts = [_split(ab) for ab in pairs]
    ps = [_dot_split(pp, diag(pp)) for pp in pair_parts]
    run_filler()
    levels = int(math.log2(half)) - 1
    for lvl in range(levels - 1):
        p_parts = [_split(p) for p in ps]
        both = [_dot_split(stack(_split(t), pp), diag(pp)) for t, pp in zip(ts_mats, p_parts)]
        ts_mats = [t + b[:half] for t, b in zip(ts_mats, both)]
        ps = [b[half:] for b in both]
        run_filler()
    ts_mats = [t + _dot_split(_split(t), diag(_split(p))) for t, p in zip(ts_mats, ps)]
    run_filler()
    t_parts = [_split(t) for t in ts_mats]
    c_ta = [_dot_split(_split(c), diag(tp)) for c, tp in zip(couplings, t_parts)]
    run_filler()
    below = lambda parts: stack([jnp.zeros_like(p) for p in parts], parts)
    tb_c_ta = [_dot_split(tp, below(_split(m))) for tp, m in zip(t_parts, c_ta)]
    return [jnp.concatenate([left_of(t), jnp.where(left, x, t)], axis=0) for t, x in zip(ts_mats, tb_c_ta)]


def _gdn_kernel(qkv_ref, z_ref, ab_ref, abt_ref, cw_ref, acol_ref, dcol_ref, arow_ref, drow_ref, nw_ref,
                o_ref, xbuf, s_ref):
    ts = GDN_CHUNK
    n_heads = s_ref.shape[0]
    dn_w = n_heads * HEAD_DIM
    _load_conv_window(qkv_ref, xbuf, qkv_ref.shape[0])

    @pl.when(pl.program_id(1) == 0)
    def _():
        s_ref[...] = jnp.zeros_like(s_ref)

    row = lax.broadcasted_iota(I32, (ts, ts), 0)
    col = lax.broadcasted_iota(I32, (ts, ts), 1)
    lower_incl = (row >= col).astype(F32)
    upper_incl = (row <= col).astype(F32)
    head_cols = lambda h, part: slice(part * dn_w + h * HEAD_DIM, part * dn_w + (h + 1) * HEAD_DIM)
    heads = range(n_heads)

    n_chunks = qkv_ref.shape[0] // ts
    state = [dict(q=[None] * n_heads, k=[None] * n_heads, k_beta=[None] * n_heads, decay=[None] * n_heads,
                  rhs=[None] * n_heads, qd=[None] * n_heads, kd=[None] * n_heads) for _ in range(n_chunks)]

    def prepare_gates(c):
        st = state[c]
        rows = slice(c * ts, (c + 1) * ts)
        ab = ab_ref[rows, :]
        g_col = -acol_ref[...] * _softplus(ab + dcol_ref[...])
        st["beta"] = _sigmoid(ab)
        g_row = -arow_ref[...] * _softplus(abt_ref[:, rows] + drow_ref[...])
        st["cum_col"] = _dot_f32(lower_incl, g_col)
        st["cum_row"] = _dot_f32(g_row, upper_incl)
        last = st["cum_col"][ts - 1:ts, :]
        st["exp_cum"] = jnp.exp(st["cum_col"])
        st["exp_rem"] = jnp.exp(jnp.broadcast_to(last, (ts, LANES)) - st["cum_col"])
        st["exp_last"] = jnp.exp(last)

    def conv_silu(c, h, part):
        return _silu(_causal_conv(xbuf, cw_ref, ts, c * ts, head_cols(h, part)))

    def prepare_key(c, h):
        st = state[c]
        k = conv_silu(c, h, 1)
        k = k * lax.rsqrt(jnp.sum(k * k, axis=-1, keepdims=True) + NORM_EPS)
        st["k"][h], st["k_beta"][h] = k, k * _lane_bcast(st["beta"], n_heads + h)
        st["decay"][h] = jnp.exp(jnp.minimum(_lane_bcast(st["cum_col"], h) - st["cum_row"][h:h + 1, :], 0.0))

    def prepare_query_value(c, h):
        st = state[c]
        q, v = conv_silu(c, h, 0), conv_silu(c, h, 2)
        q = q * (lax.rsqrt(jnp.sum(q * q, axis=-1, keepdims=True) + NORM_EPS) * (HEAD_DIM ** -0.5))
        e_cum = _lane_bcast(st["exp_cum"], h)
        st["q"][h] = q
        st["rhs"][h] = jnp.concatenate([v * _lane_bcast(st["beta"], n_heads + h), st["k_beta"][h] * e_cum], axis=1)
        st["qd"][h] = q * e_cum
        st["kd"][h] = st["k"][h] * _lane_bcast(st["exp_rem"], h)

    def key_thunks(c):
        if c >= n_chunks:
            return []
        return [functools.partial(prepare_gates, c)] + [functools.partial(prepare_key, c, h) for h in heads]

    def solve(c):
        st = state[c]
        kks = [_dot_nt(st["k_beta"][h], st["k"][h]) for h in heads]
        neg_ls = [jnp.where(row > col, -(kks[h] * st["decay"][h]), 0.0) for h in heads]
        thunks = [functools.partial(prepare_query_value, c, h) for h in heads] + key_thunks(c + 1)
        slots = int(math.log2(ts))
        per_slot = -(-len(thunks) // slots)
        groups = [thunks[i:i + per_slot] for i in range(0, len(thunks), per_slot)]
        fillers = iter([functools.partial(lambda g: [t() for t in g], g) for g in groups])
        t_mats = _unit_lower_inverses(neg_ls, fillers)
        for group in fillers:
            group()
        qks = [_dot_nt(st["q"][h], st["k"][h]) for h in heads]
        st["intra"] = [jnp.where(row >= col, qks[h] * st["decay"][h], 0.0) for h in heads]
        st["uw"] = [_dot(t_mats[h], st["rhs"][h]) for h in heads]

    def advance(c):
        st = state[c]
        rows = slice(c * ts, (c + 1) * ts)
        ws_qs = [_dot(jnp.concatenate([st["uw"][h][:, HEAD_DIM:], st["qd"][h]], axis=0), s_ref[h]) for h in heads]
        v_news = [st["uw"][h][:, :HEAD_DIM] - ws_qs[h][:ts] for h in heads]
        mixed = [_dot(jnp.concatenate([st["intra"][h], st["kd"][h].T], axis=0), v_news[h]) for h in heads]
        for h in heads:
            s_ref[h] = (s_ref[h] * _lane_bcast(jnp.broadcast_to(st["exp_last"], (HEAD_DIM, LANES)), h)
                        + mixed[h][ts:])
            o = _rmsnorm(ws_qs[h][ts:] + mixed[h][:ts], nw_ref[...]) * _silu(z_ref[rows, head_cols(h, 0)].astype(F32))
            o_ref[rows, head_cols(h, 0)] = o.astype(o_ref.dtype)

    for thunk in key_thunks(0):
        thunk()
    for c in range(n_chunks):
        solve(c)
        advance(c)


def _gdn(proj_b, ab, abt, bsz, seq, cw, acol, dcol, arow, drow, nw, n_heads):
    ts = GDN_CHUNK * GDN_CHUNKS_PER_STEP
    assert seq % ts == 0
    ns = seq // ts
    dn_w = n_heads * HEAD_DIM
    full = lambda shape: pl.BlockSpec(shape, lambda b, s: (0,) * len(shape))
    return pl.pallas_call(
        _gdn_kernel,
        grid=(bsz, ns),
        in_specs=[pl.BlockSpec((ts, 3 * dn_w), lambda b, s: (b * ns + s, 0)),
                  pl.BlockSpec((ts, dn_w), lambda b, s: (b * ns + s, 3)),
                  pl.BlockSpec((ts, LANES), lambda b, s: (b * ns + s, 0)),
                  pl.BlockSpec((2 * n_heads, ts), lambda b, s: (0, b * ns + s)),
                  full(cw.shape), full(acol.shape), full(dcol.shape), full(arow.shape), full(drow.shape),
                  full(nw.shape)],
        out_specs=pl.BlockSpec((ts, dn_w), lambda b, s: (b * ns + s, 0)),
        out_shape=jax.ShapeDtypeStruct((bsz * seq, dn_w), BF16),
        scratch_shapes=[pltpu.VMEM((ts + SUBLANES, 3 * dn_w), F32),
                        pltpu.VMEM((n_heads, HEAD_DIM, HEAD_DIM), F32)],
        compiler_params=_params("parallel", "arbitrary"),
        name="gdn",
    )(proj_b, proj_b, ab, abt, cw, acol, dcol, arow, drow, nw)


def _merge_cross_kernel(x_ref, ya_ref, yb_ref, ga_ref, gb_ref, wa_ref, wb_ref, wo_ref,
                        nw_ref, wq_ref, kv_ref, wco_ref, o_ref):
    y_a = jnp.dot(ya_ref[...], wa_ref[...], preferred_element_type=F32)
    y_b = jnp.dot(yb_ref[...], wb_ref[...], preferred_element_type=F32)
    m = _sigmoid(ga_ref[...].astype(F32)) * y_a + _sigmoid(gb_ref[...].astype(F32)) * y_b
    x = x_ref[...] + _dot(m, wo_ref[...])

    d = x.shape[1]
    hd = d // CA_HEADS
    q = _dot(_rmsnorm(x, nw_ref[...]), wq_ref[...])
    outs = []
    for h in range(CA_HEADS):
        k_h = kv_ref[:, h * hd:(h + 1) * hd]
        v_h = kv_ref[:, d + h * hd:d + (h + 1) * hd]
        s = _dot_nt(q[:, h * hd:(h + 1) * hd], k_h) * (hd ** -0.5)
        s = s - jnp.max(s, axis=-1, keepdims=True)
        e = jnp.exp(s)
        p = e / jnp.sum(e, axis=-1, keepdims=True)
        outs.append(_dot(p, v_h))
    o_ref[...] = x + _dot(jnp.concatenate(outs, axis=1), wco_ref[...])


def _merge_cross(x, gated_a, gated_b, proj_b, w_a, w_b, w_o, nw, w_q, kv, w_co, bsz, seq, n_mem, ts):
    n, d = x.shape
    ns = seq // ts
    full = lambda shape: pl.BlockSpec(shape, lambda b, s: (0,) * len(shape))
    rows = lambda width, col: pl.BlockSpec((ts, width), lambda b, s: (b * ns + s, col))
    return pl.pallas_call(
        _merge_cross_kernel,
        grid=(bsz, ns),
        in_specs=[rows(d, 0), rows(gated_a.shape[1], 0), rows(gated_b.shape[1], 0), rows(d, 4), rows(d, 5),
                  full(w_a.shape), full(w_b.shape), full(w_o.shape), full(nw.shape), full(w_q.shape),
                  pl.BlockSpec((n_mem, 2 * d), lambda b, s: (b, 0)),
                  full(w_co.shape)],
        out_specs=rows(d, 0),
        out_shape=jax.ShapeDtypeStruct((n, d), F32),
        compiler_params=_params("parallel", "parallel"),
        name="merge_cross",
    )(x, gated_a, gated_b, proj_b, proj_b, w_a, w_b, w_o, nw, w_q, kv, w_co)


def _router_kernel(h_ref, nw_ref, wr_ref, br_ref, u_ref, idx_ref, wt_ref, cnt_ref, base_ref):
    t = h_ref.shape[0]

    @pl.when(pl.program_id(0) == 0)
    def _():
        base_ref[...] = jnp.zeros_like(base_ref)

    u = _rmsnorm(h_ref[...], nw_ref[...])
    u_ref[...] = u
    logits = _dot_split(_split(u), _split(wr_ref[...])) + br_ref[...]
    lane = lax.broadcasted_iota(I32, (t, LANES), 1)
    lanef = lane.astype(F32)
    big = float(LANES)
    neg = -jnp.inf

    lg = jnp.where((lane >= N_EXPERTS) & (lane < N_EXPERTS + N_GROUPS), logits, neg)
    gmax = jnp.max(lg, axis=-1, keepdims=True)
    grp = jnp.min(jnp.where(lg == gmax, lanef - float(N_EXPERTS), big), axis=-1, keepdims=True)
    p_grp = 1.0 / jnp.sum(jnp.exp(lg - gmax), axis=-1, keepdims=True)

    in_grp = (lane < N_EXPERTS) & ((lane // GROUP_SIZE).astype(F32) == grp)
    le = jnp.where(in_grp, logits, neg)
    m1 = jnp.max(le, axis=-1, keepdims=True)
    i1 = jnp.min(jnp.where(le == m1, lanef, big), axis=-1, keepdims=True)
    le2 = jnp.where(lanef == i1, neg, le)
    m2 = jnp.max(le2, axis=-1, keepdims=True)
    i2 = jnp.min(jnp.where(le2 == m2, lanef, big), axis=-1, keepdims=True)
    ratio = jnp.exp(m2 - m1)
    p1 = 1.0 / (1.0 + ratio)
    p2 = ratio * p1

    oh1 = lanef == i1
    oh2 = lanef == i2
    onehot = jnp.where(oh1 | oh2, 1.0, 0.0)
    r_i = lax.broadcasted_iota(I32, (t, t), 0)
    c_i = lax.broadcasted_iota(I32, (t, t), 1)
    before = jnp.where(r_i > c_i, 1.0, 0.0)
    rank = _dot(before, onehot) + base_ref[...]
    r1 = jnp.sum(jnp.where(oh1, rank, 0.0), axis=-1, keepdims=True)
    r2 = jnp.sum(jnp.where(oh2, rank, 0.0), axis=-1, keepdims=True)
    base_ref[...] = base_ref[...] + jnp.sum(onehot, axis=0, keepdims=True)

    idx = jnp.where(lane == 0, i1, jnp.where(lane == 1, i2, jnp.where(lane == 2, r1, jnp.where(lane == 3, r2, 0.0))))
    idx_ref[...] = idx.astype(I32)
    wt_ref[...] = jnp.where(lane == 0, p_grp * p1, jnp.where(lane == 1, p_grp * p2, 0.0))
    cnt_ref[...] = jnp.broadcast_to(base_ref[...], cnt_ref.shape)


def _router(h, nw, wr, br, t):
    n, d = h.shape
    full = lambda shape: pl.BlockSpec(shape, lambda i: (0,) * len(shape))
    return pl.pallas_call(
        _router_kernel,
        grid=(n // t,),
        in_specs=[pl.BlockSpec((t, d), lambda i: (i, 0)), full(nw.shape), full(wr.shape), full(br.shape)],
        out_specs=[pl.BlockSpec((t, d), lambda i: (i, 0)),
                   pl.BlockSpec((t, LANES), lambda i: (i, 0)),
                   pl.BlockSpec((t, LANES), lambda i: (i, 0)),
                   pl.BlockSpec((SUBLANES, LANES), lambda i: (0, 0))],
        out_shape=[jax.ShapeDtypeStruct((n, d), F32),
                   jax.ShapeDtypeStruct((n, LANES), I32),
                   jax.ShapeDtypeStruct((n, LANES), F32),
                   jax.ShapeDtypeStruct((SUBLANES, LANES), F32)],
        scratch_shapes=[pltpu.VMEM((1, LANES), F32)],
        compiler_params=_params("arbitrary"),
        name="router",
    )(h, nw, wr, br)


def _zero_block_kernel(last_ref, o_ref):
    del last_ref
    o_ref[...] = jnp.zeros_like(o_ref)


def _zeroed_tail_blocks(last_block, rows, d):
    grid_spec = pltpu.PrefetchScalarGridSpec(
        num_scalar_prefetch=1,
        grid=(last_block.shape[0],),
        in_specs=[],
        out_specs=pl.BlockSpec((MOE_BLOCK, d), lambda e, last: (last[e], 0)),
    )
    return pl.pallas_call(
        _zero_block_kernel,
        grid_spec=grid_spec,
        out_shape=jax.ShapeDtypeStruct((rows, d), F32),
        compiler_params=_params("arbitrary"),
        name="moe_zero_tails",
    )(last_block)


def _row_copy(src, src_row, dst, dst_row, sem):
    return pltpu.make_async_copy(src.at[pl.ds(src_row, 1)], dst.at[pl.ds(dst_row, 1)], sem)


def _dispatch_kernel(dest_ref, u_ref, xs_in_ref, xs_ref, sem):
    del xs_in_ref
    t = u_ref.shape[0]

    def start(r, carry):
        _row_copy(u_ref, r, xs_ref, dest_ref[0, 0, 2 * r], sem).start()
        _row_copy(u_ref, r, xs_ref, dest_ref[0, 0, 2 * r + 1], sem).start()
        return carry

    def wait(r, carry):
        _row_copy(u_ref, r, xs_ref, dest_ref[0, 0, 2 * r], sem).wait()
        _row_copy(u_ref, r, xs_ref, dest_ref[0, 0, 2 * r + 1], sem).wait()
        return carry

    lax.fori_loop(0, t, start, 0, unroll=DMA_UNROLL)
    lax.fori_loop(0, t, wait, 0, unroll=DMA_UNROLL)


def _dispatch(dest3, u, xs_zero, t):
    n, d = u.shape
    return pl.pallas_call(
        _dispatch_kernel,
        grid=(n // t,),
        in_specs=[pl.BlockSpec((1, 1, 2 * t), lambda i: (i, 0, 0), memory_space=pltpu.SMEM),
                  pl.BlockSpec((t, d), lambda i: (i, 0)),
                  pl.BlockSpec(memory_space=pl.ANY)],
        out_specs=pl.BlockSpec(memory_space=pl.ANY),
        out_shape=jax.ShapeDtypeStruct(xs_zero.shape, xs_zero.dtype),
        scratch_shapes=[pltpu.SemaphoreType.DMA(())],
        input_output_aliases={2: 0},
        compiler_params=_params("arbitrary"),
        name="moe_dispatch",
    )(dest3, u, xs_zero)


def _expert_kernel(be_ref, nu_ref, xs_ref, wg_ref, wu_ref, wd_ref, y_ref, wg_bf, wu_bf, wd_bf):
    j = pl.program_id(0)
    used = j < nu_ref[0]
    new_expert = (j == 0) | (be_ref[j] != be_ref[jnp.maximum(j - 1, 0)])

    @pl.when(used & new_expert)
    def _():
        wg_bf[...] = wg_ref[0].astype(BF16)
        wu_bf[...] = wu_ref[0].astype(BF16)
        wd_bf[...] = wd_ref[0].astype(BF16)

    @pl.when(used)
    def _():
        x = xs_ref[...].astype(BF16)
        hid = _silu(jnp.dot(x, wg_bf[...], preferred_element_type=F32)) * jnp.dot(
            x, wu_bf[...], preferred_element_type=F32)
        y_ref[...] = _dot(hid, wd_bf[...])

    @pl.when(jnp.logical_not(used))
    def _():
        y_ref[...] = jnp.zeros_like(y_ref)


def _experts(block_e, n_used, xs, w_gate, w_up, w_down):
    rows, d = xs.shape
    de = w_gate.shape[2]
    grid_spec = pltpu.PrefetchScalarGridSpec(
        num_scalar_prefetch=2,
        grid=(rows // MOE_BLOCK,),
        in_specs=[pl.BlockSpec((MOE_BLOCK, d), lambda j, be, nu: (jnp.minimum(j, nu[0] - 1), 0)),
                  pl.BlockSpec((1, d, de), lambda j, be, nu: (be[j], 0, 0)),
                  pl.BlockSpec((1, d, de), lambda j, be, nu: (be[j], 0, 0)),
                  pl.BlockSpec((1, de, d), lambda j, be, nu: (be[j], 0, 0))],
        out_specs=pl.BlockSpec((MOE_BLOCK, d), lambda j, be, nu: (j, 0)),
        scratch_shapes=[pltpu.VMEM((d, de), BF16), pltpu.VMEM((d, de), BF16), pltpu.VMEM((de, d), BF16)],
    )
    return pl.pallas_call(
        _expert_kernel,
        grid_spec=grid_spec,
        out_shape=jax.ShapeDtypeStruct((rows, d), F32),
        compiler_params=_params("arbitrary"),
        name="moe_experts",
    )(block_e, n_used, xs, w_gate, w_up, w_down)


def _combine_kernel(dest_ref, h_ref, wt_ref, nw_ref, y_ref, o_ref, buf, sem, *, final_norm):
    t = h_ref.shape[0]

    def start(r, carry):
        _row_copy(y_ref, dest_ref[0, 0, 2 * r], buf.at[0], r, sem).start()
        _row_copy(y_ref, dest_ref[0, 0, 2 * r + 1], buf.at[1], r, sem).start()
        return carry

    def wait(r, carry):
        _row_copy(y_ref, dest_ref[0, 0, 2 * r], buf.at[0], r, sem).wait()
        _row_copy(y_ref, dest_ref[0, 0, 2 * r + 1], buf.at[1], r, sem).wait()
        return carry

    lax.fori_loop(0, t, start, 0, unroll=DMA_UNROLL)
    lax.fori_loop(0, t, wait, 0, unroll=DMA_UNROLL)
    wt = wt_ref[...]
    moe = buf[0] * wt[:, 0:1] + buf[1] * wt[:, 1:2]
    out = h_ref[...] + moe
    o_ref[...] = _rmsnorm(out, nw_ref[...]) if final_norm else out


def _combine(dest3, h, wt, nw, y, t, final_norm):
    n, d = h.shape
    return pl.pallas_call(
        functools.partial(_combine_kernel, final_norm=final_norm),
        grid=(n // t,),
        in_specs=[pl.BlockSpec((1, 1, 2 * t), lambda i: (i, 0, 0), memory_space=pltpu.SMEM),
                  pl.BlockSpec((t, d), lambda i: (i, 0)),
                  pl.BlockSpec((t, LANES), lambda i: (i, 0)),
                  pl.BlockSpec((1, d), lambda i: (0, 0)),
                  pl.BlockSpec(memory_space=pl.ANY)],
        out_specs=pl.BlockSpec((t, d), lambda i: (i, 0)),
        out_shape=jax.ShapeDtypeStruct((n, d), F32),
        scratch_shapes=[pltpu.VMEM((2, t, d), F32), pltpu.SemaphoreType.DMA(())],
        compiler_params=_params("arbitrary"),
        name="moe_combine",
    )(dest3, h, wt, nw, y)


def _tile(n, pref):
    return pref if n % pref == 0 else n


def kernel(x, mem, norm1_w, w_in, rnn_conv_w, rnn_conv_b, rglru_wa, rglru_ba, rglru_wx, rglru_bx, rglru_lambda, w_branch_a, dn_conv_w, dn_a_log, dn_dt_bias, dn_norm_w, w_branch_b, w_out, norm2_w, mem_norm_w, w_cq, w_ckv, w_co, norm3_w, w_router_group, b_router_group, w_router_expert, b_router_expert, w_exp_gate, w_exp_up, w_exp_down, norm_f_w):
    bsz, seq, d = x.shape
    n = bsz * seq
    n_mem = mem.shape[1]
    depth = w_in.shape[0]
    d_rnn = rnn_conv_w.shape[2]
    n_heads = dn_a_log.shape[1]
    dn_w = n_heads * HEAD_DIM
    row = lambda v: v.reshape(1, -1).astype(F32)

    h = x.reshape(n, d)
    mem2 = mem.reshape(bsz * n_mem, d)
    tm = _tile(n, 2048)
    for l in range(depth):
        o_rg, o_qkv, o_z = d_rnn, 2 * d_rnn, 2 * d_rnn + 3 * dn_w
        o_a = o_z + dn_w
        o_ga = o_a + 2 * n_heads
        wi = w_in[l]
        w_cat = jnp.concatenate([wi[:, :o_a], wi[:, o_ga:]], axis=1).astype(BF16)
        w_ab = jnp.pad(wi[:, o_a:o_ga], ((0, 0), (0, LANES - 2 * n_heads))).astype(BF16)
        proj_a, proj_b, ab = _in_proj(h, row(norm1_w[l]), w_cat, w_ab, o_qkv, tm, 512)

        gated_a = _rglru(proj_a, bsz, seq, rnn_conv_w[l], row(rnn_conv_b[l]), rglru_wa[l].astype(BF16),
                         row(rglru_ba[l]), rglru_wx[l].astype(BF16), row(rglru_bx[l]), row(rglru_lambda[l]),
                         _tile(seq, 512))

        a_dec = jnp.exp(dn_a_log[l].astype(F32))
        pad_h = lambda v: jnp.pad(v, (0, LANES - n_heads))
        acol, dcol = row(pad_h(a_dec)), row(pad_h(dn_dt_bias[l]))
        arow = jnp.broadcast_to(jnp.pad(a_dec, (0, n_heads))[:, None], (2 * n_heads, GDN_CHUNK))
        drow = jnp.broadcast_to(jnp.pad(dn_dt_bias[l], (0, n_heads))[:, None], (2 * n_heads, GDN_CHUNK))
        abt = ab[:, :2 * n_heads].T
        gated_b = _gdn(proj_b, ab, abt, bsz, seq, dn_conv_w[l], acol, dcol, arow, drow, row(dn_norm_w[l]), n_heads)

        kv = _norm_mm(mem2, row(mem_norm_w[l]), w_ckv[l].astype(BF16), BF16, _tile(bsz * n_mem, 1024), 512)
        h = _merge_cross(h, gated_a, gated_b, proj_b, w_branch_a[l].astype(BF16), w_branch_b[l].astype(BF16),
                         w_out[l].astype(BF16), row(norm2_w[l]), w_cq[l].astype(BF16), kv, w_co[l].astype(BF16),
                         bsz, seq, n_mem, _tile(seq, 512))

        w_r = jnp.pad(jnp.concatenate([w_router_expert[l], w_router_group[l]], axis=1),
                      ((0, 0), (0, LANES - N_EXPERTS - N_GROUPS)))
        b_r = row(jnp.pad(jnp.concatenate([b_router_expert[l], b_router_group[l]]), (0, LANES - N_EXPERTS - N_GROUPS)))
        u3, idx, wt, cnt = _router(h, row(norm3_w[l]), w_r, b_r, _tile(n, 512))

        counts = cnt[0, :N_EXPERTS].astype(I32)
        padded = (counts + MOE_BLOCK - 1) // MOE_BLOCK * MOE_BLOCK
        pend = jnp.cumsum(padded)
        pstart = pend - padded
        is_expert = idx[:, 0:2, None] == jnp.arange(N_EXPERTS, dtype=I32)
        dest = jnp.sum(jnp.where(is_expert, pstart, 0), axis=-1) + idx[:, 2:4]
        n_blocks = (2 * n + N_EXPERTS * (MOE_BLOCK - 1)) // MOE_BLOCK
        block_row = jnp.arange(n_blocks, dtype=I32) * MOE_BLOCK
        block_e = jnp.minimum(jnp.sum((pend[None, :] <= block_row[:, None]).astype(I32), axis=1), N_EXPERTS - 1)
        n_used = (pend[-1:] // MOE_BLOCK).astype(I32)
        t_moe = _tile(n, 512)
        dest3 = dest.reshape(n // t_moe, 1, 2 * t_moe)
        last_block = jnp.maximum(pend // MOE_BLOCK - 1, 0).astype(I32)
        xs = _dispatch(dest3, u3, _zeroed_tail_blocks(last_block, n_blocks * MOE_BLOCK, d), t_moe)
        yb = _experts(block_e, n_used, xs, w_exp_gate[l], w_exp_up[l], w_exp_down[l])
        h = _combine(dest3, h, wt, row(norm_f_w), yb, t_moe, final_norm=(l == depth - 1))
    return h.reshape(bsz, seq, d)
```

```python
import functools
import math

import jax
import jax.numpy as jnp
from jax import lax
from jax.experimental import pallas as pl
from jax.experimental.pallas import tpu as pltpu

F32 = jnp.float32
BF16 = jnp.bfloat16
I32 = jnp.int32
HIGHEST = lax.Precision.HIGHEST

NORM_EPS = 1e-6
CONV_TAPS = 4
RNN_BLOCK = 128
RG_POWER = 8.0
HEAD_DIM = 128
GDN_CHUNK = 128
GDN_CHUNKS_PER_STEP = 2
CA_HEADS = 4
N_GROUPS = 8
GROUP_SIZE = 8
N_EXPERTS = N_GROUPS * GROUP_SIZE
MOE_BLOCK = 512
LANES = 128
SUBLANES = 8
DMA_UNROLL = 8
VMEM_LIMIT = 48 * 1024 * 1024


def _params(*semantics):
    return pltpu.CompilerParams(dimension_semantics=semantics, vmem_limit_bytes=VMEM_LIMIT)


def _dot(a, b):
    return jnp.dot(a.astype(BF16), b.astype(BF16), preferred_element_type=F32)


def _dot_nt(a, b):
    return lax.dot_general(a.astype(BF16), b.astype(BF16), (((1,), (1,)), ((), ())),
                           preferred_element_type=F32)


def _dot_tn(a, b):
    return lax.dot_general(a.astype(BF16), b.astype(BF16), (((0,), (0,)), ((), ())),
                           preferred_element_type=F32)


def _dot_f32(a, b):
    return jnp.dot(a, b, precision=HIGHEST, preferred_element_type=F32)


def _split(x):
    hi = x.astype(BF16)
    return hi, (x - hi.astype(F32)).astype(BF16)


def _dot_split(a_parts, b_parts):
    (a_hi, a_lo), (b_hi, b_lo) = a_parts, b_parts
    dot = functools.partial(jnp.dot, preferred_element_type=F32)
    return dot(a_hi, b_hi) + (dot(a_hi, b_lo) + dot(a_lo, b_hi))


def _rmsnorm(x, w):
    return x * lax.rsqrt(jnp.mean(x * x, axis=-1, keepdims=True) + NORM_EPS) * w


def _sigmoid(x):
    return 0.5 * jnp.tanh(0.5 * x) + 0.5


def _silu(x):
    half = 0.5 * x
    return half + half * jnp.tanh(half)


def _softplus(x):
    return jnp.maximum(x, 0.0) + jnp.log(1.0 + jnp.exp(-jnp.abs(x)))


def _one_minus_exp2(y, exp_y):
    return jnp.tanh(-y) * (1.0 + exp_y * exp_y)


def _gelu_tanh(x):
    return 0.5 * x * (1.0 + jnp.tanh(math.sqrt(2.0 / math.pi) * (x + 0.044715 * (x * x * x))))


def _norm_mm_kernel(x_ref, nw_ref, w_ref, o_ref, u_ref):
    @pl.when(pl.program_id(1) == 0)
    def _():
        u_ref[...] = _rmsnorm(x_ref[...], nw_ref[...]).astype(BF16)

    o_ref[...] = jnp.dot(u_ref[...], w_ref[...], preferred_element_type=F32).astype(o_ref.dtype)


def _norm_mm(x, nw, w, out_dtype, tm, tn):
    n, d = x.shape
    c = w.shape[1]
    return pl.pallas_call(
        _norm_mm_kernel,
        grid=(n // tm, c // tn),
        in_specs=[pl.BlockSpec((tm, d), lambda i, j: (i, 0)),
                  pl.BlockSpec((1, d), lambda i, j: (0, 0)),
                  pl.BlockSpec((d, tn), lambda i, j: (0, j))],
        out_specs=pl.BlockSpec((tm, tn), lambda i, j: (i, j)),
        out_shape=jax.ShapeDtypeStruct((n, c), out_dtype),
        scratch_shapes=[pltpu.VMEM((tm, d), BF16)],
        compiler_params=_params("parallel", "arbitrary"),
        name="norm_mm",
    )(x, nw, w)


def _in_proj_kernel(x_ref, nw_ref, w_ref, wab_ref, oa_ref, ob_ref, ab_ref, u_ref, *, n_a):
    j = pl.program_id(1)

    @pl.when(j == 0)
    def _():
        u_ref[...] = _rmsnorm(x_ref[...], nw_ref[...]).astype(BF16)
        ab_ref[...] = jnp.dot(u_ref[...], wab_ref[...], preferred_element_type=F32)

    @pl.when(j < n_a)
    def _():
        oa_ref[...] = jnp.dot(u_ref[...], w_ref[...], preferred_element_type=F32).astype(oa_ref.dtype)

    @pl.when(j >= n_a)
    def _():
        ob_ref[...] = jnp.dot(u_ref[...], w_ref[...], preferred_element_type=F32).astype(ob_ref.dtype)


def _in_proj(x, nw, w_cat, w_ab, c_a, tm, tn):
    n, d = x.shape
    c = w_cat.shape[1]
    n_a, n_b = c_a // tn, (c - c_a) // tn
    return pl.pallas_call(
        functools.partial(_in_proj_kernel, n_a=n_a),
        grid=(n // tm, n_a + n_b),
        in_specs=[pl.BlockSpec((tm, d), lambda i, j: (i, 0)),
                  pl.BlockSpec((1, d), lambda i, j: (0, 0)),
                  pl.BlockSpec((d, tn), lambda i, j: (0, j)),
                  pl.BlockSpec(w_ab.shape, lambda i, j: (0, 0))],
        out_specs=[pl.BlockSpec((tm, tn), lambda i, j: (i, jnp.minimum(j, n_a - 1))),
                   pl.BlockSpec((tm, tn), lambda i, j: (i, jnp.maximum(j - n_a, 0))),
                   pl.BlockSpec((tm, w_ab.shape[1]), lambda i, j: (i, 0))],
        out_shape=[jax.ShapeDtypeStruct((n, c_a), BF16), jax.ShapeDtypeStruct((n, c - c_a), BF16),
                   jax.ShapeDtypeStruct((n, w_ab.shape[1]), F32)],
        scratch_shapes=[pltpu.VMEM((tm, d), BF16)],
        compiler_params=_params("arbitrary", "arbitrary"),
        name="in_proj",
    )(x, nw, w_cat, w_ab)


def _load_conv_window(x_ref, xbuf, ts):
    @pl.when(pl.program_id(1) == 0)
    def _():
        xbuf[0:SUBLANES, :] = jnp.zeros((SUBLANES, xbuf.shape[1]), F32)

    @pl.when(pl.program_id(1) != 0)
    def _():
        xbuf[0:SUBLANES, :] = xbuf[ts:ts + SUBLANES, :]

    xbuf[SUBLANES:SUBLANES + ts, :] = x_ref[...].astype(F32)


def _causal_conv(xbuf, cw_ref, ts, first_row=0, cols=slice(None)):
    base = first_row + SUBLANES - (CONV_TAPS - 1)
    acc = cw_ref[0:1, cols] * xbuf[base:base + ts, cols]
    for k in range(1, CONV_TAPS):
        acc = acc + cw_ref[k:k + 1, cols] * xbuf[base + k:base + k + ts, cols]
    return acc


def _rglru_kernel(rx_ref, rg_ref, cw_ref, cb_ref, wa_ref, ba_ref, wx_ref, bx_ref, lam_ref, o_ref,
                  xbuf, a_ref, b_ref, carry_ref):
    ts, c = rx_ref.shape
    _load_conv_window(rx_ref, xbuf, ts)

    @pl.when(pl.program_id(1) == 0)
    def _():
        carry_ref[...] = jnp.zeros_like(carry_ref)

    xc = _causal_conv(xbuf, cw_ref, ts) + cb_ref[...]
    neg_sp = -RG_POWER * _softplus(-lam_ref[...])
    for n in range(c // RNN_BLOCK):
        sl = slice(n * RNN_BLOCK, (n + 1) * RNN_BLOCK)
        xb = xc[:, sl]
        r = _sigmoid(_dot(xb, wa_ref[n]) + ba_ref[:, sl])
        i = _sigmoid(_dot(xb, wx_ref[n]) + bx_ref[:, sl])
        log_a = neg_sp[:, sl] * r
        a = jnp.exp(log_a)
        a_ref[:, sl] = a
        b_ref[:, sl] = jnp.sqrt(_one_minus_exp2(log_a, a)) * (i * xb)

    row = lax.broadcasted_iota(I32, (SUBLANES, c), 0)

    def slab(t, carry):
        rows = pl.ds(pl.multiple_of(t * SUBLANES, SUBLANES), SUBLANES)
        a = a_ref[rows, :]
        b = b_ref[rows, :]
        for d in (1, 2, 4):
            a_sh = jnp.where(row >= d, pltpu.roll(a, d, 0), 1.0)
            b_sh = jnp.where(row >= d, pltpu.roll(b, d, 0), 0.0)
            b = a * b_sh + b
            a = a * a_sh
        h = a * carry + b
        b_ref[rows, :] = h
        return h[SUBLANES - 1:SUBLANES, :]

    carry_ref[...] = lax.fori_loop(0, ts // SUBLANES, slab, carry_ref[...])
    o_ref[...] = (_gelu_tanh(rg_ref[...].astype(F32)) * b_ref[...]).astype(o_ref.dtype)


def _rglru(proj_a, bsz, seq, cw, cb, wa, ba, wx, bx, lam, ts):
    c = cw.shape[1]
    ns = seq // ts
    full = lambda shape: pl.BlockSpec(shape, lambda b, s: (0,) * len(shape))
    return pl.pallas_call(
        _rglru_kernel,
        grid=(bsz, ns),
        in_specs=[pl.BlockSpec((ts, c), lambda b, s: (b * ns + s, 0)),
                  pl.BlockSpec((ts, c), lambda b, s: (b * ns + s, 1)),
                  full(cw.shape), full(cb.shape), full(wa.shape), full(ba.shape),
                  full(wx.shape), full(bx.shape), full(lam.shape)],
        out_specs=pl.BlockSpec((ts, c), lambda b, s: (b * ns + s, 0)),
        out_shape=jax.ShapeDtypeStruct((bsz * seq, c), BF16),
        scratch_shapes=[pltpu.VMEM((ts + SUBLANES, c), F32), pltpu.VMEM((ts, c), F32),
                        pltpu.VMEM((ts, c), F32), pltpu.VMEM((1, c), F32)],
        compiler_params=_params("parallel", "arbitrary"),
        name="rglru",
    )(proj_a, proj_a, cw, cb, wa, ba, wx, bx, lam)


def _lane_bcast(x, lane):
    return jnp.broadcast_to(x[:, lane:lane + 1], x.shape)


def _unit_lower_inverses(neg_ls, fillers):
    ts = neg_ls[0].shape[0]
    half = ts // 2
    run_filler = lambda: next(fillers, lambda: None)()
    lane = lax.broadcasted_iota(I32, (half, ts), 1)
    left = lane < half
    left_of = lambda x: jnp.where(left, x, 0.0)
    right_of = lambda x: jnp.where(left, 0.0, x)
    left_mask, right_mask = left_of(1.0).astype(BF16), right_of(1.0).astype(BF16)
    stack = lambda top, bottom: tuple(jnp.concatenate([a, b], axis=0) for a, b in zip(top, bottom))
    diag = lambda parts: stack([p * left_mask for p in parts], [p * right_mask for p in parts])
    eye_pair = ((lane & (half - 1)) == lax.broadcasted_iota(I32, (half, ts), 0)).astype(F32)

    pairs = [n[:half] + right_of(n[half:]) for n in neg_ls]
    couplings = [left_of(n[half:]) for n in neg_ls]
    ts_mats = [eye_pair + ab for ab in pairs]
    pair_parts = [_split(ab) for ab in pairs]
    ps = [_dot_split(pp, diag(pp)) for pp in pair_parts]
    run_filler()
    levels = int(math.log2(half)) - 1
    for lvl in range(levels - 1):
        p_parts = [_split(p) for p in ps]
        both = [_dot_split(stack(_split(t), pp), diag(pp)) for t, pp in zip(ts_mats, p_parts)]
        ts_mats = [t + b[:half] for t, b in zip(ts_mats, both)]
        ps = [b[half:] for b in both]
        run_filler()
    ts_mats = [t + _dot_split(_split(t), diag(_split(p))) for t, p in zip(ts_mats, ps)]
    run_filler()
    t_parts = [_split(t) for t in ts_mats]
    c_ta = [_dot_split(_split(c), diag(tp)) for c, tp in zip(couplings, t_parts)]
    run_filler()
    below = lambda parts: stack([jnp.zeros_like(p) for p in parts], parts)
    tb_c_ta = [_dot_split(tp, below(_split(m))) for tp, m in zip(t_parts, c_ta)]
    return [jnp.concatenate([left_of(t), jnp.where(left, x, t)], axis=0) for t, x in zip(ts_mats, tb_c_ta)]


def _gdn_kernel(qkv_ref, z_ref, ab_ref, abt_ref, cw_ref, acol_ref, dcol_ref, arow_ref, drow_ref, nw_ref,
                o_ref, xbuf, s_ref):
    ts = GDN_CHUNK
    n_heads = s_ref.shape[0]
    dn_w = n_heads * HEAD_DIM
    _load_conv_window(qkv_ref, xbuf, qkv_ref.shape[0])

    @pl.when(pl.program_id(1) == 0)
    def _():
        s_ref[...] = jnp.zeros_like(s_ref)

    row = lax.broadcasted_iota(I32, (ts, ts), 0)
    col = lax.broadcasted_iota(I32, (ts, ts), 1)
    lower_incl = (row >= col).astype(F32)
    upper_incl = (row <= col).astype(F32)
    head_cols = lambda h, part: slice(part * dn_w + h * HEAD_DIM, part * dn_w + (h + 1) * HEAD_DIM)
    heads = range(n_heads)

    n_chunks = qkv_ref.shape[0] // ts
    state = [dict(q=[None] * n_heads, k=[None] * n_heads, k_beta=[None] * n_heads, decay=[None] * n_heads,
                  rhs=[None] * n_heads, qd=[None] * n_heads, kd=[None] * n_heads) for _ in range(n_chunks)]

    def prepare_gates(c):
        st = state[c]
        rows = slice(c * ts, (c + 1) * ts)
        ab = ab_ref[rows, :]
        g_col = -acol_ref[...] * _softplus(ab + dcol_ref[...])
        st["beta"] = _sigmoid(ab)
        g_row = -arow_ref[...] * _softplus(abt_ref[:, rows] + drow_ref[...])
        st["cum_col"] = _dot_f32(lower_incl, g_col)
        st["cum_row"] = _dot_f32(g_row, upper_incl)
        last = st["cum_col"][ts - 1:ts, :]
        st["exp_cum"] = jnp.exp(st["cum_col"])
        st["exp_rem"] = jnp.exp(jnp.broadcast_to(last, (ts, LANES)) - st["cum_col"])
        st["exp_last"] = jnp.exp(last)

    def conv_silu(c, h, part):
        return _silu(_causal_conv(xbuf, cw_ref, ts, c * ts, head_cols(h, part)))

    def prepare_key(c, h):
        st = state[c]
        k = conv_silu(c, h, 1)
        k = k * lax.rsqrt(jnp.sum(k * k, axis=-1, keepdims=True) + NORM_EPS)
        st["k"][h], st["k_beta"][h] = k, k * _lane_bcast(st["beta"], n_heads + h)
        st["decay"][h] = jnp.exp(jnp.minimum(_lane_bcast(st["cum_col"], h) - st["cum_row"][h:h + 1, :], 0.0))

    def prepare_query_value(c, h):
        st = state[c]
        q, v = conv_silu(c, h, 0), conv_silu(c, h, 2)
        q = q * (lax.rsqrt(jnp.sum(q * q, axis=-1, keepdims=True) + NORM_EPS) * (HEAD_DIM ** -0.5))
        e_cum = _lane_bcast(st["exp_cum"], h)
        st["q"][h] = q
        st["rhs"][h] = jnp.concatenate([v * _lane_bcast(st["beta"], n_heads + h), st["k_beta"][h] * e_cum], axis=1)
        st["qd"][h] = q * e_cum
        st["kd"][h] = st["k"][h] * _lane_bcast(st["exp_rem"], h)

    def key_thunks(c):
        if c >= n_chunks:
            return []
        return [functools.partial(prepare_gates, c)] + [functools.partial(prepare_key, c, h) for h in heads]

    def solve(c):
        st = state[c]
        kks = [_dot_nt(st["k_beta"][h], st["k"][h]) for h in heads]
        neg_ls = [jnp.where(row > col, -(kks[h] * st["decay"][h]), 0.0) for h in heads]
        thunks = [functools.partial(prepare_query_value, c, h) for h in heads] + key_thunks(c + 1)
        slots = int(math.log2(ts))
        per_slot = -(-len(thunks) // slots)
        groups = [thunks[i:i + per_slot] for i in range(0, len(thunks), per_slot)]
        fillers = iter([functools.partial(lambda g: [t() for t in g], g) for g in groups])
        t_mats = _unit_lower_inverses(neg_ls, fillers)
        for group in fillers:
            group()
        qks = [_dot_nt(st["q"][h], st["k"][h]) for h in heads]
        st["intra"] = [jnp.where(row >= col, qks[h] * st["decay"][h], 0.0) for h in heads]
        st["uw"] = [_dot(t_mats[h], st["rhs"][h]) for h in heads]

    def advance(c):
        st = state[c]
        rows = slice(c * ts, (c + 1) * ts)
        ws_qs = [_dot(jnp.concatenate([st["uw"][h][:, HEAD_DIM:], st["qd"][h]], axis=0), s_ref[h]) for h in heads]
        v_news = [st["uw"][h][:, :HEAD_DIM] - ws_qs[h][:ts] for h in heads]
        mixed = [_dot(jnp.concatenate([st["intra"][h], st["kd"][h].T], axis=0), v_news[h]) for h in heads]
        for h in heads:
            s_ref[h] = (s_ref[h] * _lane_bcast(jnp.broadcast_to(st["exp_last"], (HEAD_DIM, LANES)), h)
                        + mixed[h][ts:])
            o = _rmsnorm(ws_qs[h][ts:] + mixed[h][:ts], nw_ref[...]) * _silu(z_ref[rows, head_cols(h, 0)].astype(F32))
            o_ref[rows, head_cols(h, 0)] = o.astype(o_ref.dtype)

    for thunk in key_thunks(0):
        thunk()
    for c in range(n_chunks):
        solve(c)
        advance(c)


def _gdn(proj_b, ab, abt, bsz, seq, cw, acol, dcol, arow, drow, nw, n_heads):
    ts = GDN_CHUNK * GDN_CHUNKS_PER_STEP
    assert seq % ts == 0
    ns = seq // ts
    dn_w = n_heads * HEAD_DIM
    full = lambda shape: pl.BlockSpec(shape, lambda b, s: (0,) * len(shape))
    return pl.pallas_call(
        _gdn_kernel,
        grid=(bsz, ns),
        in_specs=[pl.BlockSpec((ts, 3 * dn_w), lambda b, s: (b * ns + s, 0)),
                  pl.BlockSpec((ts, dn_w), lambda b, s: (b * ns + s, 3)),
                  pl.BlockSpec((ts, LANES), lambda b, s: (b * ns + s, 0)),
                  pl.BlockSpec((2 * n_heads, ts), lambda b, s: (0, b * ns + s)),
                  full(cw.shape), full(acol.shape), full(dcol.shape), full(arow.shape), full(drow.shape),
                  full(nw.shape)],
        out_specs=pl.BlockSpec((ts, dn_w), lambda b, s: (b * ns + s, 0)),
        out_shape=jax.ShapeDtypeStruct((bsz * seq, dn_w), BF16),
        scratch_shapes=[pltpu.VMEM((ts + SUBLANES, 3 * dn_w), F32),
                        pltpu.VMEM((n_heads, HEAD_DIM, HEAD_DIM), F32)],
        compiler_params=_params("parallel", "arbitrary"),
        name="gdn",
    )(proj_b, proj_b, ab, abt, cw, acol, dcol, arow, drow, nw)


def _merge_cross_kernel(x_ref, ya_ref, yb_ref, ga_ref, gb_ref, wa_ref, wb_ref, wo_ref,
                        nw_ref, wq_ref, kv_ref, wco_ref, o_ref):
    y_a = jnp.dot(ya_ref[...], wa_ref[...], preferred_element_type=F32)
    y_b = jnp.dot(yb_ref[...], wb_ref[...], preferred_element_type=F32)
    m = _sigmoid(ga_ref[...].astype(F32)) * y_a + _sigmoid(gb_ref[...].astype(F32)) * y_b
    x = x_ref[...] + _dot(m, wo_ref[...])

    d = x.shape[1]
    hd = d // CA_HEADS
    q = _dot(_rmsnorm(x, nw_ref[...]), wq_ref[...])
    outs = []
    for h in range(CA_HEADS):
        k_h = kv_ref[:, h * hd:(h + 1) * hd]
        v_h = kv_ref[:, d + h * hd:d + (h + 1) * hd]
        s = _dot_nt(q[:, h * hd:(h + 1) * hd], k_h) * (hd ** -0.5)
        s = s - jnp.max(s, axis=-1, keepdims=True)
        e = jnp.exp(s)
        p = e / jnp.sum(e, axis=-1, keepdims=True)
        outs.append(_dot(p, v_h))
    o_ref[...] = x + _dot(jnp.concatenate(outs, axis=1), wco_ref[...])


def _merge_cross(x, gated_a, gated_b, proj_b, w_a, w_b, w_o, nw, w_q, kv, w_co, bsz, seq, n_mem, ts):
    n, d = x.shape
    ns = seq // ts
    full = lambda shape: pl.BlockSpec(shape, lambda b, s: (0,) * len(shape))
    rows = lambda width, col: pl.BlockSpec((ts, width), lambda b, s: (b * ns + s, col))
    return pl.pallas_call(
        _merge_cross_kernel,
        grid=(bsz, ns),
        in_specs=[rows(d, 0), rows(gated_a.shape[1], 0), rows(gated_b.shape[1], 0), rows(d, 4), rows(d, 5),
                  full(w_a.shape), full(w_b.shape), full(w_o.shape), full(nw.shape), full(w_q.shape),
                  pl.BlockSpec((n_mem, 2 * d), lambda b, s: (b, 0)),
                  full(w_co.shape)],
        out_specs=rows(d, 0),
        out_shape=jax.ShapeDtypeStruct((n, d), F32),
        compiler_params=_params("parallel", "parallel"),
        name="merge_cross",
    )(x, gated_a, gated_b, proj_b, proj_b, w_a, w_b, w_o, nw, w_q, kv, w_co)


def _router_kernel(h_ref, nw_ref, wr_ref, br_ref, u_ref, idx_ref, wt_ref, cnt_ref, base_ref):
    t = h_ref.shape[0]

    @pl.when(pl.program_id(0) == 0)
    def _():
        base_ref[...] = jnp.zeros_like(base_ref)

    u = _rmsnorm(h_ref[...], nw_ref[...])
    u_ref[...] = u
    logits = _dot_split(_split(u), _split(wr_ref[...])) + br_ref[...]
    lane = lax.broadcasted_iota(I32, (t, LANES), 1)
    lanef = lane.astype(F32)
    big = float(LANES)
    neg = -jnp.inf

    lg = jnp.where((lane >= N_EXPERTS) & (lane < N_EXPERTS + N_GROUPS), logits, neg)
    gmax = jnp.max(lg, axis=-1, keepdims=True)
    grp = jnp.min(jnp.where(lg == gmax, lanef - float(N_EXPERTS), big), axis=-1, keepdims=True)
    p_grp = 1.0 / jnp.sum(jnp.exp(lg - gmax), axis=-1, keepdims=True)

    in_grp = (lane < N_EXPERTS) & ((lane // GROUP_SIZE).astype(F32) == grp)
    le = jnp.where(in_grp, logits, neg)
    m1 = jnp.max(le, axis=-1, keepdims=True)
    i1 = jnp.min(jnp.where(le == m1, lanef, big), axis=-1, keepdims=True)
    le2 = jnp.where(lanef == i1, neg, le)
    m2 = jnp.max(le2, axis=-1, keepdims=True)
    i2 = jnp.min(jnp.where(le2 == m2, lanef, big), axis=-1, keepdims=True)
    ratio = jnp.exp(m2 - m1)
    p1 = 1.0 / (1.0 + ratio)
    p2 = ratio * p1

    oh1 = lanef == i1
    oh2 = lanef == i2
    onehot = jnp.where(oh1 | oh2, 1.0, 0.0)
    r_i = lax.broadcasted_iota(I32, (t, t), 0)
    c_i = lax.broadcasted_iota(I32, (t, t), 1)
    before = jnp.where(r_i > c_i, 1.0, 0.0)
    rank = _dot(before, onehot) + base_ref[...]
    r1 = jnp.sum(jnp.where(oh1, rank, 0.0), axis=-1, keepdims=True)
    r2 = jnp.sum(jnp.where(oh2, rank, 0.0), axis=-1, keepdims=True)
    base_ref[...] = base_ref[...] + jnp.sum(onehot, axis=0, keepdims=True)

    idx = jnp.where(lane == 0, i1, jnp.where(lane == 1, i2, jnp.where(lane == 2, r1, jnp.where(lane == 3, r2, 0.0))))
    idx_ref[...] = idx.astype(I32)
    wt_ref[...] = jnp.where(lane == 0, p_grp * p1, jnp.where(lane == 1, p_grp * p2, 0.0))
    cnt_ref[...] = jnp.broadcast_to(base_ref[...], cnt_ref.shape)


def _router(h, nw, wr, br, t):
    n, d = h.shape
    full = lambda shape: pl.BlockSpec(shape, lambda i: (0,) * len(shape))
    return pl.pallas_call(
        _router_kernel,
        grid=(n // t,),
        in_specs=[pl.BlockSpec((t, d), lambda i: (i, 0)), full(nw.shape), full(wr.shape), full(br.shape)],
        out_specs=[pl.BlockSpec((t, d), lambda i: (i, 0)),
                   pl.BlockSpec((t, LANES), lambda i: (i, 0)),
                   pl.BlockSpec((t, LANES), lambda i: (i, 0)),
                   pl.BlockSpec((SUBLANES, LANES), lambda i: (0, 0))],
        out_shape=[jax.ShapeDtypeStruct((n, d), F32),
                   jax.ShapeDtypeStruct((n, LANES), I32),
                   jax.ShapeDtypeStruct((n, LANES), F32),
                   jax.ShapeDtypeStruct((SUBLANES, LANES), F32)],
        scratch_shapes=[pltpu.VMEM((1, LANES), F32)],
        compiler_params=_params("arbitrary"),
        name="router",
    )(h, nw, wr, br)


def _row_copy(src, src_row, dst, dst_row, sem):
    return pltpu.make_async_copy(src.at[pl.ds(src_row, 1)], dst.at[pl.ds(dst_row, 1)], sem)


def _dispatch_kernel(dest_ref, u_ref, xs_in_ref, xs_ref, sem):
    del xs_in_ref
    t = u_ref.shape[0]

    def start(r, carry):
        _row_copy(u_ref, r, xs_ref, dest_ref[0, 0, 2 * r], sem).start()
        _row_copy(u_ref, r, xs_ref, dest_ref[0, 0, 2 * r + 1], sem).start()
        return carry

    def wait(r, carry):
        _row_copy(u_ref, r, xs_ref, dest_ref[0, 0, 2 * r], sem).wait()
        _row_copy(u_ref, r, xs_ref, dest_ref[0, 0, 2 * r + 1], sem).wait()
        return carry

    lax.fori_loop(0, t, start, 0, unroll=DMA_UNROLL)
    lax.fori_loop(0, t, wait, 0, unroll=DMA_UNROLL)


def _dispatch(dest3, u, xs_zero, t):
    n, d = u.shape
    return pl.pallas_call(
        _dispatch_kernel,
        grid=(n // t,),
        in_specs=[pl.BlockSpec((1, 1, 2 * t), lambda i: (i, 0, 0), memory_space=pltpu.SMEM),
                  pl.BlockSpec((t, d), lambda i: (i, 0)),
                  pl.BlockSpec(memory_space=pl.ANY)],
        out_specs=pl.BlockSpec(memory_space=pl.ANY),
        out_shape=jax.ShapeDtypeStruct(xs_zero.shape, xs_zero.dtype),
        scratch_shapes=[pltpu.SemaphoreType.DMA(())],
        input_output_aliases={2: 0},
        compiler_params=_params("arbitrary"),
        name="moe_dispatch",
    )(dest3, u, xs_zero)


def _expert_kernel(be_ref, nu_ref, xs_ref, wg_ref, wu_ref, wd_ref, y_ref, wg_bf, wu_bf, wd_bf):
    j = pl.program_id(0)
    used = j < nu_ref[0]
    new_expert = (j == 0) | (be_ref[j] != be_ref[jnp.maximum(j - 1, 0)])

    @pl.when(used & new_expert)
    def _():
        wg_bf[...] = wg_ref[0].astype(BF16)
        wu_bf[...] = wu_ref[0].astype(BF16)
        wd_bf[...] = wd_ref[0].astype(BF16)

    @pl.when(used)
    def _():
        x = xs_ref[...].astype(BF16)
        hid = _silu(jnp.dot(x, wg_bf[...], preferred_element_type=F32)) * jnp.dot(
            x, wu_bf[...], preferred_element_type=F32)
        y_ref[...] = _dot(hid, wd_bf[...])

    @pl.when(jnp.logical_not(used))
    def _():
        y_ref[...] = jnp.zeros_like(y_ref)


def _experts(block_e, n_used, xs, w_gate, w_up, w_down):
    rows, d = xs.shape
    de = w_gate.shape[2]
    grid_spec = pltpu.PrefetchScalarGridSpec(
        num_scalar_prefetch=2,
        grid=(rows // MOE_BLOCK,),
        in_specs=[pl.BlockSpec((MOE_BLOCK, d), lambda j, be, nu: (jnp.minimum(j, nu[0] - 1), 0)),
                  pl.BlockSpec((1, d, de), lambda j, be, nu: (be[j], 0, 0)),
                  pl.BlockSpec((1, d, de), lambda j, be, nu: (be[j], 0, 0)),
                  pl.BlockSpec((1, de, d), lambda j, be, nu: (be[j], 0, 0))],
        out_specs=pl.BlockSpec((MOE_BLOCK, d), lambda j, be, nu: (j, 0)),
        scratch_shapes=[pltpu.VMEM((d, de), BF16), pltpu.VMEM((d, de), BF16), pltpu.VMEM((de, d), BF16)],
    )
    return pl.pallas_call(
        _expert_kernel,
        grid_spec=grid_spec,
        out_shape=jax.ShapeDtypeStruct((rows, d), F32),
        compiler_params=_params("arbitrary"),
        name="moe_experts",
    )(block_e, n_used, xs, w_gate, w_up, w_down)


def _combine_kernel(dest_ref, h_ref, wt_ref, nw_ref, y_ref, o_ref, buf, sem, *, final_norm):
    t = h_ref.shape[0]

    def start(r, carry):
        _row_copy(y_ref, dest_ref[0, 0, 2 * r], buf.at[0], r, sem).start()
        _row_copy(y_ref, dest_ref[0, 0, 2 * r + 1], buf.at[1], r, sem).start()
        return carry

    def wait(r, carry):
        _row_copy(y_ref, dest_ref[0, 0, 2 * r], buf.at[0], r, sem).wait()
        _row_copy(y_ref, dest_ref[0, 0, 2 * r + 1], buf.at[1], r, sem).wait()
        return carry

    lax.fori_loop(0, t, start, 0, unroll=DMA_UNROLL)
    lax.fori_loop(0, t, wait, 0, unroll=DMA_UNROLL)
    wt = wt_ref[...]
    moe = buf[0] * wt[:, 0:1] + buf[1] * wt[:, 1:2]
    out = h_ref[...] + moe
    o_ref[...] = _rmsnorm(out, nw_ref[...]) if final_norm else out


def _combine(dest3, h, wt, nw, y, t, final_norm):
    n, d = h.shape
    return pl.pallas_call(
        functools.partial(_combine_kernel, final_norm=final_norm),
        grid=(n // t,),
        in_specs=[pl.BlockSpec((1, 1, 2 * t), lambda i: (i, 0, 0), memory_space=pltpu.SMEM),
                  pl.BlockSpec((t, d), lambda i: (i, 0)),
                  pl.BlockSpec((t, LANES), lambda i: (i, 0)),
                  pl.BlockSpec((1, d), lambda i: (0, 0)),
                  pl.BlockSpec(memory_space=pl.ANY)],
        out_specs=pl.BlockSpec((t, d), lambda i: (i, 0)),
        out_shape=jax.ShapeDtypeStruct((n, d), F32),
        scratch_shapes=[pltpu.VMEM((2, t, d), F32), pltpu.SemaphoreType.DMA(())],
        compiler_params=_params("arbitrary"),
        name="moe_combine",
    )(dest3, h, wt, nw, y)


def _tile(n, pref):
    return pref if n % pref == 0 else n


def kernel(x, mem, norm1_w, w_in, rnn_conv_w, rnn_conv_b, rglru_wa, rglru_ba, rglru_wx, rglru_bx, rglru_lambda, w_branch_a, dn_conv_w, dn_a_log, dn_dt_bias, dn_norm_w, w_branch_b, w_out, norm2_w, mem_norm_w, w_cq, w_ckv, w_co, norm3_w, w_router_group, b_router_group, w_router_expert, b_router_expert, w_exp_gate, w_exp_up, w_exp_down, norm_f_w):
    bsz, seq, d = x.shape
    n = bsz * seq
    n_mem = mem.shape[1]
    depth = w_in.shape[0]
    d_rnn = rnn_conv_w.shape[2]
    n_heads = dn_a_log.shape[1]
    dn_w = n_heads * HEAD_DIM
    row = lambda v: v.reshape(1, -1).astype(F32)

    h = x.reshape(n, d)
    mem2 = mem.reshape(bsz * n_mem, d)
    tm = _tile(n, 2048)
    for l in range(depth):
        o_rg, o_qkv, o_z = d_rnn, 2 * d_rnn, 2 * d_rnn + 3 * dn_w
        o_a = o_z + dn_w
        o_ga = o_a + 2 * n_heads
        wi = w_in[l]
        w_cat = jnp.concatenate([wi[:, :o_a], wi[:, o_ga:]], axis=1).astype(BF16)
        w_ab = jnp.pad(wi[:, o_a:o_ga], ((0, 0), (0, LANES - 2 * n_heads))).astype(BF16)
        proj_a, proj_b, ab = _in_proj(h, row(norm1_w[l]), w_cat, w_ab, o_qkv, tm, 512)

        gated_a = _rglru(proj_a, bsz, seq, rnn_conv_w[l], row(rnn_conv_b[l]), rglru_wa[l].astype(BF16),
                         row(rglru_ba[l]), rglru_wx[l].astype(BF16), row(rglru_bx[l]), row(rglru_lambda[l]),
                         _tile(seq, 512))

        a_dec = jnp.exp(dn_a_log[l].astype(F32))
        pad_h = lambda v: jnp.pad(v, (0, LANES - n_heads))
        acol, dcol = row(pad_h(a_dec)), row(pad_h(dn_dt_bias[l]))
        arow = jnp.broadcast_to(jnp.pad(a_dec, (0, n_heads))[:, None], (2 * n_heads, GDN_CHUNK))
        drow = jnp.broadcast_to(jnp.pad(dn_dt_bias[l], (0, n_heads))[:, None], (2 * n_heads, GDN_CHUNK))
        abt = ab[:, :2 * n_heads].T
        gated_b = _gdn(proj_b, ab, abt, bsz, seq, dn_conv_w[l], acol, dcol, arow, drow, row(dn_norm_w[l]), n_heads)

        kv = _norm_mm(mem2, row(mem_norm_w[l]), w_ckv[l].astype(BF16), BF16, _tile(bsz * n_mem, 1024), 512)
        h = _merge_cross(h, gated_a, gated_b, proj_b, w_branch_a[l].astype(BF16), w_branch_b[l].astype(BF16),
                         w_out[l].astype(BF16), row(norm2_w[l]), w_cq[l].astype(BF16), kv, w_co[l].astype(BF16),
                         bsz, seq, n_mem, _tile(seq, 512))

        w_r = jnp.pad(jnp.concatenate([w_router_expert[l], w_router_group[l]], axis=1),
                      ((0, 0), (0, LANES - N_EXPERTS - N_GROUPS)))
        b_r = row(jnp.pad(jnp.concatenate([b_router_expert[l], b_router_group[l]]), (0, LANES - N_EXPERTS - N_GROUPS)))
        u3, idx, wt, cnt = _router(h, row(norm3_w[l]), w_r, b_r, _tile(n, 512))

        counts = cnt[0, :N_EXPERTS].astype(I32)
        padded = (counts + MOE_BLOCK - 1) // MOE_BLOCK * MOE_BLOCK
        pend = jnp.cumsum(padded)
        pstart = pend - padded
        is_expert = idx[:, 0:2, None] == jnp.arange(N_EXPERTS, dtype=I32)
        dest = jnp.sum(jnp.where(is_expert, pstart, 0), axis=-1) + idx[:, 2:4]
        n_blocks = (2 * n + N_EXPERTS * (MOE_BLOCK - 1)) // MOE_BLOCK
        block_row = jnp.arange(n_blocks, dtype=I32) * MOE_BLOCK
        block_e = jnp.minimum(jnp.sum((pend[None, :] <= block_row[:, None]).astype(I32), axis=1), N_EXPERTS - 1)
        n_used = (pend[-1:] // MOE_BLOCK).astype(I32)
        t_moe = _tile(n, 512)
        dest3 = dest.reshape(n // t_moe, 1, 2 * t_moe)
        xs = _dispatch(dest3, u3, jnp.zeros((n_blocks * MOE_BLOCK, d), F32), t_moe)
        yb = _experts(block_e, n_used, xs, w_exp_gate[l], w_exp_up[l], w_exp_down[l])
        h = _combine(dest3, h, wt, row(norm_f_w), yb, t_moe, final_norm=(l == depth - 1))
    return h.reshape(bsz, seq, d)
```

```python
import functools
import math

import jax
import jax.numpy as jnp
from jax import lax
from jax.experimental import pallas as pl
from jax.experimental.pallas import tpu as pltpu

F32 = jnp.float32
BF16 = jnp.bfloat16
I32 = jnp.int32
HIGHEST = lax.Precision.HIGHEST

NORM_EPS = 1e-6
CONV_TAPS = 4
RNN_BLOCK = 128
RG_POWER = 8.0
HEAD_DIM = 128
GDN_CHUNK = 128
GDN_CHUNKS_PER_STEP = 2
CA_HEADS = 4
N_GROUPS = 8
GROUP_SIZE = 8
N_EXPERTS = N_GROUPS * GROUP_SIZE
MOE_BLOCK = 512
LANES = 128
SUBLANES = 8
DMA_UNROLL = 8
VMEM_LIMIT = 48 * 1024 * 1024


def _params(*semantics):
    return pltpu.CompilerParams(dimension_semantics=semantics, vmem_limit_bytes=VMEM_LIMIT)


def _dot(a, b):
    return jnp.dot(a.astype(BF16), b.astype(BF16), preferred_element_type=F32)


def _dot_nt(a, b):
    return lax.dot_general(a.astype(BF16), b.astype(BF16), (((1,), (1,)), ((), ())),
                           preferred_element_type=F32)


def _dot_tn(a, b):
    return lax.dot_general(a.astype(BF16), b.astype(BF16), (((0,), (0,)), ((), ())),
                           preferred_element_type=F32)


def _dot_f32(a, b):
    return jnp.dot(a, b, precision=HIGHEST, preferred_element_type=F32)


def _split(x):
    hi = x.astype(BF16)
    return hi, (x - hi.astype(F32)).astype(BF16)


def _dot_split(a_parts, b_parts):
    (a_hi, a_lo), (b_hi, b_lo) = a_parts, b_parts
    dot = functools.partial(jnp.dot, preferred_element_type=F32)
    return dot(a_hi, b_hi) + (dot(a_hi, b_lo) + dot(a_lo, b_hi))


def _rmsnorm(x, w):
    return x * lax.rsqrt(jnp.mean(x * x, axis=-1, keepdims=True) + NORM_EPS) * w


def _sigmoid(x):
    return 0.5 * jnp.tanh(0.5 * x) + 0.5


def _silu(x):
    half = 0.5 * x
    return half + half * jnp.tanh(half)


def _softplus(x):
    return jnp.maximum(x, 0.0) + jnp.log(1.0 + jnp.exp(-jnp.abs(x)))


def _one_minus_exp2(y, exp_y):
    return jnp.tanh(-y) * (1.0 + exp_y * exp_y)


def _gelu_tanh(x):
    return 0.5 * x * (1.0 + jnp.tanh(math.sqrt(2.0 / math.pi) * (x + 0.044715 * (x * x * x))))


def _norm_mm_kernel(x_ref, nw_ref, w_ref, o_ref, u_ref):
    @pl.when(pl.program_id(1) == 0)
    def _():
        u_ref[...] = _rmsnorm(x_ref[...], nw_ref[...]).astype(BF16)

    o_ref[...] = jnp.dot(u_ref[...], w_ref[...], preferred_element_type=F32).astype(o_ref.dtype)


def _norm_mm(x, nw, w, out_dtype, tm, tn):
    n, d = x.shape
    c = w.shape[1]
    return pl.pallas_call(
        _norm_mm_kernel,
        grid=(n // tm, c // tn),
        in_specs=[pl.BlockSpec((tm, d), lambda i, j: (i, 0)),
                  pl.BlockSpec((1, d), lambda i, j: (0, 0)),
                  pl.BlockSpec((d, tn), lambda i, j: (0, j))],
        out_specs=pl.BlockSpec((tm, tn), lambda i, j: (i, j)),
        out_shape=jax.ShapeDtypeStruct((n, c), out_dtype),
        scratch_shapes=[pltpu.VMEM((tm, d), BF16)],
        compiler_params=_params("parallel", "arbitrary"),
        name="norm_mm",
    )(x, nw, w)


def _in_proj_kernel(x_ref, nw_ref, w_ref, wab_ref, oa_ref, ob_ref, ab_ref, u_ref, *, n_a):
    j = pl.program_id(1)

    @pl.when(j == 0)
    def _():
        u_ref[...] = _rmsnorm(x_ref[...], nw_ref[...]).astype(BF16)
        ab_ref[...] = jnp.dot(u_ref[...], wab_ref[...], preferred_element_type=F32)

    @pl.when(j < n_a)
    def _():
        oa_ref[...] = jnp.dot(u_ref[...], w_ref[...], preferred_element_type=F32).astype(oa_ref.dtype)

    @pl.when(j >= n_a)
    def _():
        ob_ref[...] = jnp.dot(u_ref[...], w_ref[...], preferred_element_type=F32).astype(ob_ref.dtype)


def _in_proj(x, nw, w_cat, w_ab, c_a, tm, tn):
    n, d = x.shape
    c = w_cat.shape[1]
    n_a, n_b = c_a // tn, (c - c_a) // tn
    return pl.pallas_call(
        functools.partial(_in_proj_kernel, n_a=n_a),
        grid=(n // tm, n_a + n_b),
        in_specs=[pl.BlockSpec((tm, d), lambda i, j: (i, 0)),
                  pl.BlockSpec((1, d), lambda i, j: (0, 0)),
                  pl.BlockSpec((d, tn), lambda i, j: (0, j)),
                  pl.BlockSpec(w_ab.shape, lambda i, j: (0, 0))],
        out_specs=[pl.BlockSpec((tm, tn), lambda i, j: (i, jnp.minimum(j, n_a - 1))),
                   pl.BlockSpec((tm, tn), lambda i, j: (i, jnp.maximum(j - n_a, 0))),
                   pl.BlockSpec((tm, w_ab.shape[1]), lambda i, j: (i, 0))],
        out_shape=[jax.ShapeDtypeStruct((n, c_a), BF16), jax.ShapeDtypeStruct((n, c - c_a), BF16),
                   jax.ShapeDtypeStruct((n, w_ab.shape[1]), F32)],
        scratch_shapes=[pltpu.VMEM((tm, d), BF16)],
        compiler_params=_params("arbitrary", "arbitrary"),
        name="in_proj",
    )(x, nw, w_cat, w_ab)


def _load_conv_window(x_ref, xbuf, ts):
    @pl.when(pl.program_id(1) == 0)
    def _():
        xbuf[0:SUBLANES, :] = jnp.zeros((SUBLANES, xbuf.shape[1]), F32)

    @pl.when(pl.program_id(1) != 0)
    def _():
        xbuf[0:SUBLANES, :] = xbuf[ts:ts + SUBLANES, :]

    xbuf[SUBLANES:SUBLANES + ts, :] = x_ref[...].astype(F32)


def _causal_conv(xbuf, cw_ref, ts, first_row=0, cols=slice(None)):
    base = first_row + SUBLANES - (CONV_TAPS - 1)
    acc = cw_ref[0:1, cols] * xbuf[base:base + ts, cols]
    for k in range(1, CONV_TAPS):
        acc = acc + cw_ref[k:k + 1, cols] * xbuf[base + k:base + k + ts, cols]
    return acc


def _rglru_kernel(rx_ref, rg_ref, cw_ref, cb_ref, wa_ref, ba_ref, wx_ref, bx_ref, lam_ref, o_ref,
                  xbuf, a_ref, b_ref, carry_ref):
    ts, c = rx_ref.shape
    _load_conv_window(rx_ref, xbuf, ts)

    @pl.when(pl.program_id(1) == 0)
    def _():
        carry_ref[...] = jnp.zeros_like(carry_ref)

    xc = _causal_conv(xbuf, cw_ref, ts) + cb_ref[...]
    neg_sp = -RG_POWER * _softplus(-lam_ref[...])
    for n in range(c // RNN_BLOCK):
        sl = slice(n * RNN_BLOCK, (n + 1) * RNN_BLOCK)
        xb = xc[:, sl]
        r = _sigmoid(_dot(xb, wa_ref[n]) + ba_ref[:, sl])
        i = _sigmoid(_dot(xb, wx_ref[n]) + bx_ref[:, sl])
        log_a = neg_sp[:, sl] * r
        a = jnp.exp(log_a)
        a_ref[:, sl] = a
        b_ref[:, sl] = jnp.sqrt(_one_minus_exp2(log_a, a)) * (i * xb)

    row = lax.broadcasted_iota(I32, (SUBLANES, c), 0)

    def slab(t, carry):
        rows = pl.ds(pl.multiple_of(t * SUBLANES, SUBLANES), SUBLANES)
        a = a_ref[rows, :]
        b = b_ref[rows, :]
        for d in (1, 2, 4):
            a_sh = jnp.where(row >= d, pltpu.roll(a, d, 0), 1.0)
            b_sh = jnp.where(row >= d, pltpu.roll(b, d, 0), 0.0)
            b = a * b_sh + b
            a = a * a_sh
        h = a * carry + b
        b_ref[rows, :] = h
        return h[SUBLANES - 1:SUBLANES, :]

    carry_ref[...] = lax.fori_loop(0, ts // SUBLANES, slab, carry_ref[...])
    o_ref[...] = (_gelu_tanh(rg_ref[...].astype(F32)) * b_ref[...]).astype(o_ref.dtype)


def _rglru(proj_a, bsz, seq, cw, cb, wa, ba, wx, bx, lam, ts):
    c = cw.shape[1]
    ns = seq // ts
    full = lambda shape: pl.BlockSpec(shape, lambda b, s: (0,) * len(shape))
    return pl.pallas_call(
        _rglru_kernel,
        grid=(bsz, ns),
        in_specs=[pl.BlockSpec((ts, c), lambda b, s: (b * ns + s, 0)),
                  pl.BlockSpec((ts, c), lambda b, s: (b * ns + s, 1)),
                  full(cw.shape), full(cb.shape), full(wa.shape), full(ba.shape),
                  full(wx.shape), full(bx.shape), full(lam.shape)],
        out_specs=pl.BlockSpec((ts, c), lambda b, s: (b * ns + s, 0)),
        out_shape=jax.ShapeDtypeStruct((bsz * seq, c), BF16),
        scratch_shapes=[pltpu.VMEM((ts + SUBLANES, c), F32), pltpu.VMEM((ts, c), F32),
                        pltpu.VMEM((ts, c), F32), pltpu.VMEM((1, c), F32)],
        compiler_params=_params("parallel", "arbitrary"),
        name="rglru",
    )(proj_a, proj_a, cw, cb, wa, ba, wx, bx, lam)


def _lane_bcast(x, lane):
    return jnp.broadcast_to(x[:, lane:lane + 1], x.shape)


def _unit_lower_inverses(neg_ls, fillers):
    ts = neg_ls[0].shape[0]
    half = ts // 2
    run_filler = lambda: next(fillers, lambda: None)()
    lane = lax.broadcasted_iota(I32, (half, ts), 1)
    left = lane < half
    left_of = lambda x: jnp.where(left, x, 0.0)
    right_of = lambda x: jnp.where(left, 0.0, x)
    left_mask, right_mask = left_of(1.0).astype(BF16), right_of(1.0).astype(BF16)
    stack = lambda top, bottom: tuple(jnp.concatenate([a, b], axis=0) for a, b in zip(top, bottom))
    diag = lambda parts: stack([p * left_mask for p in parts], [p * right_mask for p in parts])
    eye_pair = ((lane & (half - 1)) == lax.broadcasted_iota(I32, (half, ts), 0)).astype(F32)

    pairs = [n[:half] + right_of(n[half:]) for n in neg_ls]
    couplings = [left_of(n[half:]) for n in neg_ls]
    ts_mats = [eye_pair + ab for ab in pairs]
    pair_parts = [_split(ab) for ab in pairs]
    ps = [_dot_split(pp, diag(pp)) for pp in pair_parts]
    run_filler()
    levels = int(math.log2(half)) - 1
    for lvl in range(levels - 1):
        p_parts = [_split(p) for p in ps]
        both = [_dot_split(stack(_split(t), pp), diag(pp)) for t, pp in zip(ts_mats, p_parts)]
        ts_mats = [t + b[:half] for t, b in zip(ts_mats, both)]
        ps = [b[half:] for b in both]
        run_filler()
    ts_mats = [t + _dot_split(_split(t), diag(_split(p))) for t, p in zip(ts_mats, ps)]
    run_filler()
    t_parts = [_split(t) for t in ts_mats]
    c_ta = [_dot_split(_split(c), diag(tp)) for c, tp in zip(couplings, t_parts)]
    run_filler()
    below = lambda parts: stack([jnp.zeros_like(p) for p in parts], parts)
    tb_c_ta = [_dot_split(tp, below(_split(m))) for tp, m in zip(t_parts, c_ta)]
    return [jnp.concatenate([left_of(t), jnp.where(left, x, t)], axis=0) for t, x in zip(ts_mats, tb_c_ta)]


def _gdn_kernel(qkv_ref, z_ref, ab_ref, abt_ref, cw_ref, acol_ref, dcol_ref, arow_ref, drow_ref, nw_ref,
                o_ref, xbuf, s_ref):
    ts = GDN_CHUNK
    n_heads = s_ref.shape[0]
    dn_w = n_heads * HEAD_DIM
    _load_conv_window(qkv_ref, xbuf, qkv_ref.shape[0])

    @pl.when(pl.program_id(1) == 0)
    def _():
        s_ref[...] = jnp.zeros_like(s_ref)

    row = lax.broadcasted_iota(I32, (ts, ts), 0)
    col = lax.broadcasted_iota(I32, (ts, ts), 1)
    lower_incl = (row >= col).astype(F32)
    upper_incl = (row <= col).astype(F32)
    head_cols = lambda h, part: slice(part * dn_w + h * HEAD_DIM, part * dn_w + (h + 1) * HEAD_DIM)
    heads = range(n_heads)

    n_chunks = qkv_ref.shape[0] // ts
    state = [dict(q=[None] * n_heads, k=[None] * n_heads, k_beta=[None] * n_heads, decay=[None] * n_heads,
                  rhs=[None] * n_heads, qd=[None] * n_heads, kd=[None] * n_heads) for _ in range(n_chunks)]

    def prepare_gates(c):
        st = state[c]
        rows = slice(c * ts, (c + 1) * ts)
        ab = ab_ref[rows, :]
        g_col = -acol_ref[...] * _softplus(ab + dcol_ref[...])
        st["beta"] = _sigmoid(ab)
        g_row = -arow_ref[...] * _softplus(abt_ref[:, rows] + drow_ref[...])
        st["cum_col"] = _dot_f32(lower_incl, g_col)
        st["cum_row"] = _dot_f32(g_row, upper_incl)
        last = st["cum_col"][ts - 1:ts, :]
        st["exp_cum"] = jnp.exp(st["cum_col"])
        st["exp_rem"] = jnp.exp(jnp.broadcast_to(last, (ts, LANES)) - st["cum_col"])
        st["exp_last"] = jnp.exp(last)

    def conv_silu(c, h, part):
        return _silu(_causal_conv(xbuf, cw_ref, ts, c * ts, head_cols(h, part)))

    def prepare_key(c, h):
        st = state[c]
        k = conv_silu(c, h, 1)
        k = k * lax.rsqrt(jnp.sum(k * k, axis=-1, keepdims=True) + NORM_EPS)
        st["k"][h], st["k_beta"][h] = k, k * _lane_bcast(st["beta"], n_heads + h)
        st["decay"][h] = jnp.exp(jnp.minimum(_lane_bcast(st["cum_col"], h) - st["cum_row"][h:h + 1, :], 0.0))

    def prepare_query_value(c, h):
        st = state[c]
        q, v = conv_silu(c, h, 0), conv_silu(c, h, 2)
        q = q * (lax.rsqrt(jnp.sum(q * q, axis=-1, keepdims=True) + NORM_EPS) * (HEAD_DIM ** -0.5))
        e_cum = _lane_bcast(st["exp_cum"], h)
        st["q"][h] = q
        st["rhs"][h] = jnp.concatenate([v * _lane_bcast(st["beta"], n_heads + h), st["k_beta"][h] * e_cum], axis=1)
        st["qd"][h] = q * e_cum
        st["kd"][h] = st["k"][h] * _lane_bcast(st["exp_rem"], h)

    def key_thunks(c):
        if c >= n_chunks:
            return []
        return [functools.partial(prepare_gates, c)] + [functools.partial(prepare_key, c, h) for h in heads]

    def solve(c):
        st = state[c]
        kks = [_dot_nt(st["k_beta"][h], st["k"][h]) for h in heads]
        neg_ls = [jnp.where(row > col, -(kks[h] * st["decay"][h]), 0.0) for h in heads]
        thunks = [functools.partial(prepare_query_value, c, h) for h in heads] + key_thunks(c + 1)
        slots = int(math.log2(ts))
        per_slot = -(-len(thunks) // slots)
        groups = [thunks[i:i + per_slot] for i in range(0, len(thunks), per_slot)]
        fillers = iter([functools.partial(lambda g: [t() for t in g], g) for g in groups])
        t_mats = _unit_lower_inverses(neg_ls, fillers)
        for group in fillers:
            group()
        qks = [_dot_nt(st["q"][h], st["k"][h]) for h in heads]
        st["intra"] = [jnp.where(row >= col, qks[h] * st["decay"][h], 0.0) for h in heads]
        st["uw"] = [_dot(t_mats[h], st["rhs"][h]) for h in heads]

    def advance(c):
        st = state[c]
        rows = slice(c * ts, (c + 1) * ts)
        ws_qs = [_dot(jnp.concatenate([st["uw"][h][:, HEAD_DIM:], st["qd"][h]], axis=0), s_ref[h]) for h in heads]
        v_news = [st["uw"][h][:, :HEAD_DIM] - ws_qs[h][:ts] for h in heads]
        mixed = [_dot(jnp.concatenate([st["intra"][h], st["kd"][h].T], axis=0), v_news[h]) for h in heads]
        for h in heads:
            s_ref[h] = (s_ref[h] * _lane_bcast(jnp.broadcast_to(st["exp_last"], (HEAD_DIM, LANES)), h)
                        + mixed[h][ts:])
            o = _rmsnorm(ws_qs[h][ts:] + mixed[h][:ts], nw_ref[...]) * _silu(z_ref[rows, head_cols(h, 0)].astype(F32))
            o_ref[rows, head_cols(h, 0)] = o.astype(o_ref.dtype)

    for thunk in key_thunks(0):
        thunk()
    for c in range(n_chunks):
        solve(c)
        advance(c)


def _gdn(proj_b, ab, abt, bsz, seq, cw, acol, dcol, arow, drow, nw, n_heads):
    ts = GDN_CHUNK * GDN_CHUNKS_PER_STEP
    assert seq % ts == 0
    ns = seq // ts
    dn_w = n_heads * HEAD_DIM
    full = lambda shape: pl.BlockSpec(shape, lambda b, s: (0,) * len(shape))
    return pl.pallas_call(
        _gdn_kernel,
        grid=(bsz, ns),
        in_specs=[pl.BlockSpec((ts, 3 * dn_w), lambda b, s: (b * ns + s, 0)),
                  pl.BlockSpec((ts, dn_w), lambda b, s: (b * ns + s, 3)),
                  pl.BlockSpec((ts, LANES), lambda b, s: (b * ns + s, 0)),
                  pl.BlockSpec((2 * n_heads, ts), lambda b, s: (0, b * ns + s)),
                  full(cw.shape), full(acol.shape), full(dcol.shape), full(arow.shape), full(drow.shape),
                  full(nw.shape)],
        out_specs=pl.BlockSpec((ts, dn_w), lambda b, s: (b * ns + s, 0)),
        out_shape=jax.ShapeDtypeStruct((bsz * seq, dn_w), BF16),
        scratch_shapes=[pltpu.VMEM((ts + SUBLANES, 3 * dn_w), F32),
                        pltpu.VMEM((n_heads, HEAD_DIM, HEAD_DIM), F32)],
        compiler_params=_params("parallel", "arbitrary"),
        name="gdn",
    )(proj_b, proj_b, ab, abt, cw, acol, dcol, arow, drow, nw)


def _merge_cross_kernel(x_ref, ya_ref, yb_ref, ga_ref, gb_ref, wa_ref, wb_ref, wo_ref,
                        nw_ref, wq_ref, kv_ref, wco_ref, o_ref):
    y_a = jnp.dot(ya_ref[...], wa_ref[...], preferred_element_type=F32)
    y_b = jnp.dot(yb_ref[...], wb_ref[...], preferred_element_type=F32)
    m = _sigmoid(ga_ref[...].astype(F32)) * y_a + _sigmoid(gb_ref[...].astype(F32)) * y_b
    x = x_ref[...] + _dot(m, wo_ref[...])

    d = x.shape[1]
    hd = d // CA_HEADS
    q = _dot(_rmsnorm(x, nw_ref[...]), wq_ref[...])
    outs = []
    for h in range(CA_HEADS):
        k_h = kv_ref[:, h * hd:(h + 1) * hd]
        v_h = kv_ref[:, d + h * hd:d + (h + 1) * hd]
        s = _dot_nt(q[:, h * hd:(h + 1) * hd], k_h) * (hd ** -0.5)
        s = s - jnp.max(s, axis=-1, keepdims=True)
        e = jnp.exp(s)
        p = e / jnp.sum(e, axis=-1, keepdims=True)
        outs.append(_dot(p, v_h))
    o_ref[...] = x + _dot(jnp.concatenate(outs, axis=1), wco_ref[...])


def _merge_cross(x, gated_a, gated_b, proj_b, w_a, w_b, w_o, nw, w_q, kv, w_co, bsz, seq, n_mem, ts):
    n, d = x.shape
    ns = seq // ts
    full = lambda shape: pl.BlockSpec(shape, lambda b, s: (0,) * len(shape))
    rows = lambda width, col: pl.BlockSpec((ts, width), lambda b, s: (b * ns + s, col))
    return pl.pallas_call(
        _merge_cross_kernel,
        grid=(bsz, ns),
        in_specs=[rows(d, 0), rows(gated_a.shape[1], 0), rows(gated_b.shape[1], 0), rows(d, 4), rows(d, 5),
                  full(w_a.shape), full(w_b.shape), full(w_o.shape), full(nw.shape), full(w_q.shape),
                  pl.BlockSpec((n_mem, 2 * d), lambda b, s: (b, 0)),
                  full(w_co.shape)],
        out_specs=rows(d, 0),
        out_shape=jax.ShapeDtypeStruct((n, d), F32),
        compiler_params=_params("parallel", "parallel"),
        name="merge_cross",
    )(x, gated_a, gated_b, proj_b, proj_b, w_a, w_b, w_o, nw, w_q, kv, w_co)


def _router_kernel(h_ref, nw_ref, wr_ref, br_ref, u_ref, idx_ref, wt_ref, cnt_ref, base_ref):
    t = h_ref.shape[0]

    @pl.when(pl.program_id(0) == 0)
    def _():
        base_ref[...] = jnp.zeros_like(base_ref)

    u = _rmsnorm(h_ref[...], nw_ref[...])
    u_ref[...] = u
    logits = _dot_split(_split(u), _split(wr_ref[...])) + br_ref[...]
    lane = lax.broadcasted_iota(I32, (t, LANES), 1)
    lanef = lane.astype(F32)
    big = float(LANES)
    neg = -jnp.inf

    lg = jnp.where((lane >= N_EXPERTS) & (lane < N_EXPERTS + N_GROUPS), logits, neg)
    gmax = jnp.max(lg, axis=-1, keepdims=True)
    grp = jnp.min(jnp.where(lg == gmax, lanef - float(N_EXPERTS), big), axis=-1, keepdims=True)
    p_grp = 1.0 / jnp.sum(jnp.exp(lg - gmax), axis=-1, keepdims=True)

    in_grp = (lane < N_EXPERTS) & ((lane // GROUP_SIZE).astype(F32) == grp)
    le = jnp.where(in_grp, logits, neg)
    m1 = jnp.max(le, axis=-1, keepdims=True)
    i1 = jnp.min(jnp.where(le == m1, lanef, big), axis=-1, keepdims=True)
    le2 = jnp.where(lanef == i1, neg, le)
    m2 = jnp.max(le2, axis=-1, keepdims=True)
    i2 = jnp.min(jnp.where(le2 == m2, lanef, big), axis=-1, keepdims=True)
    ratio = jnp.exp(m2 - m1)
    p1 = 1.0 / (1.0 + ratio)
    p2 = ratio * p1

    oh1 = lanef == i1
    oh2 = lanef == i2
    onehot = jnp.where(oh1 | oh2, 1.0, 0.0)
    r_i = lax.broadcasted_iota(I32, (t, t), 0)
    c_i = lax.broadcasted_iota(I32, (t, t), 1)
    before = jnp.where(r_i > c_i, 1.0, 0.0)
    rank = _dot(before, onehot) + base_ref[...]
    r1 = jnp.sum(jnp.where(oh1, rank, 0.0), axis=-1, keepdims=True)
    r2 = jnp.sum(jnp.where(oh2, rank, 0.0), axis=-1, keepdims=True)
    base_ref[...] = base_ref[...] + jnp.sum(onehot, axis=0, keepdims=True)

    idx = jnp.where(lane == 0, i1, jnp.where(lane == 1, i2, jnp.where(lane == 2, r1, jnp.where(lane == 3, r2, 0.0))))
    idx_ref[...] = idx.astype(I32)
    wt_ref[...] = jnp.where(lane == 0, p_grp * p1, jnp.where(lane == 1, p_grp * p2, 0.0))
    cnt_ref[...] = jnp.broadcast_to(base_ref[...], cnt_ref.shape)


def _router(h, nw, wr, br, t):
    n, d = h.shape
    full = lambda shape: pl.BlockSpec(shape, lambda i: (0,) * len(shape))
    return pl.pallas_call(
        _router_kernel,
        grid=(n // t,),
        in_specs=[pl.BlockSpec((t, d), lambda i: (i, 0)), full(nw.shape), full(wr.shape), full(br.shape)],
        out_specs=[pl.BlockSpec((t, d), lambda i: (i, 0)),
                   pl.BlockSpec((t, LANES), lambda i: (i, 0)),
                   pl.BlockSpec((t, LANES), lambda i: (i, 0)),
                   pl.BlockSpec((SUBLANES, LANES), lambda i: (0, 0))],
        out_shape=[jax.ShapeDtypeStruct((n, d), F32),
                   jax.ShapeDtypeStruct((n, LANES), I32),
                   jax.ShapeDtypeStruct((n, LANES), F32),
                   jax.ShapeDtypeStruct((SUBLANES, LANES), F32)],
        scratch_shapes=[pltpu.VMEM((1, LANES), F32)],
        compiler_params=_params("arbitrary"),
        name="router",
    )(h, nw, wr, br)


def _row_copy(src, src_row, dst, dst_row, sem):
    return pltpu.make_async_copy(src.at[pl.ds(src_row, 1)], dst.at[pl.ds(dst_row, 1)], sem)


def _dispatch_kernel(dest_ref, pad_ref, u_ref, xs_ref, zeros, sem, zero_sem):
    t = u_ref.shape[0]

    @pl.when(pl.program_id(0) == 0)
    def _():
        zeros[...] = jnp.zeros_like(zeros)

        def zero_block(j):
            rows = pl.ds(pl.multiple_of(j * MOE_BLOCK, MOE_BLOCK), MOE_BLOCK)
            return pltpu.make_async_copy(zeros, xs_ref.at[rows], zero_sem)

        def start_zero(j, carry):
            @pl.when(pad_ref[0, j] != 0)
            def _():
                zero_block(j).start()
            return carry

        def wait_zero(j, carry):
            @pl.when(pad_ref[0, j] != 0)
            def _():
                zero_block(j).wait()
            return carry

        lax.fori_loop(0, pad_ref.shape[1], start_zero, 0)
        lax.fori_loop(0, pad_ref.shape[1], wait_zero, 0)

    def start(r, carry):
        _row_copy(u_ref, r, xs_ref, dest_ref[0, 0, 2 * r], sem).start()
        _row_copy(u_ref, r, xs_ref, dest_ref[0, 0, 2 * r + 1], sem).start()
        return carry

    def wait(r, carry):
        _row_copy(u_ref, r, xs_ref, dest_ref[0, 0, 2 * r], sem).wait()
        _row_copy(u_ref, r, xs_ref, dest_ref[0, 0, 2 * r + 1], sem).wait()
        return carry

    lax.fori_loop(0, t, start, 0, unroll=DMA_UNROLL)
    lax.fori_loop(0, t, wait, 0, unroll=DMA_UNROLL)


def _dispatch(dest3, has_padding, u, t):
    n, d = u.shape
    n_blocks = has_padding.shape[1]
    return pl.pallas_call(
        _dispatch_kernel,
        grid=(n // t,),
        in_specs=[pl.BlockSpec((1, 1, 2 * t), lambda i: (i, 0, 0), memory_space=pltpu.SMEM),
                  pl.BlockSpec((1, n_blocks), lambda i: (0, 0), memory_space=pltpu.SMEM),
                  pl.BlockSpec((t, d), lambda i: (i, 0))],
        out_specs=pl.BlockSpec(memory_space=pl.ANY),
        out_shape=jax.ShapeDtypeStruct((n_blocks * MOE_BLOCK, d), F32),
        scratch_shapes=[pltpu.VMEM((MOE_BLOCK, d), F32), pltpu.SemaphoreType.DMA(()), pltpu.SemaphoreType.DMA(())],
        compiler_params=_params("arbitrary"),
        name="moe_dispatch",
    )(dest3, has_padding, u)


def _expert_kernel(be_ref, nu_ref, xs_ref, wg_ref, wu_ref, wd_ref, y_ref, wg_bf, wu_bf, wd_bf):
    j = pl.program_id(0)
    used = j < nu_ref[0]
    new_expert = (j == 0) | (be_ref[j] != be_ref[jnp.maximum(j - 1, 0)])

    @pl.when(used & new_expert)
    def _():
        wg_bf[...] = wg_ref[0].astype(BF16)
        wu_bf[...] = wu_ref[0].astype(BF16)
        wd_bf[...] = wd_ref[0].astype(BF16)

    @pl.when(used)
    def _():
        x = xs_ref[...].astype(BF16)
        hid = _silu(jnp.dot(x, wg_bf[...], preferred_element_type=F32)) * jnp.dot(
            x, wu_bf[...], preferred_element_type=F32)
        y_ref[...] = _dot(hid, wd_bf[...])

    @pl.when(jnp.logical_not(used))
    def _():
        y_ref[...] = jnp.zeros_like(y_ref)


def _experts(block_e, n_used, xs, w_gate, w_up, w_down):
    rows, d = xs.shape
    de = w_gate.shape[2]
    grid_spec = pltpu.PrefetchScalarGridSpec(
        num_scalar_prefetch=2,
        grid=(rows // MOE_BLOCK,),
        in_specs=[pl.BlockSpec((MOE_BLOCK, d), lambda j, be, nu: (jnp.minimum(j, nu[0] - 1), 0)),
                  pl.BlockSpec((1, d, de), lambda j, be, nu: (be[j], 0, 0)),
                  pl.BlockSpec((1, d, de), lambda j, be, nu: (be[j], 0, 0)),
                  pl.BlockSpec((1, de, d), lambda j, be, nu: (be[j], 0, 0))],
        out_specs=pl.BlockSpec((MOE_BLOCK, d), lambda j, be, nu: (j, 0)),
        scratch_shapes=[pltpu.VMEM((d, de), BF16), pltpu.VMEM((d, de), BF16), pltpu.VMEM((de, d), BF16)],
    )
    return pl.pallas_call(
        _expert_kernel,
        grid_spec=grid_spec,
        out_shape=jax.ShapeDtypeStruct((rows, d), F32),
        compiler_params=_params("arbitrary"),
        name="moe_experts",
    )(block_e, n_used, xs, w_gate, w_up, w_down)


def _combine_kernel(dest_ref, h_ref, wt_ref, nw_ref, y_ref, o_ref, buf, sem, *, final_norm):
    t = h_ref.shape[0]

    def start(r, carry):
        _row_copy(y_ref, dest_ref[0, 0, 2 * r], buf.at[0], r, sem).start()
        _row_copy(y_ref, dest_ref[0, 0, 2 * r + 1], buf.at[1], r, sem).start()
        return carry

    def wait(r, carry):
        _row_copy(y_ref, dest_ref[0, 0, 2 * r], buf.at[0], r, sem).wait()
        _row_copy(y_ref, dest_ref[0, 0, 2 * r + 1], buf.at[1], r, sem).wait()
        return carry

    lax.fori_loop(0, t, start, 0, unroll=DMA_UNROLL)
    lax.fori_loop(0, t, wait, 0, unroll=DMA_UNROLL)
    wt = wt_ref[...]
    moe = buf[0] * wt[:, 0:1] + buf[1] * wt[:, 1:2]
    out = h_ref[...] + moe
    o_ref[...] = _rmsnorm(out, nw_ref[...]) if final_norm else out


def _combine(dest3, h, wt, nw, y, t, final_norm):
    n, d = h.shape
    return pl.pallas_call(
        functools.partial(_combine_kernel, final_norm=final_norm),
        grid=(n // t,),
        in_specs=[pl.BlockSpec((1, 1, 2 * t), lambda i: (i, 0, 0), memory_space=pltpu.SMEM),
                  pl.BlockSpec((t, d), lambda i: (i, 0)),
                  pl.BlockSpec((t, LANES), lambda i: (i, 0)),
                  pl.BlockSpec((1, d), lambda i: (0, 0)),
                  pl.BlockSpec(memory_space=pl.ANY)],
        out_specs=pl.BlockSpec((t, d), lambda i: (i, 0)),
        out_shape=jax.ShapeDtypeStruct((n, d), F32),
        scratch_shapes=[pltpu.VMEM((2, t, d), F32), pltpu.SemaphoreType.DMA(())],
        compiler_params=_params("arbitrary"),
        name="moe_combine",
    )(dest3, h, wt, nw, y)


def _tile(n, pref):
    return pref if n % pref == 0 else n


def kernel(x, mem, norm1_w, w_in, rnn_conv_w, rnn_conv_b, rglru_wa, rglru_ba, rglru_wx, rglru_bx, rglru_lambda, w_branch_a, dn_conv_w, dn_a_log, dn_dt_bias, dn_norm_w, w_branch_b, w_out, norm2_w, mem_norm_w, w_cq, w_ckv, w_co, norm3_w, w_router_group, b_router_group, w_router_expert, b_router_expert, w_exp_gate, w_exp_up, w_exp_down, norm_f_w):
    bsz, seq, d = x.shape
    n = bsz * seq
    n_mem = mem.shape[1]
    depth = w_in.shape[0]
    d_rnn = rnn_conv_w.shape[2]
    n_heads = dn_a_log.shape[1]
    dn_w = n_heads * HEAD_DIM
    row = lambda v: v.reshape(1, -1).astype(F32)

    h = x.reshape(n, d)
    mem2 = mem.reshape(bsz * n_mem, d)
    tm = _tile(n, 2048)
    for l in range(depth):
        o_rg, o_qkv, o_z = d_rnn, 2 * d_rnn, 2 * d_rnn + 3 * dn_w
        o_a = o_z + dn_w
        o_ga = o_a + 2 * n_heads
        wi = w_in[l]
        w_cat = jnp.concatenate([wi[:, :o_a], wi[:, o_ga:]], axis=1).astype(BF16)
        w_ab = jnp.pad(wi[:, o_a:o_ga], ((0, 0), (0, LANES - 2 * n_heads))).astype(BF16)
        proj_a, proj_b, ab = _in_proj(h, row(norm1_w[l]), w_cat, w_ab, o_qkv, tm, 512)

        gated_a = _rglru(proj_a, bsz, seq, rnn_conv_w[l], row(rnn_conv_b[l]), rglru_wa[l].astype(BF16),
                         row(rglru_ba[l]), rglru_wx[l].astype(BF16), row(rglru_bx[l]), row(rglru_lambda[l]),
                         _tile(seq, 512))

        a_dec = jnp.exp(dn_a_log[l].astype(F32))
        pad_h = lambda v: jnp.pad(v, (0, LANES - n_heads))
        acol, dcol = row(pad_h(a_dec)), row(pad_h(dn_dt_bias[l]))
        arow = jnp.broadcast_to(jnp.pad(a_dec, (0, n_heads))[:, None], (2 * n_heads, GDN_CHUNK))
        drow = jnp.broadcast_to(jnp.pad(dn_dt_bias[l], (0, n_heads))[:, None], (2 * n_heads, GDN_CHUNK))
        abt = ab[:, :2 * n_heads].T
        gated_b = _gdn(proj_b, ab, abt, bsz, seq, dn_conv_w[l], acol, dcol, arow, drow, row(dn_norm_w[l]), n_heads)

        kv = _norm_mm(mem2, row(mem_norm_w[l]), w_ckv[l].astype(BF16), BF16, _tile(bsz * n_mem, 1024), 512)
        h = _merge_cross(h, gated_a, gated_b, proj_b, w_branch_a[l].astype(BF16), w_branch_b[l].astype(BF16),
                         w_out[l].astype(BF16), row(norm2_w[l]), w_cq[l].astype(BF16), kv, w_co[l].astype(BF16),
                         bsz, seq, n_mem, _tile(seq, 512))

        w_r = jnp.pad(jnp.concatenate([w_router_expert[l], w_router_group[l]], axis=1),
                      ((0, 0), (0, LANES - N_EXPERTS - N_GROUPS)))
        b_r = row(jnp.pad(jnp.concatenate([b_router_expert[l], b_router_group[l]]), (0, LANES - N_EXPERTS - N_GROUPS)))
        u3, idx, wt, cnt = _router(h, row(norm3_w[l]), w_r, b_r, _tile(n, 512))

        counts = cnt[0, :N_EXPERTS].astype(I32)
        padded = (counts + MOE_BLOCK - 1) // MOE_BLOCK * MOE_BLOCK
        pend = jnp.cumsum(padded)
        pstart = pend - padded
        is_expert = idx[:, 0:2, None] == jnp.arange(N_EXPERTS, dtype=I32)
        dest = jnp.sum(jnp.where(is_expert, pstart, 0), axis=-1) + idx[:, 2:4]
        n_blocks = (2 * n + N_EXPERTS * (MOE_BLOCK - 1)) // MOE_BLOCK
        block_row = jnp.arange(n_blocks, dtype=I32) * MOE_BLOCK
        block_e = jnp.minimum(jnp.sum((pend[None, :] <= block_row[:, None]).astype(I32), axis=1), N_EXPERTS - 1)
        n_used = (pend[-1:] // MOE_BLOCK).astype(I32)
        t_moe = _tile(n, 512)
        dest3 = dest.reshape(n // t_moe, 1, 2 * t_moe)
        block_id = jnp.arange(n_blocks, dtype=I32)
        next_e = block_e[jnp.minimum(block_id + 1, n_blocks - 1)]
        has_padding = ((block_id >= n_used[0] - 1) | (block_e != next_e)).astype(I32).reshape(1, n_blocks)
        xs = _dispatch(dest3, has_padding, u3, t_moe)
        yb = _experts(block_e, n_used, xs, w_exp_gate[l], w_exp_up[l], w_exp_down[l])
        h = _combine(dest3, h, wt, row(norm_f_w), yb, t_moe, final_norm=(l == depth - 1))
    return h.reshape(bsz, seq, d)
```

```python
import functools
import math

import jax
import jax.numpy as jnp
from jax import lax
from jax.experimental import pallas as pl
from jax.experimental.pallas import tpu as pltpu

F32 = jnp.float32
BF16 = jnp.bfloat16
I32 = jnp.int32
HIGHEST = lax.Precision.HIGHEST

NORM_EPS = 1e-6
CONV_TAPS = 4
RNN_BLOCK = 128
RG_POWER = 8.0
HEAD_DIM = 128
GDN_CHUNK = 128
GDN_CHUNKS_PER_STEP = 2
CA_HEADS = 4
N_GROUPS = 8
GROUP_SIZE = 8
N_EXPERTS = N_GROUPS * GROUP_SIZE
MOE_BLOCK = 512
LANES = 128
SUBLANES = 8
DMA_UNROLL = 8
VMEM_LIMIT = 48 * 1024 * 1024


def _params(*semantics):
    return pltpu.CompilerParams(dimension_semantics=semantics, vmem_limit_bytes=VMEM_LIMIT)


def _dot(a, b):
    return jnp.dot(a.astype(BF16), b.astype(BF16), preferred_element_type=F32)


def _dot_nt(a, b):
    return lax.dot_general(a.astype(BF16), b.astype(BF16), (((1,), (1,)), ((), ())),
                           preferred_element_type=F32)


def _dot_tn(a, b):
    return lax.dot_general(a.astype(BF16), b.astype(BF16), (((0,), (0,)), ((), ())),
                           preferred_element_type=F32)


def _dot_f32(a, b):
    return jnp.dot(a, b, precision=HIGHEST, preferred_element_type=F32)


def _split(x):
    hi = x.astype(BF16)
    return hi, (x - hi.astype(F32)).astype(BF16)


def _dot_split(a_parts, b_parts):
    (a_hi, a_lo), (b_hi, b_lo) = a_parts, b_parts
    dot = functools.partial(jnp.dot, preferred_element_type=F32)
    return dot(a_hi, b_hi) + (dot(a_hi, b_lo) + dot(a_lo, b_hi))


def _rmsnorm(x, w):
    return x * lax.rsqrt(jnp.mean(x * x, axis=-1, keepdims=True) + NORM_EPS) * w


def _sigmoid(x):
    return 0.5 * jnp.tanh(0.5 * x) + 0.5


def _silu(x):
    half = 0.5 * x
    return half + half * jnp.tanh(half)


def _softplus(x):
    return jnp.maximum(x, 0.0) + jnp.log(1.0 + jnp.exp(-jnp.abs(x)))


def _one_minus_exp2(y, exp_y):
    return jnp.tanh(-y) * (1.0 + exp_y * exp_y)


def _gelu_tanh(x):
    return 0.5 * x * (1.0 + jnp.tanh(math.sqrt(2.0 / math.pi) * (x + 0.044715 * (x * x * x))))


def _norm_mm_kernel(x_ref, nw_ref, w_ref, o_ref, u_ref):
    @pl.when(pl.program_id(1) == 0)
    def _():
        u_ref[...] = _rmsnorm(x_ref[...], nw_ref[...]).astype(BF16)

    o_ref[...] = jnp.dot(u_ref[...], w_ref[...], preferred_element_type=F32).astype(o_ref.dtype)


def _norm_mm(x, nw, w, out_dtype, tm, tn):
    n, d = x.shape
    c = w.shape[1]
    return pl.pallas_call(
        _norm_mm_kernel,
        grid=(n // tm, c // tn),
        in_specs=[pl.BlockSpec((tm, d), lambda i, j: (i, 0)),
                  pl.BlockSpec((1, d), lambda i, j: (0, 0)),
                  pl.BlockSpec((d, tn), lambda i, j: (0, j))],
        out_specs=pl.BlockSpec((tm, tn), lambda i, j: (i, j)),
        out_shape=jax.ShapeDtypeStruct((n, c), out_dtype),
        scratch_shapes=[pltpu.VMEM((tm, d), BF16)],
        compiler_params=_params("parallel", "arbitrary"),
        name="norm_mm",
    )(x, nw, w)


def _in_proj_kernel(x_ref, nw_ref, w_ref, wab_ref, oa_ref, ob_ref, ab_ref, u_ref, *, n_a):
    j = pl.program_id(1)

    @pl.when(j == 0)
    def _():
        u_ref[...] = _rmsnorm(x_ref[...], nw_ref[...]).astype(BF16)
        ab_ref[...] = jnp.dot(u_ref[...], wab_ref[...], preferred_element_type=F32)

    @pl.when(j < n_a)
    def _():
        oa_ref[...] = jnp.dot(u_ref[...], w_ref[...], preferred_element_type=F32).astype(oa_ref.dtype)

    @pl.when(j >= n_a)
    def _():
        ob_ref[...] = jnp.dot(u_ref[...], w_ref[...], preferred_element_type=F32).astype(ob_ref.dtype)


def _in_proj(x, nw, w_cat, w_ab, c_a, tm, tn):
    n, d = x.shape
    c = w_cat.shape[1]
    n_a, n_b = c_a // tn, (c - c_a) // tn
    return pl.pallas_call(
        functools.partial(_in_proj_kernel, n_a=n_a),
        grid=(n // tm, n_a + n_b),
        in_specs=[pl.BlockSpec((tm, d), lambda i, j: (i, 0)),
                  pl.BlockSpec((1, d), lambda i, j: (0, 0)),
                  pl.BlockSpec((d, tn), lambda i, j: (0, j)),
                  pl.BlockSpec(w_ab.shape, lambda i, j: (0, 0))],
        out_specs=[pl.BlockSpec((tm, tn), lambda i, j: (i, jnp.minimum(j, n_a - 1))),
                   pl.BlockSpec((tm, tn), lambda i, j: (i, jnp.maximum(j - n_a, 0))),
                   pl.BlockSpec((tm, w_ab.shape[1]), lambda i, j: (i, 0))],
        out_shape=[jax.ShapeDtypeStruct((n, c_a), BF16), jax.ShapeDtypeStruct((n, c - c_a), BF16),
                   jax.ShapeDtypeStruct((n, w_ab.shape[1]), F32)],
        scratch_shapes=[pltpu.VMEM((tm, d), BF16)],
        compiler_params=_params("arbitrary", "arbitrary"),
        name="in_proj",
    )(x, nw, w_cat, w_ab)


def _load_conv_window(x_ref, xbuf, ts):
    @pl.when(pl.program_id(1) == 0)
    def _():
        xbuf[0:SUBLANES, :] = jnp.zeros((SUBLANES, xbuf.shape[1]), F32)

    @pl.when(pl.program_id(1) != 0)
    def _():
        xbuf[0:SUBLANES, :] = xbuf[ts:ts + SUBLANES, :]

    xbuf[SUBLANES:SUBLANES + ts, :] = x_ref[...].astype(F32)


def _causal_conv(xbuf, cw_ref, ts, first_row=0, cols=slice(None)):
    base = first_row + SUBLANES - (CONV_TAPS - 1)
    acc = cw_ref[0:1, cols] * xbuf[base:base + ts, cols]
    for k in range(1, CONV_TAPS):
        acc = acc + cw_ref[k:k + 1, cols] * xbuf[base + k:base + k + ts, cols]
    return acc


def _rglru_kernel(rx_ref, rg_ref, cw_ref, cb_ref, wa_ref, ba_ref, wx_ref, bx_ref, lam_ref, o_ref,
                  xbuf, a_ref, b_ref, carry_ref):
    ts, c = rx_ref.shape
    _load_conv_window(rx_ref, xbuf, ts)

    @pl.when(pl.program_id(1) == 0)
    def _():
        carry_ref[...] = jnp.zeros_like(carry_ref)

    xc = _causal_conv(xbuf, cw_ref, ts) + cb_ref[...]
    neg_sp = -RG_POWER * _softplus(-lam_ref[...])
    for n in range(c // RNN_BLOCK):
        sl = slice(n * RNN_BLOCK, (n + 1) * RNN_BLOCK)
        xb = xc[:, sl]
        r = _sigmoid(_dot(xb, wa_ref[n]) + ba_ref[:, sl])
        i = _sigmoid(_dot(xb, wx_ref[n]) + bx_ref[:, sl])
        log_a = neg_sp[:, sl] * r
        a = jnp.exp(log_a)
        a_ref[:, sl] = a
        b_ref[:, sl] = jnp.sqrt(_one_minus_exp2(log_a, a)) * (i * xb)

    row = lax.broadcasted_iota(I32, (SUBLANES, c), 0)

    def slab(t, carry):
        rows = pl.ds(pl.multiple_of(t * SUBLANES, SUBLANES), SUBLANES)
        a = a_ref[rows, :]
        b = b_ref[rows, :]
        for d in (1, 2, 4):
            a_sh = jnp.where(row >= d, pltpu.roll(a, d, 0), 1.0)
            b_sh = jnp.where(row >= d, pltpu.roll(b, d, 0), 0.0)
            b = a * b_sh + b
            a = a * a_sh
        h = a * carry + b
        b_ref[rows, :] = h
        return h[SUBLANES - 1:SUBLANES, :]

    carry_ref[...] = lax.fori_loop(0, ts // SUBLANES, slab, carry_ref[...])
    o_ref[...] = (_gelu_tanh(rg_ref[...].astype(F32)) * b_ref[...]).astype(o_ref.dtype)


def _rglru(proj_a, bsz, seq, cw, cb, wa, ba, wx, bx, lam, ts):
    c = cw.shape[1]
    ns = seq // ts
    full = lambda shape: pl.BlockSpec(shape, lambda b, s: (0,) * len(shape))
    return pl.pallas_call(
        _rglru_kernel,
        grid=(bsz, ns),
        in_specs=[pl.BlockSpec((ts, c), lambda b, s: (b * ns + s, 0)),
                  pl.BlockSpec((ts, c), lambda b, s: (b * ns + s, 1)),
                  full(cw.shape), full(cb.shape), full(wa.shape), full(ba.shape),
                  full(wx.shape), full(bx.shape), full(lam.shape)],
        out_specs=pl.BlockSpec((ts, c), lambda b, s: (b * ns + s, 0)),
        out_shape=jax.ShapeDtypeStruct((bsz * seq, c), BF16),
        scratch_shapes=[pltpu.VMEM((ts + SUBLANES, c), F32), pltpu.VMEM((ts, c), F32),
                        pltpu.VMEM((ts, c), F32), pltpu.VMEM((1, c), F32)],
        compiler_params=_params("parallel", "arbitrary"),
        name="rglru",
    )(proj_a, proj_a, cw, cb, wa, ba, wx, bx, lam)


def _lane_bcast(x, lane):
    return jnp.broadcast_to(x[:, lane:lane + 1], x.shape)


def _unit_lower_inverses(neg_ls, fillers):
    ts = neg_ls[0].shape[0]
    half = ts // 2
    run_filler = lambda: next(fillers, lambda: None)()
    lane = lax.broadcasted_iota(I32, (half, ts), 1)
    left = lane < half
    left_of = lambda x: jnp.where(left, x, 0.0)
    right_of = lambda x: jnp.where(left, 0.0, x)
    left_mask, right_mask = left_of(1.0).astype(BF16), right_of(1.0).astype(BF16)
    stack = lambda top, bottom: tuple(jnp.concatenate([a, b], axis=0) for a, b in zip(top, bottom))
    diag = lambda parts: stack([p * left_mask for p in parts], [p * right_mask for p in parts])
    eye_pair = ((lane & (half - 1)) == lax.broadcasted_iota(I32, (half, ts), 0)).astype(F32)

    pairs = [n[:half] + right_of(n[half:]) for n in neg_ls]
    couplings = [left_of(n[half:]) for n in neg_ls]
    ts_mats = [eye_pair + ab for ab in pairs]
    pair_parts = [_split(ab) for ab in pairs]
    ps = [_dot_split(pp, diag(pp)) for pp in pair_parts]
    run_filler()
    levels = int(math.log2(half)) - 1
    for lvl in range(levels - 1):
        p_parts = [_split(p) for p in ps]
        both = [_dot_split(stack(_split(t), pp), diag(pp)) for t, pp in zip(ts_mats, p_parts)]
        ts_mats = [t + b[:half] for t, b in zip(ts_mats, both)]
        ps = [b[half:] for b in both]
        run_filler()
    ts_mats = [t + _dot_split(_split(t), diag(_split(p))) for t, p in zip(ts_mats, ps)]
    run_filler()
    t_parts = [_split(t) for t in ts_mats]
    c_ta = [_dot_split(_split(c), diag(tp)) for c, tp in zip(couplings, t_parts)]
    run_filler()
    below = lambda parts: stack([jnp.zeros_like(p) for p in parts], parts)
    tb_c_ta = [_dot_split(tp, below(_split(m))) for tp, m in zip(t_parts, c_ta)]
    return [jnp.concatenate([left_of(t), jnp.where(left, x, t)], axis=0) for t, x in zip(ts_mats, tb_c_ta)]


def _gdn_kernel(qkv_ref, z_ref, ab_ref, abt_ref, cw_ref, acol_ref, dcol_ref, arow_ref, drow_ref, nw_ref,
                o_ref, xbuf, s_ref):
    ts = GDN_CHUNK
    n_heads = s_ref.shape[0]
    dn_w = n_heads * HEAD_DIM
    _load_conv_window(qkv_ref, xbuf, qkv_ref.shape[0])

    @pl.when(pl.program_id(1) == 0)
    def _():
        s_ref[...] = jnp.zeros_like(s_ref)

    row = lax.broadcasted_iota(I32, (ts, ts), 0)
    col = lax.broadcasted_iota(I32, (ts, ts), 1)
    lower_incl = (row >= col).astype(F32)
    upper_incl = (row <= col).astype(F32)
    head_cols = lambda h, part: slice(part * dn_w + h * HEAD_DIM, part * dn_w + (h + 1) * HEAD_DIM)
    heads = range(n_heads)

    n_chunks = qkv_ref.shape[0] // ts
    state = [dict(q=[None] * n_heads, k=[None] * n_heads, k_beta=[None] * n_heads, decay=[None] * n_heads,
                  rhs=[None] * n_heads, qd=[None] * n_heads, kd=[None] * n_heads) for _ in range(n_chunks)]

    def prepare_gates(c):
        st = state[c]
        rows = slice(c * ts, (c + 1) * ts)
        ab = ab_ref[rows, :]
        g_col = -acol_ref[...] * _softplus(ab + dcol_ref[...])
        st["beta"] = _sigmoid(ab)
        g_row = -arow_ref[...] * _softplus(abt_ref[:, rows] + drow_ref[...])
        st["cum_col"] = _dot_f32(lower_incl, g_col)
        st["cum_row"] = _dot_f32(g_row, upper_incl)
        last = st["cum_col"][ts - 1:ts, :]
        st["exp_cum"] = jnp.exp(st["cum_col"])
        st["exp_rem"] = jnp.exp(jnp.broadcast_to(last, (ts, LANES)) - st["cum_col"])
        st["exp_last"] = jnp.exp(last)

    def conv_silu(c, h, part):
        return _silu(_causal_conv(xbuf, cw_ref, ts, c * ts, head_cols(h, part)))

    def prepare_key(c, h):
        st = state[c]
        k = conv_silu(c, h, 1)
        k = k * lax.rsqrt(jnp.sum(k * k, axis=-1, keepdims=True) + NORM_EPS)
        st["k"][h], st["k_beta"][h] = k, k * _lane_bcast(st["beta"], n_heads + h)
        st["decay"][h] = jnp.exp(jnp.minimum(_lane_bcast(st["cum_col"], h) - st["cum_row"][h:h + 1, :], 0.0))

    def prepare_query_value(c, h):
        st = state[c]
        q, v = conv_silu(c, h, 0), conv_silu(c, h, 2)
        q = q * (lax.rsqrt(jnp.sum(q * q, axis=-1, keepdims=True) + NORM_EPS) * (HEAD_DIM ** -0.5))
        e_cum = _lane_bcast(st["exp_cum"], h)
        st["q"][h] = q
        st["rhs"][h] = jnp.concatenate([v * _lane_bcast(st["beta"], n_heads + h), st["k_beta"][h] * e_cum], axis=1)
        st["qd"][h] = q * e_cum
        st["kd"][h] = st["k"][h] * _lane_bcast(st["exp_rem"], h)

    def key_thunks(c):
        if c >= n_chunks:
            return []
        return [functools.partial(prepare_gates, c)] + [functools.partial(prepare_key, c, h) for h in heads]

    def solve(c):
        st = state[c]
        kks = [_dot_nt(st["k_beta"][h], st["k"][h]) for h in heads]
        neg_ls = [jnp.where(row > col, -(kks[h] * st["decay"][h]), 0.0) for h in heads]
        thunks = [functools.partial(prepare_query_value, c, h) for h in heads] + key_thunks(c + 1)
        slots = int(math.log2(ts))
        per_slot = -(-len(thunks) // slots)
        groups = [thunks[i:i + per_slot] for i in range(0, len(thunks), per_slot)]
        fillers = iter([functools.partial(lambda g: [t() for t in g], g) for g in groups])
        t_mats = _unit_lower_inverses(neg_ls, fillers)
        for group in fillers:
            group()
        qks = [_dot_nt(st["q"][h], st["k"][h]) for h in heads]
        st["intra"] = [jnp.where(row >= col, qks[h] * st["decay"][h], 0.0) for h in heads]
        st["uw"] = [_dot(t_mats[h], st["rhs"][h]) for h in heads]

    def advance(c):
        st = state[c]
        rows = slice(c * ts, (c + 1) * ts)
        ws_qs = [_dot(jnp.concatenate([st["uw"][h][:, HEAD_DIM:], st["qd"][h]], axis=0), s_ref[h]) for h in heads]
        v_news = [st["uw"][h][:, :HEAD_DIM] - ws_qs[h][:ts] for h in heads]
        mixed = [_dot(jnp.concatenate([st["intra"][h], st["kd"][h].T], axis=0), v_news[h]) for h in heads]
        for h in heads:
            s_ref[h] = (s_ref[h] * _lane_bcast(jnp.broadcast_to(st["exp_last"], (HEAD_DIM, LANES)), h)
                        + mixed[h][ts:])
            o = _rmsnorm(ws_qs[h][ts:] + mixed[h][:ts], nw_ref[...]) * _silu(z_ref[rows, head_cols(h, 0)].astype(F32))
            o_ref[rows, head_cols(h, 0)] = o.astype(o_ref.dtype)

    for thunk in key_thunks(0):
        thunk()
    for c in range(n_chunks):
        solve(c)
        advance(c)


def _gdn(proj_b, ab, abt, bsz, seq, cw, acol, dcol, arow, drow, nw, n_heads):
    ts = GDN_CHUNK * GDN_CHUNKS_PER_STEP
    assert seq % ts == 0
    ns = seq // ts
    dn_w = n_heads * HEAD_DIM
    full = lambda shape: pl.BlockSpec(shape, lambda b, s: (0,) * len(shape))
    return pl.pallas_call(
        _gdn_kernel,
        grid=(bsz, ns),
        in_specs=[pl.BlockSpec((ts, 3 * dn_w), lambda b, s: (b * ns + s, 0)),
                  pl.BlockSpec((ts, dn_w), lambda b, s: (b * ns + s, 3)),
                  pl.BlockSpec((ts, LANES), lambda b, s: (b * ns + s, 0)),
                  pl.BlockSpec((2 * n_heads, ts), lambda b, s: (0, b * ns + s)),
                  full(cw.shape), full(acol.shape), full(dcol.shape), full(arow.shape), full(drow.shape),
                  full(nw.shape)],
        out_specs=pl.BlockSpec((ts, dn_w), lambda b, s: (b * ns + s, 0)),
        out_shape=jax.ShapeDtypeStruct((bsz * seq, dn_w), BF16),
        scratch_shapes=[pltpu.VMEM((ts + SUBLANES, 3 * dn_w), F32),
                        pltpu.VMEM((n_heads, HEAD_DIM, HEAD_DIM), F32)],
        compiler_params=_params("parallel", "arbitrary"),
        name="gdn",
    )(proj_b, proj_b, ab, abt, cw, acol, dcol, arow, drow, nw)


def _merge_cross_kernel(x_ref, ya_ref, yb_ref, ga_ref, gb_ref, wa_ref, wb_ref, wo_ref,
                        nw_ref, wq_ref, kv_ref, wco_ref, o_ref):
    y_a = jnp.dot(ya_ref[...], wa_ref[...], preferred_element_type=F32)
    y_b = jnp.dot(yb_ref[...], wb_ref[...], preferred_element_type=F32)
    m = _sigmoid(ga_ref[...].astype(F32)) * y_a + _sigmoid(gb_ref[...].astype(F32)) * y_b
    x = x_ref[...] + _dot(m, wo_ref[...])

    d = x.shape[1]
    hd = d // CA_HEADS
    q = _dot(_rmsnorm(x, nw_ref[...]), wq_ref[...])
    outs = []
    for h in range(CA_HEADS):
        k_h = kv_ref[:, h * hd:(h + 1) * hd]
        v_h = kv_ref[:, d + h * hd:d + (h + 1) * hd]
        s = _dot_nt(q[:, h * hd:(h + 1) * hd], k_h) * (hd ** -0.5)
        s = s - jnp.max(s, axis=-1, keepdims=True)
        e = jnp.exp(s)
        p = e / jnp.sum(e, axis=-1, keepdims=True)
        outs.append(_dot(p, v_h))
    o_ref[...] = x + _dot(jnp.concatenate(outs, axis=1), wco_ref[...])


def _merge_cross(x, gated_a, gated_b, proj_b, w_a, w_b, w_o, nw, w_q, kv, w_co, bsz, seq, n_mem, ts):
    n, d = x.shape
    ns = seq // ts
    full = lambda shape: pl.BlockSpec(shape, lambda b, s: (0,) * len(shape))
    rows = lambda width, col: pl.BlockSpec((ts, width), lambda b, s: (b * ns + s, col))
    return pl.pallas_call(
        _merge_cross_kernel,
        grid=(bsz, ns),
        in_specs=[rows(d, 0), rows(gated_a.shape[1], 0), rows(gated_b.shape[1], 0), rows(d, 4), rows(d, 5),
                  full(w_a.shape), full(w_b.shape), full(w_o.shape), full(nw.shape), full(w_q.shape),
                  pl.BlockSpec((n_mem, 2 * d), lambda b, s: (b, 0)),
                  full(w_co.shape)],
        out_specs=rows(d, 0),
        out_shape=jax.ShapeDtypeStruct((n, d), F32),
        compiler_params=_params("parallel", "parallel"),
        name="merge_cross",
    )(x, gated_a, gated_b, proj_b, proj_b, w_a, w_b, w_o, nw, w_q, kv, w_co)


def _router_kernel(h_ref, nw_ref, wr_ref, br_ref, u_ref, idx_ref, wt_ref, cnt_ref, base_ref):
    t = h_ref.shape[0]

    @pl.when(pl.program_id(0) == 0)
    def _():
        base_ref[...] = jnp.zeros_like(base_ref)

    u = _rmsnorm(h_ref[...], nw_ref[...])
    u_ref[...] = u
    logits = _dot_split(_split(u), _split(wr_ref[...])) + br_ref[...]
    lane = lax.broadcasted_iota(I32, (t, LANES), 1)
    lanef = lane.astype(F32)
    big = float(LANES)
    neg = -jnp.inf

    lg = jnp.where((lane >= N_EXPERTS) & (lane < N_EXPERTS + N_GROUPS), logits, neg)
    gmax = jnp.max(lg, axis=-1, keepdims=True)
    grp = jnp.min(jnp.where(lg == gmax, lanef - float(N_EXPERTS), big), axis=-1, keepdims=True)
    p_grp = 1.0 / jnp.sum(jnp.exp(lg - gmax), axis=-1, keepdims=True)

    in_grp = (lane < N_EXPERTS) & ((lane // GROUP_SIZE).astype(F32) == grp)
    le = jnp.where(in_grp, logits, neg)
    m1 = jnp.max(le, axis=-1, keepdims=True)
    i1 = jnp.min(jnp.where(le == m1, lanef, big), axis=-1, keepdims=True)
    le2 = jnp.where(lanef == i1, neg, le)
    m2 = jnp.max(le2, axis=-1, keepdims=True)
    i2 = jnp.min(jnp.where(le2 == m2, lanef, big), axis=-1, keepdims=True)
    ratio = jnp.exp(m2 - m1)
    p1 = 1.0 / (1.0 + ratio)
    p2 = ratio * p1

    oh1 = lanef == i1
    oh2 = lanef == i2
    onehot = jnp.where(oh1 | oh2, 1.0, 0.0)
    r_i = lax.broadcasted_iota(I32, (t, t), 0)
    c_i = lax.broadcasted_iota(I32, (t, t), 1)
    before = jnp.where(r_i > c_i, 1.0, 0.0)
    rank = _dot(before, onehot) + base_ref[...]
    r1 = jnp.sum(jnp.where(oh1, rank, 0.0), axis=-1, keepdims=True)
    r2 = jnp.sum(jnp.where(oh2, rank, 0.0), axis=-1, keepdims=True)
    base_ref[...] = base_ref[...] + jnp.sum(onehot, axis=0, keepdims=True)

    idx = jnp.where(lane == 0, i1, jnp.where(lane == 1, i2, jnp.where(lane == 2, r1, jnp.where(lane == 3, r2, 0.0))))
    idx_ref[...] = idx.astype(I32)
    wt_ref[...] = jnp.where(lane == 0, p_grp * p1, jnp.where(lane == 1, p_grp * p2, 0.0))
    cnt_ref[...] = jnp.broadcast_to(base_ref[...], cnt_ref.shape)


def _router(h, nw, wr, br, t):
    n, d = h.shape
    full = lambda shape: pl.BlockSpec(shape, lambda i: (0,) * len(shape))
    return pl.pallas_call(
        _router_kernel,
        grid=(n // t,),
        in_specs=[pl.BlockSpec((t, d), lambda i: (i, 0)), full(nw.shape), full(wr.shape), full(br.shape)],
        out_specs=[pl.BlockSpec((t, d), lambda i: (i, 0)),
                   pl.BlockSpec((t, LANES), lambda i: (i, 0)),
                   pl.BlockSpec((t, LANES), lambda i: (i, 0)),
                   pl.BlockSpec((SUBLANES, LANES), lambda i: (0, 0))],
        out_shape=[jax.ShapeDtypeStruct((n, d), F32),
                   jax.ShapeDtypeStruct((n, LANES), I32),
                   jax.ShapeDtypeStruct((n, LANES), F32),
                   jax.ShapeDtypeStruct((SUBLANES, LANES), F32)],
        scratch_shapes=[pltpu.VMEM((1, LANES), F32)],
        compiler_params=_params("arbitrary"),
        name="router",
    )(h, nw, wr, br)


def _row_copy(src, src_row, dst, dst_row, sem):
    return pltpu.make_async_copy(src.at[pl.ds(src_row, 1)], dst.at[pl.ds(dst_row, 1)], sem)


def _dispatch_kernel(dest_ref, pad_ref, u_ref, xs_ref, zeros, sem, zero_sem):
    t = u_ref.shape[0]

    @pl.when(pl.program_id(0) == 0)
    def _():
        zeros[...] = jnp.zeros_like(zeros)

        def zero_block(j):
            rows = pl.ds(pl.multiple_of(j * MOE_BLOCK, MOE_BLOCK), MOE_BLOCK)
            return pltpu.make_async_copy(zeros, xs_ref.at[rows], zero_sem)

        def start_zero(j, carry):
            @pl.when(pad_ref[0, j] != 0)
            def _():
                zero_block(j).start()
            return carry

        def wait_zero(j, carry):
            @pl.when(pad_ref[0, j] != 0)
            def _():
                zero_block(j).wait()
            return carry

        lax.fori_loop(0, pad_ref.shape[1], start_zero, 0)
        lax.fori_loop(0, pad_ref.shape[1], wait_zero, 0)

    def start(r, carry):
        _row_copy(u_ref, r, xs_ref, dest_ref[0, 0, 2 * r], sem).start(priority=0)
        _row_copy(u_ref, r, xs_ref, dest_ref[0, 0, 2 * r + 1], sem).start(priority=1)
        return carry

    def wait(r, carry):
        _row_copy(u_ref, r, xs_ref, dest_ref[0, 0, 2 * r], sem).wait()
        _row_copy(u_ref, r, xs_ref, dest_ref[0, 0, 2 * r + 1], sem).wait()
        return carry

    lax.fori_loop(0, t, start, 0, unroll=DMA_UNROLL)
    lax.fori_loop(0, t, wait, 0, unroll=DMA_UNROLL)


def _dispatch(dest3, has_padding, u, t):
    n, d = u.shape
    n_blocks = has_padding.shape[1]
    return pl.pallas_call(
        _dispatch_kernel,
        grid=(n // t,),
        in_specs=[pl.BlockSpec((1, 1, 2 * t), lambda i: (i, 0, 0), memory_space=pltpu.SMEM),
                  pl.BlockSpec((1, n_blocks), lambda i: (0, 0), memory_space=pltpu.SMEM),
                  pl.BlockSpec((t, d), lambda i: (i, 0))],
        out_specs=pl.BlockSpec(memory_space=pl.ANY),
        out_shape=jax.ShapeDtypeStruct((n_blocks * MOE_BLOCK, d), F32),
        scratch_shapes=[pltpu.VMEM((MOE_BLOCK, d), F32), pltpu.SemaphoreType.DMA(()), pltpu.SemaphoreType.DMA(())],
        compiler_params=_params("arbitrary"),
        name="moe_dispatch",
    )(dest3, has_padding, u)


def _expert_kernel(be_ref, nu_ref, xs_ref, wg_ref, wu_ref, wd_ref, y_ref, wg_bf, wu_bf, wd_bf):
    j = pl.program_id(0)
    used = j < nu_ref[0]
    new_expert = (j == 0) | (be_ref[j] != be_ref[jnp.maximum(j - 1, 0)])

    @pl.when(used & new_expert)
    def _():
        wg_bf[...] = wg_ref[0].astype(BF16)
        wu_bf[...] = wu_ref[0].astype(BF16)
        wd_bf[...] = wd_ref[0].astype(BF16)

    @pl.when(used)
    def _():
        x = xs_ref[...].astype(BF16)
        hid = _silu(jnp.dot(x, wg_bf[...], preferred_element_type=F32)) * jnp.dot(
            x, wu_bf[...], preferred_element_type=F32)
        y_ref[...] = _dot(hid, wd_bf[...])

    @pl.when(jnp.logical_not(used))
    def _():
        y_ref[...] = jnp.zeros_like(y_ref)


def _experts(block_e, n_used, xs, w_gate, w_up, w_down):
    rows, d = xs.shape
    de = w_gate.shape[2]
    grid_spec = pltpu.PrefetchScalarGridSpec(
        num_scalar_prefetch=2,
        grid=(rows // MOE_BLOCK,),
        in_specs=[pl.BlockSpec((MOE_BLOCK, d), lambda j, be, nu: (jnp.minimum(j, nu[0] - 1), 0)),
                  pl.BlockSpec((1, d, de), lambda j, be, nu: (be[j], 0, 0)),
                  pl.BlockSpec((1, d, de), lambda j, be, nu: (be[j], 0, 0)),
                  pl.BlockSpec((1, de, d), lambda j, be, nu: (be[j], 0, 0))],
        out_specs=pl.BlockSpec((MOE_BLOCK, d), lambda j, be, nu: (j, 0)),
        scratch_shapes=[pltpu.VMEM((d, de), BF16), pltpu.VMEM((d, de), BF16), pltpu.VMEM((de, d), BF16)],
    )
    return pl.pallas_call(
        _expert_kernel,
        grid_spec=grid_spec,
        out_shape=jax.ShapeDtypeStruct((rows, d), F32),
        compiler_params=_params("arbitrary"),
        name="moe_experts",
    )(block_e, n_used, xs, w_gate, w_up, w_down)


def _combine_kernel(dest_ref, h_ref, wt_ref, nw_ref, y_ref, o_ref, buf, sem, *, final_norm):
    t = h_ref.shape[0]

    def start(r, carry):
        _row_copy(y_ref, dest_ref[0, 0, 2 * r], buf.at[0], r, sem).start(priority=0)
        _row_copy(y_ref, dest_ref[0, 0, 2 * r + 1], buf.at[1], r, sem).start(priority=1)
        return carry

    def wait(r, carry):
        _row_copy(y_ref, dest_ref[0, 0, 2 * r], buf.at[0], r, sem).wait()
        _row_copy(y_ref, dest_ref[0, 0, 2 * r + 1], buf.at[1], r, sem).wait()
        return carry

    lax.fori_loop(0, t, start, 0, unroll=DMA_UNROLL)
    lax.fori_loop(0, t, wait, 0, unroll=DMA_UNROLL)
    wt = wt_ref[...]
    moe = buf[0] * wt[:, 0:1] + buf[1] * wt[:, 1:2]
    out = h_ref[...] + moe
    o_ref[...] = _rmsnorm(out, nw_ref[...]) if final_norm else out


def _combine(dest3, h, wt, nw, y, t, final_norm):
    n, d = h.shape
    return pl.pallas_call(
        functools.partial(_combine_kernel, final_norm=final_norm),
        grid=(n // t,),
        in_specs=[pl.BlockSpec((1, 1, 2 * t), lambda i: (i, 0, 0), memory_space=pltpu.SMEM),
                  pl.BlockSpec((t, d), lambda i: (i, 0)),
                  pl.BlockSpec((t, LANES), lambda i: (i, 0)),
                  pl.BlockSpec((1, d), lambda i: (0, 0)),
                  pl.BlockSpec(memory_space=pl.ANY)],
        out_specs=pl.BlockSpec((t, d), lambda i: (i, 0)),
        out_shape=jax.ShapeDtypeStruct((n, d), F32),
        scratch_shapes=[pltpu.VMEM((2, t, d), F32), pltpu.SemaphoreType.DMA(())],
        compiler_params=_params("arbitrary"),
        name="moe_combine",
    )(dest3, h, wt, nw, y)


def _tile(n, pref):
    return pref if n % pref == 0 else n


def kernel(x, mem, norm1_w, w_in, rnn_conv_w, rnn_conv_b, rglru_wa, rglru_ba, rglru_wx, rglru_bx, rglru_lambda, w_branch_a, dn_conv_w, dn_a_log, dn_dt_bias, dn_norm_w, w_branch_b, w_out, norm2_w, mem_norm_w, w_cq, w_ckv, w_co, norm3_w, w_router_group, b_router_group, w_router_expert, b_router_expert, w_exp_gate, w_exp_up, w_exp_down, norm_f_w):
    bsz, seq, d = x.shape
    n = bsz * seq
    n_mem = mem.shape[1]
    depth = w_in.shape[0]
    d_rnn = rnn_conv_w.shape[2]
    n_heads = dn_a_log.shape[1]
    dn_w = n_heads * HEAD_DIM
    row = lambda v: v.reshape(1, -1).astype(F32)

    h = x.reshape(n, d)
    mem2 = mem.reshape(bsz * n_mem, d)
    tm = _tile(n, 2048)
    for l in range(depth):
        o_rg, o_qkv, o_z = d_rnn, 2 * d_rnn, 2 * d_rnn + 3 * dn_w
        o_a = o_z + dn_w
        o_ga = o_a + 2 * n_heads
        wi = w_in[l]
        w_cat = jnp.concatenate([wi[:, :o_a], wi[:, o_ga:]], axis=1).astype(BF16)
        w_ab = jnp.pad(wi[:, o_a:o_ga], ((0, 0), (0, LANES - 2 * n_heads))).astype(BF16)
        proj_a, proj_b, ab = _in_proj(h, row(norm1_w[l]), w_cat, w_ab, o_qkv, tm, 512)

        gated_a = _rglru(proj_a, bsz, seq, rnn_conv_w[l], row(rnn_conv_b[l]), rglru_wa[l].astype(BF16),
                         row(rglru_ba[l]), rglru_wx[l].astype(BF16), row(rglru_bx[l]), row(rglru_lambda[l]),
                         _tile(seq, 512))

        a_dec = jnp.exp(dn_a_log[l].astype(F32))
        pad_h = lambda v: jnp.pad(v, (0, LANES - n_heads))
        acol, dcol = row(pad_h(a_dec)), row(pad_h(dn_dt_bias[l]))
        arow = jnp.broadcast_to(jnp.pad(a_dec, (0, n_heads))[:, None], (2 * n_heads, GDN_CHUNK))
        drow = jnp.broadcast_to(jnp.pad(dn_dt_bias[l], (0, n_heads))[:, None], (2 * n_heads, GDN_CHUNK))
        abt = ab[:, :2 * n_heads].T
        gated_b = _gdn(proj_b, ab, abt, bsz, seq, dn_conv_w[l], acol, dcol, arow, drow, row(dn_norm_w[l]), n_heads)

        kv = _norm_mm(mem2, row(mem_norm_w[l]), w_ckv[l].astype(BF16), BF16, _tile(bsz * n_mem, 1024), 512)
        h = _merge_cross(h, gated_a, gated_b, proj_b, w_branch_a[l].astype(BF16), w_branch_b[l].astype(BF16),
                         w_out[l].astype(BF16), row(norm2_w[l]), w_cq[l].astype(BF16), kv, w_co[l].astype(BF16),
                         bsz, seq, n_mem, _tile(seq, 512))

        w_r = jnp.pad(jnp.concatenate([w_router_expert[l], w_router_group[l]], axis=1),
                      ((0, 0), (0, LANES - N_EXPERTS - N_GROUPS)))
        b_r = row(jnp.pad(jnp.concatenate([b_router_expert[l], b_router_group[l]]), (0, LANES - N_EXPERTS - N_GROUPS)))
        u3, idx, wt, cnt = _router(h, row(norm3_w[l]), w_r, b_r, _tile(n, 512))

        counts = cnt[0, :N_EXPERTS].astype(I32)
        padded = (counts + MOE_BLOCK - 1) // MOE_BLOCK * MOE_BLOCK
        pend = jnp.cumsum(padded)
        pstart = pend - padded
        is_expert = idx[:, 0:2, None] == jnp.arange(N_EXPERTS, dtype=I32)
        dest = jnp.sum(jnp.where(is_expert, pstart, 0), axis=-1) + idx[:, 2:4]
        n_blocks = (2 * n + N_EXPERTS * (MOE_BLOCK - 1)) // MOE_BLOCK
        block_row = jnp.arange(n_blocks, dtype=I32) * MOE_BLOCK
        block_e = jnp.minimum(jnp.sum((pend[None, :] <= block_row[:, None]).astype(I32), axis=1), N_EXPERTS - 1)
        n_used = (pend[-1:] // MOE_BLOCK).astype(I32)
        t_moe = _tile(n, 512)
        dest3 = dest.reshape(n // t_moe, 1, 2 * t_moe)
        block_id = jnp.arange(n_blocks, dtype=I32)
        next_e = block_e[jnp.minimum(block_id + 1, n_blocks - 1)]
        has_padding = ((block_id >= n_used[0] - 1) | (block_e != next_e)).astype(I32).reshape(1, n_blocks)
        xs = _dispatch(dest3, has_padding, u3, t_moe)
        yb = _experts(block_e, n_used, xs, w_exp_gate[l], w_exp_up[l], w_exp_down[l])
        h = _combine(dest3, h, wt, row(norm_f_w), yb, t_moe, final_norm=(l == depth - 1))
    return h.reshape(bsz, seq, d)
```
